```python
import math
import jax
import jax.numpy as jnp
from jax import lax
import numpy as np

D_MODEL = 1024
BATCH = 16
SEQ = 2048
DEPTH = 1

MEM_LEN = 256
EPS = 1e-6
DA_HEADS = 4
DA_DIM = 64
DA_VDIM = 2 * DA_DIM
GLA_HEADS = 4
GLA_DK = 64
GLA_DV = 128
GLA_RANK = 16
GLA_TAU = 16.0
GLA_CHUNK = 64
MIX_WIDTH = DA_HEADS * DA_VDIM + GLA_HEADS * GLA_DV
ROPE_THETA = 500000.0
ROPE_DIM = DA_DIM // 4
Q_BLOCK = 128
X_HEADS = 4
X_DIM = D_MODEL // X_HEADS
N_EXPERTS = 32
TOP_K = 4
D_FF = D_MODEL
SWIGLU_LIMIT = 7.0
SWIGLU_ALPHA = 1.702
MOE_BLOCK = 128
DA_QK = DA_HEADS * 2 * DA_DIM
DA_V = DA_HEADS * DA_VDIM
G_QK = GLA_HEADS * GLA_DK
G_V = GLA_HEADS * GLA_DV
SPLITS = (DA_QK, DA_QK, DA_V, G_QK, G_QK, G_V, G_V, GLA_RANK)
IN_WIDTH = 2 * DA_QK + DA_V + 2 * G_QK + 2 * G_V + GLA_RANK

kernel_name = 'hymba_style_diffattn_gla_moe_block'


def rms_norm(x, g):
    xf = x.astype(jnp.float32)
    y = xf * lax.rsqrt(jnp.mean(xf * xf, axis=-1, keepdims=True) + EPS)
    return (y * g.astype(jnp.float32)).astype(x.dtype)


def partial_rope(t, positions):
    half = ROPE_DIM // 2
    inv_freq = ROPE_THETA ** (-jnp.arange(0, ROPE_DIM, 2, dtype=jnp.float32) / ROPE_DIM)
    ang = positions.astype(jnp.float32)[..., None] * inv_freq
    cos = jnp.cos(ang)[:, :, None, None, :]
    sin = jnp.sin(ang)[:, :, None, None, :]
    tf = t.astype(jnp.float32)
    x1, x2 = tf[..., :half], tf[..., half:ROPE_DIM]
    rot = jnp.concatenate([x1 * cos - x2 * sin, x2 * cos + x1 * sin, tf[..., ROPE_DIM:]], axis=-1)
    return rot.astype(t.dtype)


def diff_attention(q, k, v, lam, positions):
    S = q.shape[1]
    q = partial_rope(q, positions) * (DA_DIM ** -0.5)
    k = partial_rope(k, positions)
    outs = []
    for i in range(S // Q_BLOCK):
        q0 = i * Q_BLOCK
        kend = q0 + Q_BLOCK
        s = jnp.einsum('bqhcd,bkhcd->bhcqk', q[:, q0:kend], k[:, :kend]).astype(jnp.float32)
        mask = (q0 + jnp.arange(Q_BLOCK))[:, None] >= jnp.arange(kend)[None, :]
        p = jax.nn.softmax(jnp.where(mask, s, -jnp.inf), axis=-1)
        a = p[:, :, 0] - lam * p[:, :, 1]
        outs.append(jnp.einsum('bhqk,bkhe->bqhe', a.astype(v.dtype), v[:, :kend]))
    return jnp.concatenate(outs, axis=1)


def gla_chunked(q, k, v, log_a):
    B, S, H, dk = q.shape
    dv = v.shape[-1]
    C = GLA_CHUNK
    n = S // C

    def to_chunks(t):
        return t.reshape(B, n, C, H, t.shape[-1]).transpose(1, 0, 3, 2, 4)

    qc, kc, vc, gc = (to_chunks(t) for t in (q * (dk ** -0.5), k, v, log_a))
    causal = jnp.tril(jnp.ones((C, C), dtype=bool))[:, :, None]

    def step(state, inp):
        qi, ki, vi, gi = inp
        b = jnp.cumsum(gi, axis=2)
        inter = jnp.einsum('bhtd,bhde->bhte', qi * jnp.exp(b), state)
        diff = b[:, :, :, None, :] - b[:, :, None, :, :]
        decay = jnp.exp(jnp.where(causal, diff, -jnp.inf))
        attn = jnp.einsum('bhtd,bhsd,bhtsd->bhts', qi, ki, decay)
        intra = jnp.einsum('bhts,bhse->bhte', attn, vi)
        b_last = b[:, :, -1:, :]
        new_state = state * jnp.exp(b_last[:, :, 0, :, None]) + jnp.einsum(
            'bhsd,bhse->bhde', ki * jnp.exp(b_last - b), vi)
        return new_state, inter + intra

    state0 = jnp.zeros((B, H, dk, dv), jnp.float32)
    _, out = lax.scan(step, state0, (qc, kc, vc, gc))
    return out.transpose(1, 0, 3, 2, 4).reshape(B, S, H, dv)


def hybrid_mixer(h, positions, w_in, lambda_q1, lambda_k1, lambda_q2, lambda_k2,
                 diff_norm_g, w_alpha2, b_alpha, gla_norm_g, w_out, lambda_init):
    B, S, _ = h.shape
    proj = h @ w_in
    idx = [int(c) for c in np.cumsum(SPLITS)[:-1]]
    dq, dk_, dv_, gq, gk, gv, gr, ga = jnp.split(proj, idx, axis=-1)
    lam = (jnp.exp(jnp.sum(lambda_q1.astype(jnp.float32) * lambda_k1.astype(jnp.float32)))
           - jnp.exp(jnp.sum(lambda_q2.astype(jnp.float32) * lambda_k2.astype(jnp.float32)))
           + lambda_init)
    o_da = diff_attention(dq.reshape(B, S, DA_HEADS, 2, DA_DIM),
                          dk_.reshape(B, S, DA_HEADS, 2, DA_DIM),
                          dv_.reshape(B, S, DA_HEADS, DA_VDIM), lam, positions)
    o_da = (rms_norm(o_da, diff_norm_g) * (1.0 - lambda_init)).reshape(B, S, DA_V)
    log_a = jax.nn.log_sigmoid((ga @ w_alpha2 + b_alpha).astype(jnp.float32)) / GLA_TAU
    o_gla = gla_chunked(gq.reshape(B, S, GLA_HEADS, GLA_DK).astype(jnp.float32),
                        gk.reshape(B, S, GLA_HEADS, GLA_DK).astype(jnp.float32),
                        gv.reshape(B, S, GLA_HEADS, GLA_DV).astype(jnp.float32),
                        log_a.reshape(B, S, GLA_HEADS, GLA_DK))
    o_gla = rms_norm(o_gla, gla_norm_g) * jax.nn.silu(gr.reshape(B, S, GLA_HEADS, GLA_DV).astype(jnp.float32))
    o_gla = o_gla.reshape(B, S, G_V).astype(h.dtype)
    return jnp.concatenate([o_da, o_gla], axis=-1) @ w_out


def memory_cross_attention(h, m, w_cq, w_ck, w_cv, w_co):
    B, S, D = h.shape
    M = m.shape[1]
    q = (h @ w_cq).reshape(B, S, X_HEADS, X_DIM)
    k = (m @ w_ck).reshape(B, M, X_HEADS, X_DIM)
    v = (m @ w_cv).reshape(B, M, X_HEADS, X_DIM)
    s = jnp.einsum('bqhd,bkhd->bhqk', q, k).astype(jnp.float32) * (X_DIM ** -0.5)
    p = jax.nn.softmax(s, axis=-1)
    o = jnp.einsum('bhqk,bkhd->bqhd', p.astype(v.dtype), v).reshape(B, S, D)
    return o @ w_co


def moe_ffn(h, w_router, b_router, w_up, b_up, w_down, b_down):
    B, S, D = h.shape
    N = B * S
    A = N * TOP_K
    t = h.reshape(N, D)
    logits = (t @ w_router + b_router).astype(jnp.float32)
    top_logit, top_e = lax.top_k(logits, TOP_K)
    gate = jax.nn.softmax(top_logit, axis=-1).reshape(A)
    flat_e = top_e.reshape(A)
    order = jnp.argsort(flat_e)
    sorted_e = flat_e[order]
    tok = order // TOP_K
    counts = jnp.bincount(flat_e, length=N_EXPERTS)
    padded = (counts + MOE_BLOCK - 1) // MOE_BLOCK * MOE_BLOCK
    start = jnp.cumsum(counts) - counts
    pad_end = jnp.cumsum(padded)
    pad_start = pad_end - padded
    dest = pad_start[sorted_e] + jnp.arange(A) - start[sorted_e]
    cap = A + N_EXPERTS * MOE_BLOCK
    n_blk = cap // MOE_BLOCK
    x_pad = jnp.zeros((cap, D), t.dtype).at[dest].set(t[tok])
    blk_e = jnp.minimum(jnp.searchsorted(pad_end, jnp.arange(n_blk) * MOE_BLOCK, side='right'),
                        N_EXPERTS - 1)

    def expert_block(args):
        xb, e = args
        u = xb @ w_up[e] + b_up[e]
        glu = jnp.minimum(u[:, :D_FF], SWIGLU_LIMIT)
        lin = jnp.clip(u[:, D_FF:], -SWIGLU_LIMIT, SWIGLU_LIMIT)
        act = glu * jax.nn.sigmoid(SWIGLU_ALPHA * glu) * (lin + 1.0)
        return act @ w_down[e] + b_down[e]

    y_pad = lax.map(expert_block, (x_pad.reshape(n_blk, MOE_BLOCK, D), blk_e))
    y = y_pad.reshape(cap, D)[dest] * gate[order][:, None].astype(t.dtype)
    return jax.ops.segment_sum(y, tok, num_segments=N).reshape(B, S, D)


def setup_inputs(seed: int = 0) -> dict:
    key = jax.random.key(seed)
    ks = jax.random.split(key, 32)
    L = DEPTH
    f32 = jnp.float32

    def nrm(k, shape, scale):
        return jax.random.normal(k, shape, f32) * scale

    def gain(k, shape):
        return 1.0 + 0.02 * jax.random.normal(k, shape, f32)

    return {
        'x': nrm(ks[0], (BATCH, SEQ, D_MODEL), 1.0),
        'mem': nrm(ks[1], (BATCH, MEM_LEN, D_MODEL), 1.0),
        'positions': (jnp.arange(SEQ, dtype=jnp.int32)[None, :]
                      + jax.random.randint(ks[2], (BATCH, 1), 0, 4096, dtype=jnp.int32)),
        'norm_mix_g': gain(ks[3], (L, D_MODEL)),
        'w_in': nrm(ks[4], (L, D_MODEL, IN_WIDTH), D_MODEL ** -0.5),
        'lambda_q1': nrm(ks[5], (L, DA_DIM), 0.1),
        'lambda_k1': nrm(ks[6], (L, DA_DIM), 0.1),
        'lambda_q2': nrm(ks[7], (L, DA_DIM), 0.1),
        'lambda_k2': nrm(ks[8], (L, DA_DIM), 0.1),
        'diff_norm_g': gain(ks[9], (L, DA_VDIM)),
        'w_alpha2': nrm(ks[10], (L, GLA_RANK, G_QK), GLA_RANK ** -0.5),
        'b_alpha': nrm(ks[11], (L, G_QK), 0.1),
        'gla_norm_g': gain(ks[12], (L, GLA_DV)),
        'w_out': nrm(ks[13], (L, MIX_WIDTH, D_MODEL), MIX_WIDTH ** -0.5),
        'norm_cross_g': gain(ks[14], (L, D_MODEL)),
        'norm_mem_g': gain(ks[15], (L, D_MODEL)),
        'w_cq': nrm(ks[16], (L, D_MODEL, D_MODEL), D_MODEL ** -0.5),
        'w_ck': nrm(ks[17], (L, D_MODEL, D_MODEL), D_MODEL ** -0.5),
        'w_cv': nrm(ks[18], (L, D_MODEL, D_MODEL), D_MODEL ** -0.5),
        'w_co': nrm(ks[19], (L, D_MODEL, D_MODEL), D_MODEL ** -0.5),
        'norm_ffn_g': gain(ks[20], (L, D_MODEL)),
        'w_router': nrm(ks[21], (L, D_MODEL, N_EXPERTS), D_MODEL ** -0.5),
        'b_router': nrm(ks[22], (L, N_EXPERTS), 0.01),
        'w_up': nrm(ks[23], (L, N_EXPERTS, D_MODEL, 2 * D_FF), D_MODEL ** -0.5),
        'b_up': nrm(ks[24], (L, N_EXPERTS, 2 * D_FF), 0.01),
        'w_down': nrm(ks[25], (L, N_EXPERTS, D_FF, D_MODEL), D_FF ** -0.5),
        'b_down': nrm(ks[26], (L, N_EXPERTS, D_MODEL), 0.01),
        'norm_final_g': gain(ks[27], (D_MODEL,)),
    }


def reference(x, mem, positions, norm_mix_g, w_in, lambda_q1, lambda_k1, lambda_q2, lambda_k2,
              diff_norm_g, w_alpha2, b_alpha, gla_norm_g, w_out, norm_cross_g, norm_mem_g,
              w_cq, w_ck, w_cv, w_co, norm_ffn_g, w_router, b_router, w_up, b_up, w_down, b_down,
              norm_final_g):
    for l in range(DEPTH):
        lambda_init = 0.8 - 0.6 * math.exp(-0.3 * l)
        x = x + hybrid_mixer(rms_norm(x, norm_mix_g[l]), positions, w_in[l],
                             lambda_q1[l], lambda_k1[l], lambda_q2[l], lambda_k2[l],
                             diff_norm_g[l], w_alpha2[l], b_alpha[l], gla_norm_g[l], w_out[l],
                             lambda_init)
        x = x + memory_cross_attention(rms_norm(x, norm_cross_g[l]), rms_norm(mem, norm_mem_g[l]),
                                       w_cq[l], w_ck[l], w_cv[l], w_co[l])
        x = x + moe_ffn(rms_norm(x, norm_ffn_g[l]), w_router[l], b_router[l],
                        w_up[l], b_up[l], w_down[l], b_down[l])
    return rms_norm(x, norm_final_g)
```

```python
import functools

import numpy as np
import jax
import jax.numpy as jnp
from jax import lax
from jax.experimental import pallas as pl
from jax.experimental.pallas import tpu as pltpu

EPS = 1e-6
D_MODEL = 1024
DA_HEADS = 4
DA_DIM = 64
GLA_HEADS = 4
GLA_DK = 64
GLA_DV = 128
GLA_RANK = 16
GLA_TAU = 16.0
GLA_CHUNK = 64
ROPE_THETA = 500000.0
ROPE_DIM = DA_DIM // 4
X_HEADS = 4
X_DIM = D_MODEL // X_HEADS
N_EXPERTS = 32
TOP_K = 4
D_FF = D_MODEL
SWIGLU_LIMIT = 7.0
SWIGLU_ALPHA = 1.702
LAMBDA_INIT = 0.8 - 0.6 * 1.0

LANES = 128
BF16 = jnp.bfloat16
F32 = jnp.float32
VMEM_LIMIT = 48 * 1024 * 1024


def _params(sem, vmem=VMEM_LIMIT):
    return pltpu.CompilerParams(dimension_semantics=sem, vmem_limit_bytes=vmem)


def _rms(xf, g):
    return xf * lax.rsqrt(jnp.mean(xf * xf, axis=-1, keepdims=True) + EPS) * g


def _dot(a, b):
    return jnp.dot(a, b, preferred_element_type=F32)


def _dot_nt(a, b):
    return lax.dot_general(a, b, (((1,), (1,)), ((), ())), preferred_element_type=F32)


def _split3(a):
    hi = a.astype(BF16)
    r1 = a - hi.astype(F32)
    mid = r1.astype(BF16)
    lo = (r1 - mid.astype(F32)).astype(BF16)
    return hi, mid, lo


def _inproj_kernel(x_ref, pos_ref, invf_ref, g_ref, wqk_ref, wv_ref, wg_ref, wga_ref,
                   wa2_ref, ba_ref,
                   q_ref, k_ref, v_ref, gq_ref, gk_ref, gv_ref, gr_ref, la_ref):
    h = _rms(x_ref[...], g_ref[...]).astype(BF16)
    ang = pos_ref[...].astype(F32) * invf_ref[...]
    lane = lax.broadcasted_iota(jnp.int32, ang.shape, 1) % DA_DIM
    half = ROPE_DIM // 2
    cosv = jnp.where(lane < ROPE_DIM, jnp.cos(ang), 1.0)
    sinv = jnp.sin(ang)
    s_lo = jnp.where(lane < half, -sinv, 0.0)
    s_hi = jnp.where((lane >= half) & (lane < ROPE_DIM), sinv, 0.0)

    qk = _dot(h, wqk_ref[...])
    n_grp = qk.shape[1] // LANES
    for j in range(n_grp):
        t = qk[:, j * LANES:(j + 1) * LANES]
        rot = (t * cosv + pltpu.roll(t, LANES - half, 1) * s_lo
               + pltpu.roll(t, half, 1) * s_hi)
        if j < n_grp // 2:
            q_ref[:, j * LANES:(j + 1) * LANES] = (rot * (DA_DIM ** -0.5)).astype(BF16)
        else:
            jj = j - n_grp // 2
            k_ref[:, jj * LANES:(jj + 1) * LANES] = rot.astype(BF16)
    v_ref[...] = _dot(h, wv_ref[...]).astype(BF16)
    gg = _dot(h, wg_ref[...])
    gq_ref[...] = gg[:, 0:256]
    gk_ref[...] = gg[:, 256:512]
    gv_ref[...] = gg[:, 512:1024]
    gr_ref[...] = gg[:, 1024:1536]
    ga = _dot(h, wga_ref[...])
    z = _dot(ga.astype(BF16), wa2_ref[...]) + ba_ref[...]
    la_ref[...] = (jnp.minimum(z, 0.0) - jnp.log1p(jnp.exp(-jnp.abs(z)))) * (1.0 / GLA_TAU)


def _inproj(x2d, pos2d, invf, g, wqk, wv, wg, wga, wa2, ba, tm=256):
    n = x2d.shape[0]
    row = lambda w: pl.BlockSpec((tm, w), lambda i: (i, 0))
    full = lambda a: pl.BlockSpec(a.shape, lambda i: (0,) * a.ndim)
    outs = [(512, BF16), (512, BF16), (512, BF16), (256, F32), (256, F32), (512, F32),
            (512, F32), (256, F32)]
    return pl.pallas_call(
        _inproj_kernel,
        grid=(n // tm,),
        in_specs=[row(D_MODEL), row(1), full(invf), full(g), full(wqk), full(wv), full(wg),
                  full(wga), full(wa2), full(ba)],
        out_specs=[row(w) for w, _ in outs],
        out_shape=[jax.ShapeDtypeStruct((n, w), dt) for w, dt in outs],
        compiler_params=_params(("arbitrary",)),
        name="inproj",
    )(x2d, pos2d, invf, g, wqk, wv, wg, wga, wa2, ba)


def _diffattn_kernel(lq1_ref, lk1_ref, lq2_ref, lk2_ref, gn_ref, q_ref, k_ref, v_ref, o_ref,
                     m0_ref, l0_ref, a0_ref, m1_ref, l1_ref, a1_ref, *, tq):
    qi = pl.program_id(2)
    q = q_ref[...]
    lane = lax.broadcasted_iota(jnp.int32, q.shape, 1)
    zero = jnp.zeros_like(q)
    qc = (jnp.where(lane < DA_DIM, q, zero), jnp.where(lane >= DA_DIM, q, zero))
    stats = ((m0_ref, l0_ref, a0_ref), (m1_ref, l1_ref, a1_ref))
    for m_ref, l_ref, a_ref in stats:
        m_ref[...] = jnp.full_like(m_ref, -1e30)
        l_ref[...] = jnp.zeros_like(l_ref)
        a_ref[...] = jnp.zeros_like(a_ref)

    def block(kb, masked):
        start = pl.multiple_of(kb * tq, tq)
        kblk = k_ref[pl.ds(start, tq), :]
        vblk = v_ref[pl.ds(start, tq), :]
        for c in range(2):
            m_ref, l_ref, a_ref = stats[c]
            s = _dot_nt(qc[c], kblk)
            if masked:
                r = lax.broadcasted_iota(jnp.int32, s.shape, 0)
                cidx = lax.broadcasted_iota(jnp.int32, s.shape, 1)
                s = jnp.where(cidx <= r, s, -jnp.inf)
            m_old = m_ref[...]
            m_new = jnp.maximum(m_old, jnp.max(s, axis=-1, keepdims=True))
            alpha = jnp.exp(m_old - m_new)
            p = jnp.exp(s - m_new)
            l_ref[...] = alpha * l_ref[...] + jnp.sum(p, axis=-1, keepdims=True)
            a_ref[...] = alpha * a_ref[...] + _dot(p.astype(BF16), vblk)
            m_ref[...] = m_new

    def body(kb, carry):
        block(kb, False)
        return carry

    lax.fori_loop(0, qi, body, 0)
    block(qi, True)

    lam = (jnp.exp(jnp.sum(lq1_ref[...] * lk1_ref[...], axis=-1, keepdims=True))
           - jnp.exp(jnp.sum(lq2_ref[...] * lk2_ref[...], axis=-1, keepdims=True))
           + LAMBDA_INIT)
    o = a0_ref[...] / l0_ref[...] - lam * (a1_ref[...] / l1_ref[...])
    o_ref[...] = (_rms(o, gn_ref[...]) * (1.0 - LAMBDA_INIT)).astype(o_ref.dtype)


def _diffattn(q, k, v, lq1, lk1, lq2, lk2, gn, batch, seq, tq=256):
    nq = seq // tq
    vec = lambda a: pl.BlockSpec(a.shape, lambda b, h, i: (0, 0))
    return pl.pallas_call(
        functools.partial(_diffattn_kernel, tq=tq),
        grid=(batch, DA_HEADS, nq),
        in_specs=[vec(lq1), vec(lk1), vec(lq2), vec(lk2), vec(gn),
                  pl.BlockSpec((tq, LANES), lambda b, h, i: (b * nq + i, h)),
                  pl.BlockSpec((seq, LANES), lambda b, h, i: (b, h)),
                  pl.BlockSpec((seq, LANES), lambda b, h, i: (b, h))],
        out_specs=pl.BlockSpec((tq, LANES), lambda b, h, i: (b * nq + i, h)),
        out_shape=jax.ShapeDtypeStruct(q.shape, BF16),
        scratch_shapes=[pltpu.VMEM((tq, 1), F32), pltpu.VMEM((tq, 1), F32),
                        pltpu.VMEM((tq, LANES), F32)] * 2,
        compiler_params=_params(("arbitrary", "arbitrary", "arbitrary")),
        name="diffattn",
    )(lq1, lk1, lq2, lk2, gn, q, k, v)


def _gla_kernel(gq_ref, gk_ref, gv_ref, gr_ref, la_ref, gn_ref, o_ref, st_ref, *, rows):
    c = GLA_CHUNK
    pw = 2 * GLA_DK
    vw = 2 * GLA_DV

    @pl.when(pl.program_id(1) == 0)
    def _():
        st_ref[...] = jnp.zeros_like(st_ref)

    r_i = lax.broadcasted_iota(jnp.int32, (c, c), 0)
    c_i = lax.broadcasted_iota(jnp.int32, (c, c), 1)
    causal = c_i <= r_i
    tril = causal.astype(BF16)
    sr = lax.broadcasted_iota(jnp.int32, (pw, vw), 0) // GLA_DK
    sc = lax.broadcasted_iota(jnp.int32, (pw, vw), 1) // GLA_DV
    blockdiag = sr == sc
    klane = lax.broadcasted_iota(jnp.int32, (c, pw), 1) // GLA_DK

    for ci in range(rows // c):
        rs = slice(ci * c, (ci + 1) * c)
        for p in range(GLA_HEADS // 2):
            ks = slice(p * pw, (p + 1) * pw)
            vs = slice(p * vw, (p + 1) * vw)
            la = la_ref[rs, ks]
            hi, mid, lo = _split3(la)
            b = _dot(tril, hi) + _dot(tril, mid) + _dot(tril, lo)
            b_last = b[c - 1:c, :]
            qe = gq_ref[rs, ks] * jnp.exp(b) * (GLA_DK ** -0.5)
            gk = gk_ref[rs, ks]
            kn = (gk * jnp.exp(-b)).astype(BF16)
            kd = gk * jnp.exp(b_last - b)
            v = gv_ref[rs, vs].astype(BF16)
            st = st_ref[p]
            inter = _dot(qe.astype(BF16), st.astype(BF16))
            outs = []
            for hh in range(2):
                qh = jnp.where(klane == hh, qe, 0.0).astype(BF16)
                attn = jnp.where(causal, _dot_nt(qh, kn), 0.0)
                outs.append(_dot(attn.astype(BF16), v[:, hh * GLA_DV:(hh + 1) * GLA_DV]))
            o = inter + jnp.concatenate(outs, axis=1)
            upd = _dot(kd.T.astype(BF16), v)
            decay = jnp.exp(b_last).T
            st_ref[p] = st * decay + jnp.where(blockdiag, upd, 0.0)
            for hh in range(2):
                hs = slice(hh * GLA_DV, (hh + 1) * GLA_DV)
                os_ = slice(p * vw + hh * GLA_DV, p * vw + (hh + 1) * GLA_DV)
                gr = gr_ref[rs, os_]
                y = _rms(o[:, hs], gn_ref[...]) * (gr * jax.nn.sigmoid(gr))
                o_ref[rs, os_] = y.astype(o_ref.dtype)


def _gla(gq, gk, gv, gr, la, gn, batch, seq, rows=256):
    nb = seq // rows
    row = lambda w: pl.BlockSpec((rows, w), lambda b, i: (b * nb + i, 0))
    return pl.pallas_call(
        functools.partial(_gla_kernel, rows=rows),
        grid=(batch, nb),
        in_specs=[row(256), row(256), row(512), row(512), row(256),
                  pl.BlockSpec(gn.shape, lambda b, i: (0, 0))],
        out_specs=row(512),
        out_shape=jax.ShapeDtypeStruct(gv.shape, BF16),
        scratch_shapes=[pltpu.VMEM((GLA_HEADS // 2, 2 * GLA_DK, 2 * GLA_DV), F32)],
        compiler_params=_params(("arbitrary", "arbitrary")),
        name="gla",
    )(gq, gk, gv, gr, la, gn)


def _outproj_kernel(oda_ref, ogla_ref, x_ref, wo_ref, g_ref, wcq_ref, x1_ref, qc_ref):
    half = oda_ref.shape[1]
    mix = _dot(oda_ref[...], wo_ref[0:half, :]) + _dot(ogla_ref[...], wo_ref[half:, :])
    x1 = x_ref[...] + mix
    x1_ref[...] = x1
    hq = _rms(x1, g_ref[...]).astype(BF16)
    qc_ref[...] = (_dot(hq, wcq_ref[...]) * (X_DIM ** -0.5)).astype(BF16)


def _outproj(oda, ogla, x2d, wo, g, wcq, tm=256):
    n = x2d.shape[0]
    row = lambda w: pl.BlockSpec((tm, w), lambda i: (i, 0))
    full = lambda a: pl.BlockSpec(a.shape, lambda i: (0,) * a.ndim)
    return pl.pallas_call(
        _outproj_kernel,
        grid=(n // tm,),
        in_specs=[row(512), row(512), row(D_MODEL), full(wo), full(g), full(wcq)],
        out_specs=[row(D_MODEL), row(D_MODEL)],
        out_shape=[jax.ShapeDtypeStruct((n, D_MODEL), F32),
                   jax.ShapeDtypeStruct((n, D_MODEL), BF16)],
        compiler_params=_params(("arbitrary",)),
        name="outproj",
    )(oda, ogla, x2d, wo, g, wcq)


def _memkv_kernel(m_ref, g_ref, wk_ref, wv_ref, k_ref, v_ref):
    hm = _rms(m_ref[...], g_ref[...]).astype(BF16)
    k_ref[...] = _dot(hm, wk_ref[...]).astype(BF16)
    v_ref[...] = _dot(hm, wv_ref[...]).astype(BF16)


def _memkv(mem2d, g, wk, wv, tm=256):
    n = mem2d.shape[0]
    row = pl.BlockSpec((tm, D_MODEL), lambda i: (i, 0))
    full = lambda a: pl.BlockSpec(a.shape, lambda i: (0,) * a.ndim)
    return pl.pallas_call(
        _memkv_kernel,
        grid=(n // tm,),
        in_specs=[row, full(g), full(wk), full(wv)],
        out_specs=[row, row],
        out_shape=[jax.ShapeDtypeStruct((n, D_MODEL), BF16)] * 2,
        compiler_params=_params(("arbitrary",)),
        name="memkv",
    )(mem2d, g, wk, wv)


def _cross_kernel(qc_ref, kc_ref, vc_ref, x1_ref, wco_ref, g_ref, wrt_ref, br_ref,
                  x2_ref, t_ref, e_ref, gate_ref, rank_ref, cnt_ref, run_ref):
    @pl.when((pl.program_id(0) == 0) & (pl.program_id(1) == 0))
    def _():
        run_ref[...] = jnp.zeros_like(run_ref)

    tq = qc_ref.shape[0]
    outs = []
    for h in range(X_HEADS):
        hs = slice(h * X_DIM, (h + 1) * X_DIM)
        s = _dot_nt(qc_ref[:, hs], kc_ref[:, hs])
        m = jnp.max(s, axis=-1, keepdims=True)
        p = jnp.exp(s - m)
        l = jnp.sum(p, axis=-1, keepdims=True)
        outs.append((_dot(p.astype(BF16), vc_ref[:, hs]) / l).astype(BF16))
    o = jnp.concatenate(outs, axis=1)
    x2 = x1_ref[...] + _dot(o, wco_ref[...])
    x2_ref[...] = x2
    t = _rms(x2, g_ref[...])
    t_ref[...] = t

    t_hi, t_mid, _ = _split3(t)
    w_hi, w_mid, _ = _split3(wrt_ref[...])
    logit = (_dot_nt(w_hi, t_hi) + _dot_nt(w_hi, t_mid) + _dot_nt(w_mid, t_hi)) + br_ref[...]
    iota_e = lax.broadcasted_iota(jnp.int32, logit.shape, 0)
    vals, idxs, sels = [], [], []
    for _ in range(TOP_K):
        mx = jnp.max(logit, axis=0, keepdims=True)
        idx = jnp.min(jnp.where(logit == mx, iota_e, N_EXPERTS), axis=0, keepdims=True)
        sel = iota_e == idx
        vals.append(mx)
        idxs.append(idx)
        sels.append(sel)
        logit = jnp.where(sel, -jnp.inf, logit)
    ex = [jnp.exp(v - vals[0]) for v in vals]
    den = ex[0] + ex[1] + ex[2] + ex[3]
    gate_ref[...] = jnp.concatenate([e / den for e in ex], axis=0)
    e_ref[...] = jnp.concatenate(idxs, axis=0)
    onehot = (sels[0] | sels[1] | sels[2] | sels[3])
    cnt = onehot.astype(BF16)
    ur = lax.broadcasted_iota(jnp.int32, (tq, tq), 0)
    uc = lax.broadcasted_iota(jnp.int32, (tq, tq), 1)
    before = (ur < uc).astype(BF16)
    base = _dot(cnt, before) + run_ref[...]
    rank_ref[...] = jnp.concatenate(
        [jnp.sum(jnp.where(s_, base, 0.0), axis=0, keepdims=True) for s_ in sels],
        axis=0).astype(jnp.int32)
    run = run_ref[...] + jnp.sum(onehot.astype(F32), axis=1, keepdims=True)
    run_ref[...] = run
    cnt_ref[...] = jnp.broadcast_to(run, cnt_ref.shape)


def _cross(qc, kc, vc, x1, wco, g, wrt, br, batch, seq, mem_len, tq=256):
    nq = seq // tq
    n = batch * seq
    row = lambda w: pl.BlockSpec((tq, w), lambda b, i: (b * nq + i, 0))
    col = lambda r: pl.BlockSpec((r, tq), lambda b, i: (0, b * nq + i))
    full = lambda a: pl.BlockSpec(a.shape, lambda b, i: (0,) * a.ndim)
    memb = pl.BlockSpec((mem_len, D_MODEL), lambda b, i: (b, 0))
    return pl.pallas_call(
        _cross_kernel,
        grid=(batch, nq),
        in_specs=[row(D_MODEL), memb, memb, row(D_MODEL), full(wco), full(g), full(wrt),
                  full(br)],
        out_specs=[row(D_MODEL), row(D_MODEL), col(TOP_K), col(TOP_K), col(TOP_K),
                   pl.BlockSpec((N_EXPERTS, LANES), lambda b, i: (0, 0))],
        out_shape=[jax.ShapeDtypeStruct((n, D_MODEL), F32),
                   jax.ShapeDtypeStruct((n, D_MODEL), F32),
                   jax.ShapeDtypeStruct((TOP_K, n), jnp.int32),
                   jax.ShapeDtypeStruct((TOP_K, n), F32),
                   jax.ShapeDtypeStruct((TOP_K, n), jnp.int32),
                   jax.ShapeDtypeStruct((N_EXPERTS, LANES), F32)],
        scratch_shapes=[pltpu.VMEM((N_EXPERTS, 1), F32)],
        compiler_params=_params(("arbitrary", "arbitrary")),
        name="cross_router",
    )(qc, kc, vc, x1, wco, g, wrt, br)


def _dispatch_kernel(pos_ref, t_ref, xin_ref, xp_ref, sem, *, tm):
    del xin_ref
    base = pl.program_id(0) * tm

    def issue(j, carry):
        for k in range(TOP_K):
            pltpu.make_async_copy(t_ref.at[pl.ds(base + j, 1)],
                                  xp_ref.at[pl.ds(pos_ref[k, j], 1)], sem).start()
        return carry

    lax.fori_loop(0, tm, issue, 0)

    def drain(j, carry):
        pltpu.make_async_copy(t_ref.at[pl.ds(0, 1)], xp_ref.at[pl.ds(0, 1)], sem).wait()
        return carry

    lax.fori_loop(0, tm * TOP_K, drain, 0)


def _dispatch(pos, t, x_pad_init, tm=512):
    n = t.shape[0]
    return pl.pallas_call(
        functools.partial(_dispatch_kernel, tm=tm),
        grid=(n // tm,),
        in_specs=[pl.BlockSpec((TOP_K, tm), lambda i: (0, i), memory_space=pltpu.SMEM),
                  pl.BlockSpec(memory_space=pl.ANY),
                  pl.BlockSpec(memory_space=pl.ANY)],
        out_specs=pl.BlockSpec(memory_space=pl.ANY),
        out_shape=jax.ShapeDtypeStruct(x_pad_init.shape, x_pad_init.dtype),
        scratch_shapes=[pltpu.SemaphoreType.DMA(())],
        input_output_aliases={2: 0},
        compiler_params=_params(("arbitrary",)),
        name="moe_dispatch",
    )(pos, t, x_pad_init)


def _expert_kernel(blk_e_ref, nused_ref, x_ref, wu_ref, bu_ref, wd_ref, bd_ref, y_ref):
    del blk_e_ref

    @pl.when(pl.program_id(0) < nused_ref[0])
    def _():
        x = x_ref[...].astype(BF16)
        u = _dot(x, wu_ref[...]) + bu_ref[...]
        glu = jnp.minimum(u[:, :D_FF], SWIGLU_LIMIT)
        lin = jnp.clip(u[:, D_FF:], -SWIGLU_LIMIT, SWIGLU_LIMIT)
        act = glu * jax.nn.sigmoid(SWIGLU_ALPHA * glu) * (lin + 1.0)
        y_ref[...] = _dot(act.astype(BF16), wd_ref[...]) + bd_ref[...]


def _experts(blk_e, n_used, x_pad, wu, bu, wd, bd, blk):
    cap = x_pad.shape[0]
    n_blk = cap // blk
    rowmap = lambda i, be, nu: (jnp.minimum(i, nu[0] - 1), 0)
    emap = lambda i, be, nu: (be[i], 0, 0)
    return pl.pallas_call(
        _expert_kernel,
        grid_spec=pltpu.PrefetchScalarGridSpec(
            num_scalar_prefetch=2,
            grid=(n_blk,),
            in_specs=[pl.BlockSpec((blk, D_MODEL), rowmap),
                      pl.BlockSpec((None, D_MODEL, 2 * D_FF), emap),
                      pl.BlockSpec((None, 1, 2 * D_FF), emap),
                      pl.BlockSpec((None, D_FF, D_MODEL), emap),
                      pl.BlockSpec((None, 1, D_MODEL), emap)],
            out_specs=pl.BlockSpec((blk, D_MODEL), rowmap)),
        out_shape=jax.ShapeDtypeStruct((cap, D_MODEL), F32),
        compiler_params=_params(("arbitrary",)),
        name="moe_experts",
    )(blk_e, n_used, x_pad, wu, bu, wd, bd)


def _combine_kernel(pos_ref, y_ref, gate_ref, x2_ref, g_ref, o_ref, rows_ref, sem, *, tm):
    def issue(j, carry):
        for k in range(TOP_K):
            pltpu.make_async_copy(y_ref.at[pl.ds(pos_ref[k, j], 1)],
                                  rows_ref.at[k, pl.ds(j, 1)], sem).start()
        return carry

    lax.fori_loop(0, tm, issue, 0)

    def drain(j, carry):
        pltpu.make_async_copy(y_ref.at[pl.ds(0, 1)], rows_ref.at[0, pl.ds(0, 1)], sem).wait()
        return carry

    lax.fori_loop(0, tm * TOP_K, drain, 0)
    acc = x2_ref[...]
    gate = gate_ref[...]
    for k in range(TOP_K):
        acc = acc + rows_ref[k] * gate[:, k:k + 1]
    o_ref[...] = _rms(acc, g_ref[...])


def _combine(pos, y_pad, gate_t, x2, g, tm=128):
    n = x2.shape[0]
    row = pl.BlockSpec((tm, D_MODEL), lambda i: (i, 0))
    return pl.pallas_call(
        functools.partial(_combine_kernel, tm=tm),
        grid=(n // tm,),
        in_specs=[pl.BlockSpec((TOP_K, tm), lambda i: (0, i), memory_space=pltpu.SMEM),
                  pl.BlockSpec(memory_space=pl.ANY),
                  pl.BlockSpec((tm, TOP_K), lambda i: (i, 0)),
                  row,
                  pl.BlockSpec(g.shape, lambda i: (0, 0))],
        out_specs=row,
        out_shape=jax.ShapeDtypeStruct((n, D_MODEL), F32),
        scratch_shapes=[pltpu.VMEM((TOP_K, tm, D_MODEL), F32), pltpu.SemaphoreType.DMA(())],
        compiler_params=_params(("arbitrary",)),
        name="moe_combine",
    )(pos, y_pad, gate_t, x2, g)


def _rope_inv_freq():
    inv = ROPE_THETA ** (-np.arange(0, ROPE_DIM, 2, dtype=np.float32) / ROPE_DIM)
    lane = np.arange(LANES) % DA_DIM
    tab = np.where(lane < ROPE_DIM, inv.astype(np.float32)[lane % (ROPE_DIM // 2)], 0.0)
    return jnp.asarray(tab.astype(np.float32)[None, :])


def kernel(x, mem, positions, norm_mix_g, w_in, lambda_q1, lambda_k1, lambda_q2, lambda_k2, diff_norm_g, w_alpha2, b_alpha, gla_norm_g, w_out, norm_cross_g, norm_mem_g, w_cq, w_ck, w_cv, w_co, norm_ffn_g, w_router, b_router, w_up, b_up, w_down, b_down, norm_final_g):
    batch, seq, d = x.shape
    mem_len = mem.shape[1]
    n = batch * seq
    moe_blk = 256
    row = lambda a: a.reshape(1, -1)

    x2d = x.reshape(n, d)
    w = w_in[0]
    wqk, wv = w[:, :1024].astype(BF16), w[:, 1024:1536].astype(BF16)
    wg, wga = w[:, 1536:3072].astype(BF16), w[:, 3072:].astype(BF16)
    q, k, v, gq, gk, gv, gr, la = _inproj(
        x2d, positions.reshape(n, 1), _rope_inv_freq(), row(norm_mix_g[0]), wqk, wv, wg, wga,
        w_alpha2[0].astype(BF16), row(b_alpha[0]))

    o_da = _diffattn(q, k, v, row(lambda_q1[0]), row(lambda_k1[0]), row(lambda_q2[0]),
                     row(lambda_k2[0]), row(diff_norm_g[0]), batch, seq)
    o_gla = _gla(gq, gk, gv, gr, la, row(gla_norm_g[0]), batch, seq)

    x1, qc = _outproj(o_da, o_gla, x2d, w_out[0].astype(BF16), row(norm_cross_g[0]),
                      w_cq[0].astype(BF16))
    kc, vc = _memkv(mem.reshape(batch * mem_len, d), row(norm_mem_g[0]),
                    w_ck[0].astype(BF16), w_cv[0].astype(BF16))
    x2, t, top_e, gate, rank, counts = _cross(
        qc, kc, vc, x1, w_co[0].astype(BF16), row(norm_ffn_g[0]), w_router[0].T,
        b_router[0].reshape(-1, 1), batch, seq, mem_len)

    cnt = counts[:, 0].astype(jnp.int32)
    padded = (cnt + moe_blk - 1) // moe_blk * moe_blk
    pad_end = jnp.cumsum(padded)
    pad_start = pad_end - padded
    cap = n * TOP_K + N_EXPERTS * moe_blk
    n_blk = cap // moe_blk
    n_used = (pad_end[-1] // moe_blk).astype(jnp.int32)
    blk_ids = jnp.minimum(jnp.arange(n_blk, dtype=jnp.int32), n_used - 1)
    blk_e = jnp.minimum(jnp.searchsorted(pad_end, blk_ids * moe_blk, side='right'),
                        N_EXPERTS - 1).astype(jnp.int32)
    pos = (pad_start[top_e] + rank).astype(jnp.int32)

    x_pad = _dispatch(pos, t, jnp.zeros((cap, d), F32))
    y_pad = _experts(blk_e, n_used.reshape(1), x_pad, w_up[0].astype(BF16),
                     b_up[0][:, None, :], w_down[0].astype(BF16), b_down[0][:, None, :],
                     moe_blk)
    out = _combine(pos, y_pad, gate.T, x2, row(norm_final_g))
    return out.reshape(batch, seq, d)
```

```python
import functools

import numpy as np
import jax
import jax.numpy as jnp
from jax import lax
from jax.experimental import pallas as pl
from jax.experimental.pallas import tpu as pltpu

EPS = 1e-6
D_MODEL = 1024
DA_HEADS = 4
DA_DIM = 64
GLA_HEADS = 4
GLA_DK = 64
GLA_DV = 128
GLA_RANK = 16
GLA_TAU = 16.0
GLA_CHUNK = 64
ROPE_THETA = 500000.0
ROPE_DIM = DA_DIM // 4
X_HEADS = 4
X_DIM = D_MODEL // X_HEADS
N_EXPERTS = 32
TOP_K = 4
D_FF = D_MODEL
SWIGLU_LIMIT = 7.0
SWIGLU_ALPHA = 1.702
LAMBDA_INIT = 0.8 - 0.6 * 1.0

LANES = 128
BF16 = jnp.bfloat16
F32 = jnp.float32
VMEM_LIMIT = 48 * 1024 * 1024


def _params(sem, vmem=VMEM_LIMIT):
    return pltpu.CompilerParams(dimension_semantics=sem, vmem_limit_bytes=vmem)


def _rms(xf, g):
    return xf * lax.rsqrt(jnp.mean(xf * xf, axis=-1, keepdims=True) + EPS) * g


def _dot(a, b):
    return jnp.dot(a, b, preferred_element_type=F32)


def _dot_nt(a, b):
    return lax.dot_general(a, b, (((1,), (1,)), ((), ())), preferred_element_type=F32)


def _split3(a):
    hi = a.astype(BF16)
    r1 = a - hi.astype(F32)
    mid = r1.astype(BF16)
    lo = (r1 - mid.astype(F32)).astype(BF16)
    return hi, mid, lo


def _inproj_kernel(x_ref, pos_ref, invf_ref, g_ref, wqk_ref, wv_ref, wg_ref, wga_ref,
                   wa2_ref, ba_ref,
                   q_ref, k_ref, v_ref, gq_ref, gk_ref, gv_ref, gr_ref, la_ref):
    h = _rms(x_ref[...], g_ref[...]).astype(BF16)
    ang = pos_ref[...].astype(F32) * invf_ref[...]
    lane = lax.broadcasted_iota(jnp.int32, ang.shape, 1) % DA_DIM
    half = ROPE_DIM // 2
    cosv = jnp.where(lane < ROPE_DIM, jnp.cos(ang), 1.0)
    sinv = jnp.sin(ang)
    s_lo = jnp.where(lane < half, -sinv, 0.0)
    s_hi = jnp.where((lane >= half) & (lane < ROPE_DIM), sinv, 0.0)

    qk = _dot(h, wqk_ref[...])
    n_grp = qk.shape[1] // LANES
    for j in range(n_grp):
        t = qk[:, j * LANES:(j + 1) * LANES]
        rot = (t * cosv + pltpu.roll(t, LANES - half, 1) * s_lo
               + pltpu.roll(t, half, 1) * s_hi)
        if j < n_grp // 2:
            q_ref[:, j * LANES:(j + 1) * LANES] = (rot * (DA_DIM ** -0.5)).astype(BF16)
        else:
            jj = j - n_grp // 2
            k_ref[:, jj * LANES:(jj + 1) * LANES] = rot.astype(BF16)
    v_ref[...] = _dot(h, wv_ref[...]).astype(BF16)
    gg = _dot(h, wg_ref[...])
    gq_ref[...] = gg[:, 0:256]
    gk_ref[...] = gg[:, 256:512]
    gv_ref[...] = gg[:, 512:1024]
    gr_ref[...] = gg[:, 1024:1536]
    ga = _dot(h, wga_ref[...])
    z = _dot(ga.astype(BF16), wa2_ref[...]) + ba_ref[...]
    la_ref[...] = (jnp.minimum(z, 0.0) - jnp.log1p(jnp.exp(-jnp.abs(z)))) * (1.0 / GLA_TAU)


def _inproj(x2d, pos2d, invf, g, wqk, wv, wg, wga, wa2, ba, tm=256):
    n = x2d.shape[0]
    row = lambda w: pl.BlockSpec((tm, w), lambda i: (i, 0))
    full = lambda a: pl.BlockSpec(a.shape, lambda i: (0,) * a.ndim)
    outs = [(512, BF16), (512, BF16), (512, BF16), (256, F32), (256, F32), (512, F32),
            (512, F32), (256, F32)]
    return pl.pallas_call(
        _inproj_kernel,
        grid=(n // tm,),
        in_specs=[row(D_MODEL), row(1), full(invf), full(g), full(wqk), full(wv), full(wg),
                  full(wga), full(wa2), full(ba)],
        out_specs=[row(w) for w, _ in outs],
        out_shape=[jax.ShapeDtypeStruct((n, w), dt) for w, dt in outs],
        compiler_params=_params(("arbitrary",)),
        name="inproj",
    )(x2d, pos2d, invf, g, wqk, wv, wg, wga, wa2, ba)


def _diffattn_kernel(lq1_ref, lk1_ref, lq2_ref, lk2_ref, gn_ref, q_ref, k_ref, v_ref, o_ref,
                     s_ref, mx_ref, l_ref, acc_ref, *, tq):
    qi = pl.program_id(2)
    q = q_ref[...]
    lane = lax.broadcasted_iota(jnp.int32, q.shape, 1)
    zero = jnp.zeros_like(q)
    qs = jnp.concatenate([jnp.where(lane < DA_DIM, q, zero),
                          jnp.where(lane >= DA_DIM, q, zero)], axis=0)
    mx_ref[...] = jnp.full_like(mx_ref, -jnp.inf)

    def scores(kb, masked):
        start = pl.multiple_of(kb * tq, tq)
        s = _dot_nt(qs, k_ref[pl.ds(start, tq), :])
        if masked:
            r = lax.broadcasted_iota(jnp.int32, s.shape, 0) % tq
            c = lax.broadcasted_iota(jnp.int32, s.shape, 1)
            s = jnp.where(c <= r, s, -jnp.inf)
        s_ref[kb] = s
        for j in range(tq // LANES):
            mx_ref[...] = jnp.maximum(mx_ref[...], s[:, j * LANES:(j + 1) * LANES])

    def score_body(kb, carry):
        scores(kb, False)
        return carry

    lax.fori_loop(0, qi, score_body, 0)
    scores(qi, True)

    mx_ref[...] = jnp.broadcast_to(jnp.max(mx_ref[...], axis=-1, keepdims=True), mx_ref.shape)
    l_ref[...] = jnp.zeros_like(l_ref)
    acc_ref[...] = jnp.zeros_like(acc_ref)

    def pv_body(kb, carry):
        start = pl.multiple_of(kb * tq, tq)
        m = mx_ref[...]
        p = jnp.exp(s_ref[kb] - jnp.concatenate([m] * (tq // LANES), axis=1))
        part = p[:, 0:LANES]
        for j in range(1, tq // LANES):
            part = part + p[:, j * LANES:(j + 1) * LANES]
        l_ref[...] += part
        acc_ref[...] += _dot(p.astype(BF16), v_ref[pl.ds(start, tq), :])
        return carry

    lax.fori_loop(0, qi + 1, pv_body, 0)

    lam = (jnp.exp(jnp.sum(lq1_ref[...] * lk1_ref[...], axis=-1, keepdims=True))
           - jnp.exp(jnp.sum(lq2_ref[...] * lk2_ref[...], axis=-1, keepdims=True))
           + LAMBDA_INIT)
    on = acc_ref[...] / jnp.sum(l_ref[...], axis=-1, keepdims=True)
    o = on[0:tq] - lam * on[tq:2 * tq]
    o_ref[...] = (_rms(o, gn_ref[...]) * (1.0 - LAMBDA_INIT)).astype(o_ref.dtype)


def _diffattn(q, k, v, lq1, lk1, lq2, lk2, gn, batch, seq, tq=256):
    nq = seq // tq
    vec = lambda a: pl.BlockSpec(a.shape, lambda b, h, i: (0, 0))
    return pl.pallas_call(
        functools.partial(_diffattn_kernel, tq=tq),
        grid=(batch, DA_HEADS, nq),
        in_specs=[vec(lq1), vec(lk1), vec(lq2), vec(lk2), vec(gn),
                  pl.BlockSpec((tq, LANES), lambda b, h, i: (b * nq + i, h)),
                  pl.BlockSpec((seq, LANES), lambda b, h, i: (b, h)),
                  pl.BlockSpec((seq, LANES), lambda b, h, i: (b, h))],
        out_specs=pl.BlockSpec((tq, LANES), lambda b, h, i: (b * nq + i, h)),
        out_shape=jax.ShapeDtypeStruct(q.shape, BF16),
        scratch_shapes=[pltpu.VMEM((nq, 2 * tq, tq), F32),
                        pltpu.VMEM((2 * tq, LANES), F32),
                        pltpu.VMEM((2 * tq, LANES), F32),
                        pltpu.VMEM((2 * tq, LANES), F32)],
        compiler_params=_params(("arbitrary", "arbitrary", "arbitrary")),
        name="diffattn",
    )(lq1, lk1, lq2, lk2, gn, q, k, v)


def _gla_kernel(gq_ref, gk_ref, gv_ref, gr_ref, la_ref, gn_ref, o_ref, st_ref, *, rows):
    c = GLA_CHUNK
    pw = 2 * GLA_DK
    vw = 2 * GLA_DV

    @pl.when(pl.program_id(1) == 0)
    def _():
        st_ref[...] = jnp.zeros_like(st_ref)

    r_i = lax.broadcasted_iota(jnp.int32, (c, c), 0)
    c_i = lax.broadcasted_iota(jnp.int32, (c, c), 1)
    causal = c_i <= r_i
    tril = causal.astype(BF16)
    sr = lax.broadcasted_iota(jnp.int32, (pw, vw), 0) // GLA_DK
    sc = lax.broadcasted_iota(jnp.int32, (pw, vw), 1) // GLA_DV
    blockdiag = sr == sc
    klane = lax.broadcasted_iota(jnp.int32, (c, pw), 1) // GLA_DK

    for ci in range(rows // c):
        rs = slice(ci * c, (ci + 1) * c)
        for p in range(GLA_HEADS // 2):
            ks = slice(p * pw, (p + 1) * pw)
            vs = slice(p * vw, (p + 1) * vw)
            la = la_ref[rs, ks]
            hi, mid, lo = _split3(la)
            b = _dot(tril, hi) + _dot(tril, mid) + _dot(tril, lo)
            b_last = b[c - 1:c, :]
            qe = gq_ref[rs, ks] * jnp.exp(b) * (GLA_DK ** -0.5)
            gk = gk_ref[rs, ks]
            kn = (gk * jnp.exp(-b)).astype(BF16)
            kd = gk * jnp.exp(b_last - b)
            v = gv_ref[rs, vs].astype(BF16)
            st = st_ref[p]
            inter = _dot(qe.astype(BF16), st.astype(BF16))
            outs = []
            for hh in range(2):
                qh = jnp.where(klane == hh, qe, 0.0).astype(BF16)
                attn = jnp.where(causal, _dot_nt(qh, kn), 0.0)
                outs.append(_dot(attn.astype(BF16), v[:, hh * GLA_DV:(hh + 1) * GLA_DV]))
            o = inter + jnp.concatenate(outs, axis=1)
            upd = _dot(kd.T.astype(BF16), v)
            decay = jnp.exp(b_last).T
            st_ref[p] = st * decay + jnp.where(blockdiag, upd, 0.0)
            for hh in range(2):
                hs = slice(hh * GLA_DV, (hh + 1) * GLA_DV)
                os_ = slice(p * vw + hh * GLA_DV, p * vw + (hh + 1) * GLA_DV)
                gr = gr_ref[rs, os_]
                y = _rms(o[:, hs], gn_ref[...]) * (gr * jax.nn.sigmoid(gr))
                o_ref[rs, os_] = y.astype(o_ref.dtype)


def _gla(gq, gk, gv, gr, la, gn, batch, seq, rows=256):
    nb = seq // rows
    row = lambda w: pl.BlockSpec((rows, w), lambda b, i: (b * nb + i, 0))
    return pl.pallas_call(
        functools.partial(_gla_kernel, rows=rows),
        grid=(batch, nb),
        in_specs=[row(256), row(256), row(512), row(512), row(256),
                  pl.BlockSpec(gn.shape, lambda b, i: (0, 0))],
        out_specs=row(512),
        out_shape=jax.ShapeDtypeStruct(gv.shape, BF16),
        scratch_shapes=[pltpu.VMEM((GLA_HEADS // 2, 2 * GLA_DK, 2 * GLA_DV), F32)],
        compiler_params=_params(("arbitrary", "arbitrary")),
        name="gla",
    )(gq, gk, gv, gr, la, gn)


def _outproj_kernel(oda_ref, ogla_ref, x_ref, wo_ref, g_ref, wcq_ref, x1_ref, qc_ref):
    half = oda_ref.shape[1]
    mix = _dot(oda_ref[...], wo_ref[0:half, :]) + _dot(ogla_ref[...], wo_ref[half:, :])
    x1 = x_ref[...] + mix
    x1_ref[...] = x1
    hq = _rms(x1, g_ref[...]).astype(BF16)
    qc_ref[...] = (_dot(hq, wcq_ref[...]) * (X_DIM ** -0.5)).astype(BF16)


def _outproj(oda, ogla, x2d, wo, g, wcq, tm=256):
    n = x2d.shape[0]
    row = lambda w: pl.BlockSpec((tm, w), lambda i: (i, 0))
    full = lambda a: pl.BlockSpec(a.shape, lambda i: (0,) * a.ndim)
    return pl.pallas_call(
        _outproj_kernel,
        grid=(n // tm,),
        in_specs=[row(512), row(512), row(D_MODEL), full(wo), full(g), full(wcq)],
        out_specs=[row(D_MODEL), row(D_MODEL)],
        out_shape=[jax.ShapeDtypeStruct((n, D_MODEL), F32),
                   jax.ShapeDtypeStruct((n, D_MODEL), BF16)],
        compiler_params=_params(("arbitrary",)),
        name="outproj",
    )(oda, ogla, x2d, wo, g, wcq)


def _memkv_kernel(m_ref, g_ref, wk_ref, wv_ref, k_ref, v_ref):
    hm = _rms(m_ref[...], g_ref[...]).astype(BF16)
    k_ref[...] = _dot(hm, wk_ref[...]).astype(BF16)
    v_ref[...] = _dot(hm, wv_ref[...]).astype(BF16)


def _memkv(mem2d, g, wk, wv, tm=256):
    n = mem2d.shape[0]
    row = pl.BlockSpec((tm, D_MODEL), lambda i: (i, 0))
    full = lambda a: pl.BlockSpec(a.shape, lambda i: (0,) * a.ndim)
    return pl.pallas_call(
        _memkv_kernel,
        grid=(n // tm,),
        in_specs=[row, full(g), full(wk), full(wv)],
        out_specs=[row, row],
        out_shape=[jax.ShapeDtypeStruct((n, D_MODEL), BF16)] * 2,
        compiler_params=_params(("arbitrary",)),
        name="memkv",
    )(mem2d, g, wk, wv)


def _cross_kernel(qc_ref, kc_ref, vc_ref, x1_ref, wco_ref, g_ref, wrt_ref, br_ref,
                  x2_ref, t_ref, e_ref, gate_ref, rank_ref, cnt_ref, run_ref):
    @pl.when((pl.program_id(0) == 0) & (pl.program_id(1) == 0))
    def _():
        run_ref[...] = jnp.zeros_like(run_ref)

    tq = qc_ref.shape[0]
    outs = []
    for h in range(X_HEADS):
        hs = slice(h * X_DIM, (h + 1) * X_DIM)
        s = _dot_nt(qc_ref[:, hs], kc_ref[:, hs])
        m = jnp.max(s, axis=-1, keepdims=True)
        p = jnp.exp(s - m)
        l = jnp.sum(p, axis=-1, keepdims=True)
        outs.append((_dot(p.astype(BF16), vc_ref[:, hs]) / l).astype(BF16))
    o = jnp.concatenate(outs, axis=1)
    x2 = x1_ref[...] + _dot(o, wco_ref[...])
    x2_ref[...] = x2
    t = _rms(x2, g_ref[...])
    t_ref[...] = t

    t_hi, t_mid, _ = _split3(t)
    w_hi, w_mid, _ = _split3(wrt_ref[...])
    logit = (_dot_nt(w_hi, t_hi) + _dot_nt(w_hi, t_mid) + _dot_nt(w_mid, t_hi)) + br_ref[...]
    iota_e = lax.broadcasted_iota(jnp.int32, logit.shape, 0)
    vals, idxs, sels = [], [], []
    for _ in range(TOP_K):
        mx = jnp.max(logit, axis=0, keepdims=True)
        idx = jnp.min(jnp.where(logit == mx, iota_e, N_EXPERTS), axis=0, keepdims=True)
        sel = iota_e == idx
        vals.append(mx)
        idxs.append(idx)
        sels.append(sel)
        logit = jnp.where(sel, -jnp.inf, logit)
    ex = [jnp.exp(v - vals[0]) for v in vals]
    den = ex[0] + ex[1] + ex[2] + ex[3]
    gate_ref[...] = jnp.concatenate([e / den for e in ex], axis=0)
    e_ref[...] = jnp.concatenate(idxs, axis=0)
    onehot = (sels[0] | sels[1] | sels[2] | sels[3])
    cnt = onehot.astype(BF16)
    ur = lax.broadcasted_iota(jnp.int32, (tq, tq), 0)
    uc = lax.broadcasted_iota(jnp.int32, (tq, tq), 1)
    before = (ur < uc).astype(BF16)
    base = _dot(cnt, before) + run_ref[...]
    rank_ref[...] = jnp.concatenate(
        [jnp.sum(jnp.where(s_, base, 0.0), axis=0, keepdims=True) for s_ in sels],
        axis=0).astype(jnp.int32)
    run = run_ref[...] + jnp.sum(onehot.astype(F32), axis=1, keepdims=True)
    run_ref[...] = run
    cnt_ref[...] = jnp.broadcast_to(run, cnt_ref.shape)


def _cross(qc, kc, vc, x1, wco, g, wrt, br, batch, seq, mem_len, tq=256):
    nq = seq // tq
    n = batch * seq
    row = lambda w: pl.BlockSpec((tq, w), lambda b, i: (b * nq + i, 0))
    col = lambda r: pl.BlockSpec((r, tq), lambda b, i: (0, b * nq + i))
    full = lambda a: pl.BlockSpec(a.shape, lambda b, i: (0,) * a.ndim)
    memb = pl.BlockSpec((mem_len, D_MODEL), lambda b, i: (b, 0))
    return pl.pallas_call(
        _cross_kernel,
        grid=(batch, nq),
        in_specs=[row(D_MODEL), memb, memb, row(D_MODEL), full(wco), full(g), full(wrt),
                  full(br)],
        out_specs=[row(D_MODEL), row(D_MODEL), col(TOP_K), col(TOP_K), col(TOP_K),
                   pl.BlockSpec((N_EXPERTS, LANES), lambda b, i: (0, 0))],
        out_shape=[jax.ShapeDtypeStruct((n, D_MODEL), F32),
                   jax.ShapeDtypeStruct((n, D_MODEL), F32),
                   jax.ShapeDtypeStruct((TOP_K, n), jnp.int32),
                   jax.ShapeDtypeStruct((TOP_K, n), F32),
                   jax.ShapeDtypeStruct((TOP_K, n), jnp.int32),
                   jax.ShapeDtypeStruct((N_EXPERTS, LANES), F32)],
        scratch_shapes=[pltpu.VMEM((N_EXPERTS, 1), F32)],
        compiler_params=_params(("arbitrary", "arbitrary")),
        name="cross_router",
    )(qc, kc, vc, x1, wco, g, wrt, br)


def _dispatch_kernel(tail_ref, pos_ref, t_ref, xp_ref, zero_ref, sem, zsem, *, tm, blk):
    @pl.when(pl.program_id(0) == 0)
    def _():
        zero_ref[...] = jnp.zeros_like(zero_ref)
        tails = [pl.multiple_of(tail_ref[e], blk) for e in range(N_EXPERTS)]
        for e in range(N_EXPERTS):
            pltpu.make_async_copy(zero_ref, xp_ref.at[pl.ds(tails[e], blk)], zsem).start()
        for e in range(N_EXPERTS):
            pltpu.make_async_copy(zero_ref, xp_ref.at[pl.ds(tails[e], blk)], zsem).wait()

    def issue(j, carry):
        for k in range(TOP_K):
            pltpu.make_async_copy(t_ref.at[pl.ds(j, 1)],
                                  xp_ref.at[pl.ds(pos_ref[k, j], 1)], sem).start()
        return carry

    lax.fori_loop(0, tm, issue, 0, unroll=4)
    pltpu.make_async_copy(xp_ref.at[pl.ds(0, TOP_K * tm)], xp_ref.at[pl.ds(0, TOP_K * tm)],
                          sem).wait()


def _dispatch(tail, pos, t, cap, blk, tm=256):
    n, d = t.shape
    return pl.pallas_call(
        functools.partial(_dispatch_kernel, tm=tm, blk=blk),
        grid_spec=pltpu.PrefetchScalarGridSpec(
            num_scalar_prefetch=1,
            grid=(n // tm,),
            in_specs=[pl.BlockSpec((TOP_K, tm), lambda i, tl: (0, i), memory_space=pltpu.SMEM),
                      pl.BlockSpec((tm, d), lambda i, tl: (i, 0))],
            out_specs=pl.BlockSpec(memory_space=pl.ANY),
            scratch_shapes=[pltpu.VMEM((blk, d), t.dtype), pltpu.SemaphoreType.DMA(()),
                            pltpu.SemaphoreType.DMA(())]),
        out_shape=jax.ShapeDtypeStruct((cap, d), t.dtype),
        compiler_params=_params(("arbitrary",)),
        name="moe_dispatch",
    )(tail, pos, t)


def _expert_kernel(blk_e_ref, nused_ref, x_ref, wu_ref, bu_ref, wd_ref, bd_ref, y_ref):
    del blk_e_ref

    @pl.when(pl.program_id(0) < nused_ref[0])
    def _():
        x = x_ref[...].astype(BF16)
        u = _dot(x, wu_ref[...]) + bu_ref[...]
        glu = jnp.minimum(u[:, :D_FF], SWIGLU_LIMIT)
        lin = jnp.clip(u[:, D_FF:], -SWIGLU_LIMIT, SWIGLU_LIMIT)
        act = glu * jax.nn.sigmoid(SWIGLU_ALPHA * glu) * (lin + 1.0)
        y_ref[...] = _dot(act.astype(BF16), wd_ref[...]) + bd_ref[...]


def _experts(blk_e, n_used, x_pad, wu, bu, wd, bd, blk):
    cap = x_pad.shape[0]
    n_blk = cap // blk
    rowmap = lambda i, be, nu: (jnp.minimum(i, nu[0] - 1), 0)
    emap = lambda i, be, nu: (be[i], 0, 0)
    return pl.pallas_call(
        _expert_kernel,
        grid_spec=pltpu.PrefetchScalarGridSpec(
            num_scalar_prefetch=2,
            grid=(n_blk,),
            in_specs=[pl.BlockSpec((blk, D_MODEL), rowmap),
                      pl.BlockSpec((None, D_MODEL, 2 * D_FF), emap),
                      pl.BlockSpec((None, 1, 2 * D_FF), emap),
                      pl.BlockSpec((None, D_FF, D_MODEL), emap),
                      pl.BlockSpec((None, 1, D_MODEL), emap)],
            out_specs=pl.BlockSpec((blk, D_MODEL), rowmap)),
        out_shape=jax.ShapeDtypeStruct((cap, D_MODEL), F32),
        compiler_params=_params(("arbitrary",)),
        name="moe_experts",
    )(blk_e, n_used, x_pad, wu, bu, wd, bd)


def _combine_kernel(pos_ref, posn_ref, y_ref, gate_ref, x2_ref, g_ref, o_ref, rows_ref, sem,
                    *, tm):
    i = pl.program_id(0)
    slot = i % 2

    def issue(p_ref, s):
        def body(j, carry):
            for k in range(TOP_K):
                pltpu.make_async_copy(y_ref.at[pl.ds(p_ref[k, j], 1)],
                                      rows_ref.at[s, pl.ds(k * tm + j, 1)], sem.at[s]).start()
            return carry

        lax.fori_loop(0, tm, body, 0, unroll=4)

    @pl.when(i == 0)
    def _():
        issue(pos_ref, 0)

    @pl.when(i + 1 < pl.num_programs(0))
    def _():
        issue(posn_ref, 1 - slot)

    pltpu.make_async_copy(y_ref.at[pl.ds(0, TOP_K * tm)], rows_ref.at[slot], sem.at[slot]).wait()
    acc = x2_ref[...]
    gate = gate_ref[...]
    for k in range(TOP_K):
        acc = acc + rows_ref[slot, k * tm:(k + 1) * tm, :] * gate[:, k:k + 1]
    o_ref[...] = _rms(acc, g_ref[...])


def _combine(pos, y_pad, gate_t, x2, g, tm=128):
    n = x2.shape[0]
    nt = n // tm
    row = pl.BlockSpec((tm, D_MODEL), lambda i: (i, 0))
    return pl.pallas_call(
        functools.partial(_combine_kernel, tm=tm),
        grid=(nt,),
        in_specs=[pl.BlockSpec((TOP_K, tm), lambda i: (0, i), memory_space=pltpu.SMEM),
                  pl.BlockSpec((TOP_K, tm), lambda i: (0, jnp.minimum(i + 1, nt - 1)),
                               memory_space=pltpu.SMEM),
                  pl.BlockSpec(memory_space=pl.ANY),
                  pl.BlockSpec((tm, TOP_K), lambda i: (i, 0)),
                  row,
                  pl.BlockSpec(g.shape, lambda i: (0, 0))],
        out_specs=row,
        out_shape=jax.ShapeDtypeStruct((n, D_MODEL), F32),
        scratch_shapes=[pltpu.VMEM((2, TOP_K * tm, D_MODEL), F32),
                        pltpu.SemaphoreType.DMA((2,))],
        compiler_params=_params(("arbitrary",)),
        name="moe_combine",
    )(pos, pos, y_pad, gate_t, x2, g)


def _rope_inv_freq():
    inv = ROPE_THETA ** (-np.arange(0, ROPE_DIM, 2, dtype=np.float32) / ROPE_DIM)
    lane = np.arange(LANES) % DA_DIM
    tab = np.where(lane < ROPE_DIM, inv.astype(np.float32)[lane % (ROPE_DIM // 2)], 0.0)
    return jnp.asarray(tab.astype(np.float32)[None, :])


def kernel(x, mem, positions, norm_mix_g, w_in, lambda_q1, lambda_k1, lambda_q2, lambda_k2, diff_norm_g, w_alpha2, b_alpha, gla_norm_g, w_out, norm_cross_g, norm_mem_g, w_cq, w_ck, w_cv, w_co, norm_ffn_g, w_router, b_router, w_up, b_up, w_down, b_down, norm_final_g):
    batch, seq, d = x.shape
    mem_len = mem.shape[1]
    n = batch * seq
    moe_blk = 256
    row = lambda a: a.reshape(1, -1)

    x2d = x.reshape(n, d)
    w = w_in[0]
    wqk, wv = w[:, :1024].astype(BF16), w[:, 1024:1536].astype(BF16)
    wg, wga = w[:, 1536:3072].astype(BF16), w[:, 3072:].astype(BF16)
    q, k, v, gq, gk, gv, gr, la = _inproj(
        x2d, positions.reshape(n, 1), _rope_inv_freq(), row(norm_mix_g[0]), wqk, wv, wg, wga,
        w_alpha2[0].astype(BF16), row(b_alpha[0]))

    o_da = _diffattn(q, k, v, row(lambda_q1[0]), row(lambda_k1[0]), row(lambda_q2[0]),
                     row(lambda_k2[0]), row(diff_norm_g[0]), batch, seq)
    o_gla = _gla(gq, gk, gv, gr, la, row(gla_norm_g[0]), batch, seq)

    x1, qc = _outproj(o_da, o_gla, x2d, w_out[0].astype(BF16), row(norm_cross_g[0]),
                      w_cq[0].astype(BF16))
    kc, vc = _memkv(mem.reshape(batch * mem_len, d), row(norm_mem_g[0]),
                    w_ck[0].astype(BF16), w_cv[0].astype(BF16))
    x2, t, top_e, gate, rank, counts = _cross(
        qc, kc, vc, x1, w_co[0].astype(BF16), row(norm_ffn_g[0]), w_router[0].T,
        b_router[0].reshape(-1, 1), batch, seq, mem_len)

    cnt = counts[:, 0].astype(jnp.int32)
    padded = (cnt + moe_blk - 1) // moe_blk * moe_blk
    pad_end = jnp.cumsum(padded)
    pad_start = pad_end - padded
    cap = n * TOP_K + N_EXPERTS * moe_blk
    n_blk = cap // moe_blk
    n_used = (pad_end[-1] // moe_blk).astype(jnp.int32)
    blk_ids = jnp.minimum(jnp.arange(n_blk, dtype=jnp.int32), n_used - 1)
    blk_e = jnp.minimum(jnp.sum(pad_end[None, :] <= (blk_ids * moe_blk)[:, None], axis=1),
                        N_EXPERTS - 1).astype(jnp.int32)
    eids = jnp.arange(N_EXPERTS, dtype=jnp.int32)[:, None, None]
    pos = (rank + jnp.sum(jnp.where(top_e[None] == eids, pad_start[:, None, None], 0),
                          axis=0)).astype(jnp.int32)
    tail = jnp.where(cnt > 0, pad_end, pad_end[-1]).astype(jnp.int32) - moe_blk

    x_pad = _dispatch(tail, pos, t, cap, moe_blk)
    y_pad = _experts(blk_e, n_used.reshape(1), x_pad, w_up[0].astype(BF16),
                     b_up[0][:, None, :], w_down[0].astype(BF16), b_down[0][:, None, :],
                     moe_blk)
    out = _combine(pos, y_pad, gate.T, x2, row(norm_final_g))
    return out.reshape(batch, seq, d)
```

```python
import functools

import numpy as np
import jax
import jax.numpy as jnp
from jax import lax
from jax.experimental import pallas as pl
from jax.experimental.pallas import tpu as pltpu

EPS = 1e-6
D_MODEL = 1024
DA_HEADS = 4
DA_DIM = 64
GLA_HEADS = 4
GLA_DK = 64
GLA_DV = 128
GLA_RANK = 16
GLA_TAU = 16.0
GLA_CHUNK = 64
ROPE_THETA = 500000.0
ROPE_DIM = DA_DIM // 4
X_HEADS = 4
X_DIM = D_MODEL // X_HEADS
N_EXPERTS = 32
TOP_K = 4
D_FF = D_MODEL
SWIGLU_LIMIT = 7.0
SWIGLU_ALPHA = 1.702
LAMBDA_INIT = 0.8 - 0.6 * 1.0
Q_SCALE = DA_DIM ** -0.5 * float(np.log2(np.e))

LANES = 128
ROW_GRAN = 8
BF16 = jnp.bfloat16
F32 = jnp.float32
VMEM_LIMIT = 48 * 1024 * 1024


def _params(sem, vmem=VMEM_LIMIT):
    return pltpu.CompilerParams(dimension_semantics=sem, vmem_limit_bytes=vmem)


def _rms(xf, g):
    return xf * lax.rsqrt(jnp.mean(xf * xf, axis=-1, keepdims=True) + EPS) * g


def _dot(a, b):
    return jnp.dot(a, b, preferred_element_type=F32)


def _dot_nt(a, b):
    return lax.dot_general(a, b, (((1,), (1,)), ((), ())), preferred_element_type=F32)


def _split3(a):
    hi = a.astype(BF16)
    r1 = a - hi.astype(F32)
    mid = r1.astype(BF16)
    lo = (r1 - mid.astype(F32)).astype(BF16)
    return hi, mid, lo


def _inproj_kernel(x_ref, pos_ref, invf_ref, g_ref, wqk_ref, wv_ref, wg_ref, wga_ref,
                   wa2_ref, ba_ref,
                   q_ref, k_ref, v_ref, gq_ref, gk_ref, gv_ref, gr_ref, la_ref):
    h = _rms(x_ref[...], g_ref[...]).astype(BF16)
    ang = pos_ref[...].astype(F32) * invf_ref[...]
    lane = lax.broadcasted_iota(jnp.int32, ang.shape, 1) % DA_DIM
    half = ROPE_DIM // 2
    cosv = jnp.where(lane < ROPE_DIM, jnp.cos(ang), 1.0)
    sinv = jnp.sin(ang)
    s_lo = jnp.where(lane < half, -sinv, 0.0)
    s_hi = jnp.where((lane >= half) & (lane < ROPE_DIM), sinv, 0.0)

    qk = _dot(h, wqk_ref[...])
    n_grp = qk.shape[1] // LANES
    for j in range(n_grp):
        t = qk[:, j * LANES:(j + 1) * LANES]
        rot = (t * cosv + pltpu.roll(t, LANES - half, 1) * s_lo
               + pltpu.roll(t, half, 1) * s_hi)
        if j < n_grp // 2:
            q_ref[:, j * LANES:(j + 1) * LANES] = (rot * Q_SCALE).astype(BF16)
        else:
            jj = j - n_grp // 2
            k_ref[:, jj * LANES:(jj + 1) * LANES] = rot.astype(BF16)
    v_ref[...] = _dot(h, wv_ref[...]).astype(BF16)
    gg = _dot(h, wg_ref[...])
    gq_ref[...] = gg[:, 0:256]
    gk_ref[...] = gg[:, 256:512]
    gv_ref[...] = gg[:, 512:1024]
    gr_ref[...] = gg[:, 1024:1536]
    ga = _dot(h, wga_ref[...])
    z = _dot(ga.astype(BF16), wa2_ref[...]) + ba_ref[...]
    la_ref[...] = (jnp.minimum(z, 0.0) - jnp.log1p(jnp.exp(-jnp.abs(z)))) * (1.0 / GLA_TAU)


def _inproj(x2d, pos2d, invf, g, wqk, wv, wg, wga, wa2, ba, tm=256):
    n = x2d.shape[0]
    row = lambda w: pl.BlockSpec((tm, w), lambda i: (i, 0))
    full = lambda a: pl.BlockSpec(a.shape, lambda i: (0,) * a.ndim)
    outs = [(512, BF16), (512, BF16), (512, BF16), (256, F32), (256, F32), (512, F32),
            (512, F32), (256, F32)]
    return pl.pallas_call(
        _inproj_kernel,
        grid=(n // tm,),
        in_specs=[row(D_MODEL), row(1), full(invf), full(g), full(wqk), full(wv), full(wg),
                  full(wga), full(wa2), full(ba)],
        out_specs=[row(w) for w, _ in outs],
        out_shape=[jax.ShapeDtypeStruct((n, w), dt) for w, dt in outs],
        compiler_params=_params(("arbitrary",)),
        name="inproj",
    )(x2d, pos2d, invf, g, wqk, wv, wg, wga, wa2, ba)


def _diffattn_kernel(lq1_ref, lk1_ref, lq2_ref, lk2_ref, gn_ref, q_ref, k_ref, v_ref, o_ref,
                     s_ref, mx_ref, l_ref, acc_ref, *, tq):
    qi = pl.program_id(2)
    q = q_ref[...]
    lane = lax.broadcasted_iota(jnp.int32, q.shape, 1)
    zero = jnp.zeros_like(q)
    qs = jnp.concatenate([jnp.where(lane < DA_DIM, q, zero),
                          jnp.where(lane >= DA_DIM, q, zero)], axis=0)
    mx_ref[...] = jnp.full_like(mx_ref, -jnp.inf)

    def scores(kb, masked):
        start = pl.multiple_of(kb * tq, tq)
        s = _dot_nt(qs, k_ref[pl.ds(start, tq), :])
        if masked:
            r = lax.broadcasted_iota(jnp.int32, s.shape, 0) % tq
            c = lax.broadcasted_iota(jnp.int32, s.shape, 1)
            s = jnp.where(c <= r, s, -jnp.inf)
        s_ref[kb] = s
        part = s[:, 0:LANES]
        for j in range(1, tq // LANES):
            part = jnp.maximum(part, s[:, j * LANES:(j + 1) * LANES])
        mx_ref[...] = jnp.maximum(mx_ref[...], part)

    def score_pair(pb, carry):
        scores(2 * pb, False)
        scores(2 * pb + 1, False)
        return carry

    lax.fori_loop(0, qi // 2, score_pair, 0)

    @pl.when(qi % 2 == 1)
    def _():
        scores(qi - 1, False)

    scores(qi, True)

    mx_ref[...] = jnp.broadcast_to(jnp.max(mx_ref[...], axis=-1, keepdims=True), mx_ref.shape)
    l_ref[...] = jnp.zeros_like(l_ref)
    acc_ref[...] = jnp.zeros_like(acc_ref)

    def pv(kb):
        start = pl.multiple_of(kb * tq, tq)
        m = mx_ref[...]
        p = jnp.exp2(s_ref[kb] - jnp.concatenate([m] * (tq // LANES), axis=1))
        part = p[:, 0:LANES]
        for j in range(1, tq // LANES):
            part = part + p[:, j * LANES:(j + 1) * LANES]
        l_ref[...] += part
        acc_ref[...] += _dot(p.astype(BF16), v_ref[pl.ds(start, tq), :])

    def pv_pair(pb, carry):
        pv(2 * pb)
        pv(2 * pb + 1)
        return carry

    lax.fori_loop(0, (qi + 1) // 2, pv_pair, 0)

    @pl.when(qi % 2 == 0)
    def _():
        pv(qi)

    lam = (jnp.exp(jnp.sum(lq1_ref[...] * lk1_ref[...], axis=-1, keepdims=True))
           - jnp.exp(jnp.sum(lq2_ref[...] * lk2_ref[...], axis=-1, keepdims=True))
           + LAMBDA_INIT)
    on = acc_ref[...] / jnp.sum(l_ref[...], axis=-1, keepdims=True)
    o = on[0:tq] - lam * on[tq:2 * tq]
    o_ref[...] = (_rms(o, gn_ref[...]) * (1.0 - LAMBDA_INIT)).astype(o_ref.dtype)


def _diffattn(q, k, v, lq1, lk1, lq2, lk2, gn, batch, seq, tq=256):
    nq = seq // tq
    vec = lambda a: pl.BlockSpec(a.shape, lambda b, h, i: (0, 0))
    return pl.pallas_call(
        functools.partial(_diffattn_kernel, tq=tq),
        grid=(batch, DA_HEADS, nq),
        in_specs=[vec(lq1), vec(lk1), vec(lq2), vec(lk2), vec(gn),
                  pl.BlockSpec((tq, LANES), lambda b, h, i: (b * nq + i, h)),
                  pl.BlockSpec((seq, LANES), lambda b, h, i: (b, h)),
                  pl.BlockSpec((seq, LANES), lambda b, h, i: (b, h))],
        out_specs=pl.BlockSpec((tq, LANES), lambda b, h, i: (b * nq + i, h)),
        out_shape=jax.ShapeDtypeStruct(q.shape, BF16),
        scratch_shapes=[pltpu.VMEM((nq, 2 * tq, tq), F32),
                        pltpu.VMEM((2 * tq, LANES), F32),
                        pltpu.VMEM((2 * tq, LANES), F32),
                        pltpu.VMEM((2 * tq, LANES), F32)],
        compiler_params=_params(("arbitrary", "arbitrary", "arbitrary")),
        name="diffattn",
    )(lq1, lk1, lq2, lk2, gn, q, k, v)


def _gla_kernel(gq_ref, gk_ref, gv_ref, gr_ref, la_ref, gn_ref, o_ref, st_ref, *, rows):
    c = GLA_CHUNK
    pw = 2 * GLA_DK
    vw = 2 * GLA_DV

    @pl.when(pl.program_id(1) == 0)
    def _():
        st_ref[...] = jnp.zeros_like(st_ref)

    r_i = lax.broadcasted_iota(jnp.int32, (c, c), 0)
    c_i = lax.broadcasted_iota(jnp.int32, (c, c), 1)
    causal = c_i <= r_i
    tril = causal.astype(BF16)
    sr = lax.broadcasted_iota(jnp.int32, (pw, vw), 0) // GLA_DK
    sc = lax.broadcasted_iota(jnp.int32, (pw, vw), 1) // GLA_DV
    blockdiag = sr == sc
    klane = lax.broadcasted_iota(jnp.int32, (c, pw), 1) // GLA_DK

    for ci in range(rows // c):
        rs = slice(ci * c, (ci + 1) * c)
        for p in range(GLA_HEADS // 2):
            ks = slice(p * pw, (p + 1) * pw)
            vs = slice(p * vw, (p + 1) * vw)
            la = la_ref[rs, ks]
            hi, mid, lo = _split3(la)
            b = _dot(tril, hi) + _dot(tril, mid) + _dot(tril, lo)
            b_last = b[c - 1:c, :]
            qe = gq_ref[rs, ks] * jnp.exp(b) * (GLA_DK ** -0.5)
            gk = gk_ref[rs, ks]
            kn = (gk * jnp.exp(-b)).astype(BF16)
            kd = gk * jnp.exp(b_last - b)
            v = gv_ref[rs, vs].astype(BF16)
            st = st_ref[p]
            inter = _dot(qe.astype(BF16), st.astype(BF16))
            outs = []
            for hh in range(2):
                qh = jnp.where(klane == hh, qe, 0.0).astype(BF16)
                attn = jnp.where(causal, _dot_nt(qh, kn), 0.0)
                outs.append(_dot(attn.astype(BF16), v[:, hh * GLA_DV:(hh + 1) * GLA_DV]))
            o = inter + jnp.concatenate(outs, axis=1)
            upd = _dot(kd.T.astype(BF16), v)
            decay = jnp.exp(b_last).T
            st_ref[p] = st * decay + jnp.where(blockdiag, upd, 0.0)
            for hh in range(2):
                hs = slice(hh * GLA_DV, (hh + 1) * GLA_DV)
                os_ = slice(p * vw + hh * GLA_DV, p * vw + (hh + 1) * GLA_DV)
                gr = gr_ref[rs, os_]
                y = _rms(o[:, hs], gn_ref[...]) * (gr * jax.nn.sigmoid(gr))
                o_ref[rs, os_] = y.astype(o_ref.dtype)


def _gla(gq, gk, gv, gr, la, gn, batch, seq, rows=256):
    nb = seq // rows
    row = lambda w: pl.BlockSpec((rows, w), lambda b, i: (b * nb + i, 0))
    return pl.pallas_call(
        functools.partial(_gla_kernel, rows=rows),
        grid=(batch, nb),
        in_specs=[row(256), row(256), row(512), row(512), row(256),
                  pl.BlockSpec(gn.shape, lambda b, i: (0, 0))],
        out_specs=row(512),
        out_shape=jax.ShapeDtypeStruct(gv.shape, BF16),
        scratch_shapes=[pltpu.VMEM((GLA_HEADS // 2, 2 * GLA_DK, 2 * GLA_DV), F32)],
        compiler_params=_params(("arbitrary", "arbitrary")),
        name="gla",
    )(gq, gk, gv, gr, la, gn)


def _outproj_kernel(oda_ref, ogla_ref, x_ref, wo_ref, g_ref, wcq_ref, x1_ref, qc_ref):
    half = oda_ref.shape[1]
    mix = _dot(oda_ref[...], wo_ref[0:half, :]) + _dot(ogla_ref[...], wo_ref[half:, :])
    x1 = x_ref[...] + mix
    x1_ref[...] = x1
    hq = _rms(x1, g_ref[...]).astype(BF16)
    qc_ref[...] = (_dot(hq, wcq_ref[...]) * (X_DIM ** -0.5)).astype(BF16)


def _outproj(oda, ogla, x2d, wo, g, wcq, tm=256):
    n = x2d.shape[0]
    row = lambda w: pl.BlockSpec((tm, w), lambda i: (i, 0))
    full = lambda a: pl.BlockSpec(a.shape, lambda i: (0,) * a.ndim)
    return pl.pallas_call(
        _outproj_kernel,
        grid=(n // tm,),
        in_specs=[row(512), row(512), row(D_MODEL), full(wo), full(g), full(wcq)],
        out_specs=[row(D_MODEL), row(D_MODEL)],
        out_shape=[jax.ShapeDtypeStruct((n, D_MODEL), F32),
                   jax.ShapeDtypeStruct((n, D_MODEL), BF16)],
        compiler_params=_params(("arbitrary",)),
        name="outproj",
    )(oda, ogla, x2d, wo, g, wcq)


def _memkv_kernel(m_ref, g_ref, wk_ref, wv_ref, k_ref, v_ref):
    hm = _rms(m_ref[...], g_ref[...]).astype(BF16)
    k_ref[...] = _dot(hm, wk_ref[...]).astype(BF16)
    v_ref[...] = _dot(hm, wv_ref[...]).astype(BF16)


def _memkv(mem2d, g, wk, wv, tm=256):
    n = mem2d.shape[0]
    row = pl.BlockSpec((tm, D_MODEL), lambda i: (i, 0))
    full = lambda a: pl.BlockSpec(a.shape, lambda i: (0,) * a.ndim)
    return pl.pallas_call(
        _memkv_kernel,
        grid=(n // tm,),
        in_specs=[row, full(g), full(wk), full(wv)],
        out_specs=[row, row],
        out_shape=[jax.ShapeDtypeStruct((n, D_MODEL), BF16)] * 2,
        compiler_params=_params(("arbitrary",)),
        name="memkv",
    )(mem2d, g, wk, wv)


def _cross_kernel(qc_ref, kc_ref, vc_ref, x1_ref, wco_ref, g_ref, wrt_ref, br_ref,
                  x2_ref, t_ref, e_ref, gate_ref, rank_ref, cnt_ref):
    tq = qc_ref.shape[0]
    outs = []
    for h in range(X_HEADS):
        hs = slice(h * X_DIM, (h + 1) * X_DIM)
        s = _dot_nt(qc_ref[:, hs], kc_ref[:, hs])
        m = jnp.max(s, axis=-1, keepdims=True)
        p = jnp.exp(s - m)
        l = jnp.sum(p, axis=-1, keepdims=True)
        outs.append((_dot(p.astype(BF16), vc_ref[:, hs]) / l).astype(BF16))
    o = jnp.concatenate(outs, axis=1)
    x2 = x1_ref[...] + _dot(o, wco_ref[...])
    x2_ref[...] = x2
    t = _rms(x2, g_ref[...])
    t_ref[...] = t.astype(t_ref.dtype)

    t_hi, t_mid, _ = _split3(t)
    w_hi, w_mid, _ = _split3(wrt_ref[...])
    logit = (_dot_nt(w_hi, t_hi) + _dot_nt(w_hi, t_mid) + _dot_nt(w_mid, t_hi)) + br_ref[...]
    iota_e = lax.broadcasted_iota(jnp.int32, logit.shape, 0)
    vals, idxs, sels = [], [], []
    for _ in range(TOP_K):
        mx = jnp.max(logit, axis=0, keepdims=True)
        idx = jnp.min(jnp.where(logit == mx, iota_e, N_EXPERTS), axis=0, keepdims=True)
        sel = iota_e == idx
        vals.append(mx)
        idxs.append(idx)
        sels.append(sel)
        logit = jnp.where(sel, -jnp.inf, logit)
    ex = [jnp.exp(v - vals[0]) for v in vals]
    den = ex[0] + ex[1] + ex[2] + ex[3]
    gate_ref[...] = jnp.concatenate([e / den for e in ex], axis=0)
    e_ref[...] = jnp.concatenate(idxs, axis=0)
    onehot = (sels[0] | sels[1] | sels[2] | sels[3])
    cnt = onehot.astype(BF16)
    ur = lax.broadcasted_iota(jnp.int32, (tq, tq), 0)
    uc = lax.broadcasted_iota(jnp.int32, (tq, tq), 1)
    before = (ur < uc).astype(BF16)
    base = _dot(cnt, before)
    rank_ref[...] = jnp.concatenate(
        [jnp.sum(jnp.where(s_, base, 0.0), axis=0, keepdims=True) for s_ in sels],
        axis=0).astype(jnp.int32)
    cnt_ref[...] = jnp.sum(onehot.astype(F32), axis=1, keepdims=True)


def _cross(qc, kc, vc, x1, wco, g, wrt, br, batch, seq, mem_len, tq=256):
    nq = seq // tq
    n = batch * seq
    row = lambda w: pl.BlockSpec((tq, w), lambda b, i: (b * nq + i, 0))
    col = lambda r: pl.BlockSpec((r, tq), lambda b, i: (0, b * nq + i))
    full = lambda a: pl.BlockSpec(a.shape, lambda b, i: (0,) * a.ndim)
    memb = pl.BlockSpec((mem_len, D_MODEL), lambda b, i: (b, 0))
    return pl.pallas_call(
        _cross_kernel,
        grid=(batch, nq),
        in_specs=[row(D_MODEL), memb, memb, row(D_MODEL), full(wco), full(g), full(wrt),
                  full(br)],
        out_specs=[row(D_MODEL), row(D_MODEL), col(TOP_K), col(TOP_K), col(TOP_K),
                   pl.BlockSpec((None, N_EXPERTS, 1), lambda b, i: (b * nq + i, 0, 0))],
        out_shape=[jax.ShapeDtypeStruct((n, D_MODEL), F32),
                   jax.ShapeDtypeStruct((n, D_MODEL), BF16),
                   jax.ShapeDtypeStruct((TOP_K, n), jnp.int32),
                   jax.ShapeDtypeStruct((TOP_K, n), F32),
                   jax.ShapeDtypeStruct((TOP_K, n), jnp.int32),
                   jax.ShapeDtypeStruct((n // tq, N_EXPERTS, 1), F32)],
        compiler_params=_params(("arbitrary", "arbitrary")),
        name="cross_router",
    )(qc, kc, vc, x1, wco, g, wrt, br)


def _chunk_copy(hbm_ref, dst_ref, loc_ref, sem, tile, j, s, mc, to_hbm):
    loc = loc_ref.at[s, pl.ds(pl.multiple_of(j * ROW_GRAN, ROW_GRAN), ROW_GRAN)]
    far = hbm_ref.at[pl.ds(pl.multiple_of(dst_ref[tile * mc + j], ROW_GRAN), ROW_GRAN)]
    return pltpu.make_async_copy(loc, far, sem.at[s]) if to_hbm else \
        pltpu.make_async_copy(far, loc, sem.at[s])


def _dispatch_kernel(dst_ref, nct_ref, tail_ref, slot_ref, t_ref, xp_ref, xloc_ref, zero_ref,
                     sem, zsem, *, blk, mc):
    i = pl.program_id(0)
    s = i % 2
    lmax = xloc_ref.shape[1]

    @pl.when(i == 0)
    def _():
        zero_ref[...] = jnp.zeros_like(zero_ref)
        tails = [pl.multiple_of(tail_ref[e], blk) for e in range(N_EXPERTS)]
        for e in range(N_EXPERTS):
            pltpu.make_async_copy(zero_ref, xp_ref.at[pl.ds(tails[e], blk)], zsem).start()
        for e in range(N_EXPERTS):
            pltpu.make_async_copy(zero_ref, xp_ref.at[pl.ds(tails[e], blk)], zsem).wait()

    rows = lax.broadcasted_iota(jnp.int32, (lmax, t_ref.shape[0]), 0)
    hit = rows == slot_ref[0:1, :]
    for k in range(1, TOP_K):
        hit = hit | (rows == slot_ref[k:k + 1, :])
    xloc_ref[s] = _dot(jnp.where(hit, 1.0, 0.0).astype(BF16), t_ref[...])

    def issue(j, carry):
        _chunk_copy(xp_ref, dst_ref, xloc_ref, sem, i, j, s, mc, True).start()
        return carry

    lax.fori_loop(0, nct_ref[i], issue, 0)

    def drain(tile, slot):
        def body(j, carry):
            _chunk_copy(xp_ref, dst_ref, xloc_ref, sem, tile, 0, slot, mc, True).wait()
            return carry
        lax.fori_loop(0, nct_ref[tile], body, 0)

    @pl.when(i > 0)
    def _():
        drain(i - 1, 1 - s)

    @pl.when(i == pl.num_programs(0) - 1)
    def _():
        drain(i, s)


def _dispatch(dst_tab, nct, tail, slot, t, cap, blk, lmax, tm):
    n, d = t.shape
    mc = lmax // ROW_GRAN
    return pl.pallas_call(
        functools.partial(_dispatch_kernel, blk=blk, mc=mc),
        grid_spec=pltpu.PrefetchScalarGridSpec(
            num_scalar_prefetch=3,
            grid=(n // tm,),
            in_specs=[pl.BlockSpec((TOP_K, tm), lambda i, *_: (0, i)),
                      pl.BlockSpec((tm, d), lambda i, *_: (i, 0))],
            out_specs=pl.BlockSpec(memory_space=pl.ANY),
            scratch_shapes=[pltpu.VMEM((2, lmax, d), F32), pltpu.VMEM((blk, d), F32),
                            pltpu.SemaphoreType.DMA((2,)), pltpu.SemaphoreType.DMA(())]),
        out_shape=jax.ShapeDtypeStruct((cap, d), F32),
        compiler_params=_params(("arbitrary",)),
        name="moe_dispatch",
    )(dst_tab, nct, tail, slot, t)


def _expert_kernel(blk_e_ref, nused_ref, x_ref, wu_ref, bu_ref, wd_ref, bd_ref, y_ref):
    del blk_e_ref

    @pl.when(pl.program_id(0) < nused_ref[0])
    def _():
        x = x_ref[...].astype(BF16)
        u = _dot(x, wu_ref[...]) + bu_ref[...]
        glu = jnp.minimum(u[:, :D_FF], SWIGLU_LIMIT)
        lin = jnp.clip(u[:, D_FF:], -SWIGLU_LIMIT, SWIGLU_LIMIT)
        act = glu * jax.nn.sigmoid(SWIGLU_ALPHA * glu) * (lin + 1.0)
        y_ref[...] = _dot(act.astype(BF16), wd_ref[...]) + bd_ref[...]


def _experts(blk_e, n_used, x_pad, wu, bu, wd, bd, blk):
    cap = x_pad.shape[0]
    n_blk = cap // blk
    rowmap = lambda i, be, nu: (jnp.minimum(i, nu[0] - 1), 0)
    emap = lambda i, be, nu: (be[i], 0, 0)
    return pl.pallas_call(
        _expert_kernel,
        grid_spec=pltpu.PrefetchScalarGridSpec(
            num_scalar_prefetch=2,
            grid=(n_blk,),
            in_specs=[pl.BlockSpec((blk, D_MODEL), rowmap),
                      pl.BlockSpec((None, D_MODEL, 2 * D_FF), emap),
                      pl.BlockSpec((None, 1, 2 * D_FF), emap),
                      pl.BlockSpec((None, D_FF, D_MODEL), emap),
                      pl.BlockSpec((None, 1, D_MODEL), emap)],
            out_specs=pl.BlockSpec((blk, D_MODEL), rowmap)),
        out_shape=jax.ShapeDtypeStruct((cap, D_MODEL), F32),
        compiler_params=_params(("arbitrary",)),
        name="moe_experts",
    )(blk_e, n_used, x_pad, wu, bu, wd, bd)


def _combine_kernel(dst_ref, nct_ref, slot_ref, gate_ref, x2_ref, g_ref, y_ref, o_ref,
                    yloc_ref, sem, *, mc):
    i = pl.program_id(0)
    s = i % 2
    lmax = yloc_ref.shape[1]

    def issue(tile, slot):
        def body(j, carry):
            _chunk_copy(y_ref, dst_ref, yloc_ref, sem, tile, j, slot, mc, False).start()
            return carry
        lax.fori_loop(0, nct_ref[tile], body, 0)

    @pl.when(i == 0)
    def _():
        yloc_ref[...] = jnp.zeros_like(yloc_ref)
        issue(0, 0)

    @pl.when(i + 1 < pl.num_programs(0))
    def _():
        issue(i + 1, 1 - s)

    def drain(j, carry):
        _chunk_copy(y_ref, dst_ref, yloc_ref, sem, i, 0, s, mc, False).wait()
        return carry

    lax.fori_loop(0, nct_ref[i], drain, 0)

    cols = lax.broadcasted_iota(jnp.int32, (x2_ref.shape[0], lmax), 1)
    w = jnp.where(cols == slot_ref[:, 0:1], gate_ref[:, 0:1], 0.0)
    for k in range(1, TOP_K):
        w = w + jnp.where(cols == slot_ref[:, k:k + 1], gate_ref[:, k:k + 1], 0.0)
    acc = x2_ref[...] + _dot(w.astype(BF16), yloc_ref[s].astype(BF16))
    o_ref[...] = _rms(acc, g_ref[...])


def _combine(dst_tab, nct, slot_t, gate_t, x2, g, y_pad, lmax, tm):
    n = x2.shape[0]
    mc = lmax // ROW_GRAN
    row = pl.BlockSpec((tm, D_MODEL), lambda i, *_: (i, 0))
    col = pl.BlockSpec((tm, TOP_K), lambda i, *_: (i, 0))
    return pl.pallas_call(
        functools.partial(_combine_kernel, mc=mc),
        grid_spec=pltpu.PrefetchScalarGridSpec(
            num_scalar_prefetch=2,
            grid=(n // tm,),
            in_specs=[col, col, row, pl.BlockSpec(g.shape, lambda i, *_: (0, 0)),
                      pl.BlockSpec(memory_space=pl.ANY)],
            out_specs=row,
            scratch_shapes=[pltpu.VMEM((2, lmax, D_MODEL), F32),
                            pltpu.SemaphoreType.DMA((2,))]),
        out_shape=jax.ShapeDtypeStruct((n, D_MODEL), F32),
        compiler_params=_params(("arbitrary",)),
        name="moe_combine",
    )(dst_tab, nct, slot_t, gate_t, x2, g, y_pad)


def _rope_inv_freq():
    inv = ROPE_THETA ** (-np.arange(0, ROPE_DIM, 2, dtype=np.float32) / ROPE_DIM)
    lane = np.arange(LANES) % DA_DIM
    tab = np.where(lane < ROPE_DIM, inv.astype(np.float32)[lane % (ROPE_DIM // 2)], 0.0)
    return jnp.asarray(tab.astype(np.float32)[None, :])


def kernel(x, mem, positions, norm_mix_g, w_in, lambda_q1, lambda_k1, lambda_q2, lambda_k2, diff_norm_g, w_alpha2, b_alpha, gla_norm_g, w_out, norm_cross_g, norm_mem_g, w_cq, w_ck, w_cv, w_co, norm_ffn_g, w_router, b_router, w_up, b_up, w_down, b_down, norm_final_g):
    batch, seq, d = x.shape
    mem_len = mem.shape[1]
    n = batch * seq
    moe_blk = 256
    row = lambda a: a.reshape(1, -1)

    x2d = x.reshape(n, d)
    w = w_in[0]
    wqk, wv = w[:, :1024].astype(BF16), w[:, 1024:1536].astype(BF16)
    wg, wga = w[:, 1536:3072].astype(BF16), w[:, 3072:].astype(BF16)
    q, k, v, gq, gk, gv, gr, la = _inproj(
        x2d, positions.reshape(n, 1), _rope_inv_freq(), row(norm_mix_g[0]), wqk, wv, wg, wga,
        w_alpha2[0].astype(BF16), row(b_alpha[0]))

    o_da = _diffattn(q, k, v, row(lambda_q1[0]), row(lambda_k1[0]), row(lambda_q2[0]),
                     row(lambda_k2[0]), row(diff_norm_g[0]), batch, seq)
    o_gla = _gla(gq, gk, gv, gr, la, row(gla_norm_g[0]), batch, seq)

    x1, qc = _outproj(o_da, o_gla, x2d, w_out[0].astype(BF16), row(norm_cross_g[0]),
                      w_cq[0].astype(BF16))
    kc, vc = _memkv(mem.reshape(batch * mem_len, d), row(norm_mem_g[0]),
                    w_ck[0].astype(BF16), w_cv[0].astype(BF16))
    tile = 256
    x2, t, top_e, gate, lrank, counts = _cross(
        qc, kc, vc, x1, w_co[0].astype(BF16), row(norm_ffn_g[0]), w_router[0].T,
        b_router[0].reshape(-1, 1), batch, seq, mem_len, tq=tile)

    nt = n // tile
    lmax = -(-(TOP_K * tile + N_EXPERTS * (ROW_GRAN - 1)) // LANES) * LANES
    mc = lmax // ROW_GRAN
    cnt = counts[:, :, 0].astype(jnp.int32)
    plc = (cnt + ROW_GRAN - 1) // ROW_GRAN * ROW_GRAN
    lend = jnp.cumsum(plc, axis=1)
    lstart = lend - plc
    tile_off = jnp.cumsum(plc, axis=0) - plc
    etot = jnp.sum(plc, axis=0)
    eblk = (etot + moe_blk - 1) // moe_blk * moe_blk
    gend = jnp.cumsum(eblk)
    dst0 = (gend - eblk)[None, :] + tile_off
    cap = (nt * (TOP_K * tile + N_EXPERTS * (ROW_GRAN - 1)) + N_EXPERTS * (moe_blk - 1))
    cap = -(-cap // moe_blk) * moe_blk
    n_blk = cap // moe_blk
    n_used = (gend[-1] // moe_blk).astype(jnp.int32)
    blk_ids = jnp.minimum(jnp.arange(n_blk, dtype=jnp.int32), n_used - 1)
    blk_e = jnp.minimum(jnp.sum(gend[None, :] <= (blk_ids * moe_blk)[:, None], axis=1),
                        N_EXPERTS - 1).astype(jnp.int32)
    tail = jnp.where(etot > 0, gend, gend[-1]).astype(jnp.int32) - moe_blk
    eids = jnp.arange(N_EXPERTS, dtype=jnp.int32)[:, None, None]
    lstart_tok = jnp.repeat(lstart.T, tile, axis=1)[:, None, :]
    slot = (lrank + jnp.sum(jnp.where(top_e[None] == eids, lstart_tok, 0), axis=0)
            ).astype(jnp.int32)
    crow = jnp.arange(mc, dtype=jnp.int32) * ROW_GRAN
    ce = jnp.minimum(jnp.sum(lend[:, None, :] <= crow[None, :, None], axis=2), N_EXPERTS - 1)
    csel = ce[:, :, None] == jnp.arange(N_EXPERTS, dtype=jnp.int32)[None, None, :]
    dst_tab = (crow[None, :] + jnp.sum(jnp.where(csel, (dst0 - lstart)[:, None, :], 0), axis=2)
               ).astype(jnp.int32).reshape(-1)
    nct = (lend[:, -1] // ROW_GRAN).astype(jnp.int32)

    x_pad = _dispatch(dst_tab, nct, tail, slot, t, cap, moe_blk, lmax, tile)
    y_pad = _experts(blk_e, n_used.reshape(1), x_pad, w_up[0].astype(BF16),
                     b_up[0][:, None, :], w_down[0].astype(BF16), b_down[0][:, None, :],
                     moe_blk)
    out = _combine(dst_tab, nct, slot.T, gate.T, x2, row(norm_final_g), y_pad, lmax, tile)
    return out.reshape(batch, seq, d)
```

```python
import functools

import numpy as np
import jax
import jax.numpy as jnp
from jax import lax
from jax.experimental import pallas as pl
from jax.experimental.pallas import tpu as pltpu

EPS = 1e-6
D_MODEL = 1024
DA_HEADS = 4
DA_DIM = 64
GLA_HEADS = 4
GLA_DK = 64
GLA_DV = 128
GLA_RANK = 16
GLA_TAU = 16.0
GLA_CHUNK = 64
ROPE_THETA = 500000.0
ROPE_DIM = DA_DIM // 4
X_HEADS = 4
X_DIM = D_MODEL // X_HEADS
N_EXPERTS = 32
TOP_K = 4
D_FF = D_MODEL
SWIGLU_LIMIT = 7.0
SWIGLU_ALPHA = 1.702
LAMBDA_INIT = 0.8 - 0.6 * 1.0
Q_SCALE = DA_DIM ** -0.5 * float(np.log2(np.e))

LANES = 128
ROW_GRAN = 8
BF16 = jnp.bfloat16
F32 = jnp.float32
VMEM_LIMIT = 48 * 1024 * 1024


def _params(sem, vmem=VMEM_LIMIT):
    return pltpu.CompilerParams(dimension_semantics=sem, vmem_limit_bytes=vmem)


def _rms(xf, g):
    return xf * lax.rsqrt(jnp.mean(xf * xf, axis=-1, keepdims=True) + EPS) * g


def _dot(a, b):
    return jnp.dot(a, b, preferred_element_type=F32)


def _dot_nt(a, b):
    return lax.dot_general(a, b, (((1,), (1,)), ((), ())), preferred_element_type=F32)


def _split3(a):
    hi = a.astype(BF16)
    r1 = a - hi.astype(F32)
    mid = r1.astype(BF16)
    lo = (r1 - mid.astype(F32)).astype(BF16)
    return hi, mid, lo


def _inproj_kernel(x_ref, pos_ref, invf_ref, g_ref, wqk_ref, wv_ref, wg_ref, wga_ref,
                   wa2_ref, ba_ref,
                   q_ref, k_ref, v_ref, gq_ref, gk_ref, gv_ref, gr_ref, la_ref):
    h = _rms(x_ref[...], g_ref[...]).astype(BF16)
    ang = pos_ref[...].astype(F32) * invf_ref[...]
    lane = lax.broadcasted_iota(jnp.int32, ang.shape, 1) % DA_DIM
    half = ROPE_DIM // 2
    cosv = jnp.where(lane < ROPE_DIM, jnp.cos(ang), 1.0)
    sinv = jnp.sin(ang)
    s_lo = jnp.where(lane < half, -sinv, 0.0)
    s_hi = jnp.where((lane >= half) & (lane < ROPE_DIM), sinv, 0.0)

    qk = _dot(h, wqk_ref[...])
    n_grp = qk.shape[1] // LANES
    for j in range(n_grp):
        t = qk[:, j * LANES:(j + 1) * LANES]
        rot = (t * cosv + pltpu.roll(t, LANES - half, 1) * s_lo
               + pltpu.roll(t, half, 1) * s_hi)
        if j < n_grp // 2:
            q_ref[:, j * LANES:(j + 1) * LANES] = (rot * Q_SCALE).astype(BF16)
        else:
            jj = j - n_grp // 2
            k_ref[:, jj * LANES:(jj + 1) * LANES] = rot.astype(BF16)
    v_ref[...] = _dot(h, wv_ref[...]).astype(BF16)
    gg = _dot(h, wg_ref[...])
    gq_ref[...] = gg[:, 0:256]
    gk_ref[...] = gg[:, 256:512]
    gv_ref[...] = gg[:, 512:1024]
    gr_ref[...] = gg[:, 1024:1536]
    ga = _dot(h, wga_ref[...])
    z = _dot(ga.astype(BF16), wa2_ref[...]) + ba_ref[...]
    la_ref[...] = (jnp.minimum(z, 0.0) - jnp.log1p(jnp.exp(-jnp.abs(z)))) * (1.0 / GLA_TAU)


def _inproj(x2d, pos2d, invf, g, wqk, wv, wg, wga, wa2, ba, tm=256):
    n = x2d.shape[0]
    row = lambda w: pl.BlockSpec((tm, w), lambda i: (i, 0))
    full = lambda a: pl.BlockSpec(a.shape, lambda i: (0,) * a.ndim)
    outs = [(512, BF16), (512, BF16), (512, BF16), (256, F32), (256, F32), (512, F32),
            (512, F32), (256, F32)]
    return pl.pallas_call(
        _inproj_kernel,
        grid=(n // tm,),
        in_specs=[row(D_MODEL), row(1), full(invf), full(g), full(wqk), full(wv), full(wg),
                  full(wga), full(wa2), full(ba)],
        out_specs=[row(w) for w, _ in outs],
        out_shape=[jax.ShapeDtypeStruct((n, w), dt) for w, dt in outs],
        compiler_params=_params(("arbitrary",)),
        name="inproj",
    )(x2d, pos2d, invf, g, wqk, wv, wg, wga, wa2, ba)


def _diffattn_kernel(lq1_ref, lk1_ref, lq2_ref, lk2_ref, gn_ref, q_ref, k_ref, v_ref, o_ref,
                     *, tq):
    seq = q_ref.shape[0]
    lam = (jnp.exp(jnp.sum(lq1_ref[...] * lk1_ref[...], axis=-1, keepdims=True))
           - jnp.exp(jnp.sum(lq2_ref[...] * lk2_ref[...], axis=-1, keepdims=True))
           + LAMBDA_INIT)
    lane = lax.broadcasted_iota(jnp.int32, (tq, LANES), 1)
    r = lax.broadcasted_iota(jnp.int32, (2 * tq, tq), 0) % tq
    c = lax.broadcasted_iota(jnp.int32, (2 * tq, tq), 1)
    causal = c <= r
    for qi in range(seq // tq):
        q = q_ref[qi * tq:(qi + 1) * tq, :]
        zero = jnp.zeros_like(q)
        qs = jnp.concatenate([jnp.where(lane < DA_DIM, q, zero),
                              jnp.where(lane >= DA_DIM, q, zero)], axis=0)
        past = qi * tq
        s_diag = jnp.where(causal, _dot_nt(qs, k_ref[past:past + tq, :]), -jnp.inf)
        m = jnp.max(s_diag, axis=-1, keepdims=True)
        if qi > 0:
            s_past = _dot_nt(qs, k_ref[0:past, :])
            m = jnp.maximum(m, jnp.max(s_past, axis=-1, keepdims=True))
            p = jnp.exp2(jnp.concatenate([s_past, s_diag], axis=1) - m)
        else:
            p = jnp.exp2(s_diag - m)
        l = jnp.sum(p, axis=-1, keepdims=True)
        on = _dot(p.astype(BF16), v_ref[0:past + tq, :]) / l
        o = on[0:tq] - lam * on[tq:2 * tq]
        o_ref[qi * tq:(qi + 1) * tq, :] = (
            _rms(o, gn_ref[...]) * (1.0 - LAMBDA_INIT)).astype(o_ref.dtype)


def _diffattn(q, k, v, lq1, lk1, lq2, lk2, gn, batch, seq, tq=256):
    vec = lambda a: pl.BlockSpec(a.shape, lambda b, h: (0, 0))
    blk = pl.BlockSpec((seq, LANES), lambda b, h: (b, h))
    return pl.pallas_call(
        functools.partial(_diffattn_kernel, tq=tq),
        grid=(batch, DA_HEADS),
        in_specs=[vec(lq1), vec(lk1), vec(lq2), vec(lk2), vec(gn), blk, blk, blk],
        out_specs=blk,
        out_shape=jax.ShapeDtypeStruct(q.shape, BF16),
        compiler_params=_params(("arbitrary", "arbitrary")),
        name="diffattn",
    )(lq1, lk1, lq2, lk2, gn, q, k, v)


def _gla_kernel(gq_ref, gk_ref, gv_ref, gr_ref, la_ref, gn_ref, o_ref, st_ref, *, rows):
    c = GLA_CHUNK
    pw = 2 * GLA_DK
    vw = 2 * GLA_DV

    @pl.when(pl.program_id(1) == 0)
    def _():
        st_ref[...] = jnp.zeros_like(st_ref)

    r_i = lax.broadcasted_iota(jnp.int32, (c, c), 0)
    c_i = lax.broadcasted_iota(jnp.int32, (c, c), 1)
    causal = c_i <= r_i
    tril = causal.astype(BF16)
    sr = lax.broadcasted_iota(jnp.int32, (pw, vw), 0) // GLA_DK
    sc = lax.broadcasted_iota(jnp.int32, (pw, vw), 1) // GLA_DV
    blockdiag = sr == sc
    klane = lax.broadcasted_iota(jnp.int32, (c, pw), 1) // GLA_DK

    for ci in range(rows // c):
        rs = slice(ci * c, (ci + 1) * c)
        for p in range(GLA_HEADS // 2):
            ks = slice(p * pw, (p + 1) * pw)
            vs = slice(p * vw, (p + 1) * vw)
            la = la_ref[rs, ks]
            hi, mid, lo = _split3(la)
            b = _dot(tril, hi) + _dot(tril, mid) + _dot(tril, lo)
            b_last = b[c - 1:c, :]
            qe = gq_ref[rs, ks] * jnp.exp(b) * (GLA_DK ** -0.5)
            gk = gk_ref[rs, ks]
            kn = (gk * jnp.exp(-b)).astype(BF16)
            kd = gk * jnp.exp(b_last - b)
            v = gv_ref[rs, vs].astype(BF16)
            st = st_ref[p]
            inter = _dot(qe.astype(BF16), st.astype(BF16))
            outs = []
            for hh in range(2):
                qh = jnp.where(klane == hh, qe, 0.0).astype(BF16)
                attn = jnp.where(causal, _dot_nt(qh, kn), 0.0)
                outs.append(_dot(attn.astype(BF16), v[:, hh * GLA_DV:(hh + 1) * GLA_DV]))
            o = inter + jnp.concatenate(outs, axis=1)
            upd = _dot(kd.T.astype(BF16), v)
            decay = jnp.exp(b_last).T
            st_ref[p] = st * decay + jnp.where(blockdiag, upd, 0.0)
            for hh in range(2):
                hs = slice(hh * GLA_DV, (hh + 1) * GLA_DV)
                os_ = slice(p * vw + hh * GLA_DV, p * vw + (hh + 1) * GLA_DV)
                gr = gr_ref[rs, os_]
                y = _rms(o[:, hs], gn_ref[...]) * (gr * jax.nn.sigmoid(gr))
                o_ref[rs, os_] = y.astype(o_ref.dtype)


def _gla(gq, gk, gv, gr, la, gn, batch, seq, rows=256):
    nb = seq // rows
    row = lambda w: pl.BlockSpec((rows, w), lambda b, i: (b * nb + i, 0))
    return pl.pallas_call(
        functools.partial(_gla_kernel, rows=rows),
        grid=(batch, nb),
        in_specs=[row(256), row(256), row(512), row(512), row(256),
                  pl.BlockSpec(gn.shape, lambda b, i: (0, 0))],
        out_specs=row(512),
        out_shape=jax.ShapeDtypeStruct(gv.shape, BF16),
        scratch_shapes=[pltpu.VMEM((GLA_HEADS // 2, 2 * GLA_DK, 2 * GLA_DV), F32)],
        compiler_params=_params(("arbitrary", "arbitrary")),
        name="gla",
    )(gq, gk, gv, gr, la, gn)


def _outproj_kernel(oda_ref, ogla_ref, x_ref, wo_ref, g_ref, wcq_ref, x1_ref, qc_ref):
    half = oda_ref.shape[1]
    mix = _dot(oda_ref[...], wo_ref[0:half, :]) + _dot(ogla_ref[...], wo_ref[half:, :])
    x1 = x_ref[...] + mix
    x1_ref[...] = x1
    hq = _rms(x1, g_ref[...]).astype(BF16)
    qc_ref[...] = (_dot(hq, wcq_ref[...]) * (X_DIM ** -0.5)).astype(BF16)


def _outproj(oda, ogla, x2d, wo, g, wcq, tm=256):
    n = x2d.shape[0]
    row = lambda w: pl.BlockSpec((tm, w), lambda i: (i, 0))
    full = lambda a: pl.BlockSpec(a.shape, lambda i: (0,) * a.ndim)
    return pl.pallas_call(
        _outproj_kernel,
        grid=(n // tm,),
        in_specs=[row(512), row(512), row(D_MODEL), full(wo), full(g), full(wcq)],
        out_specs=[row(D_MODEL), row(D_MODEL)],
        out_shape=[jax.ShapeDtypeStruct((n, D_MODEL), F32),
                   jax.ShapeDtypeStruct((n, D_MODEL), BF16)],
        compiler_params=_params(("arbitrary",)),
        name="outproj",
    )(oda, ogla, x2d, wo, g, wcq)


def _memkv_kernel(m_ref, g_ref, wk_ref, wv_ref, k_ref, v_ref):
    hm = _rms(m_ref[...], g_ref[...]).astype(BF16)
    k_ref[...] = _dot(hm, wk_ref[...]).astype(BF16)
    v_ref[...] = _dot(hm, wv_ref[...]).astype(BF16)


def _memkv(mem2d, g, wk, wv, tm=256):
    n = mem2d.shape[0]
    row = pl.BlockSpec((tm, D_MODEL), lambda i: (i, 0))
    full = lambda a: pl.BlockSpec(a.shape, lambda i: (0,) * a.ndim)
    return pl.pallas_call(
        _memkv_kernel,
        grid=(n // tm,),
        in_specs=[row, full(g), full(wk), full(wv)],
        out_specs=[row, row],
        out_shape=[jax.ShapeDtypeStruct((n, D_MODEL), BF16)] * 2,
        compiler_params=_params(("arbitrary",)),
        name="memkv",
    )(mem2d, g, wk, wv)


def _cross_kernel(qc_ref, kc_ref, vc_ref, x1_ref, wco_ref, g_ref, wrt_ref, br_ref,
                  x2_ref, t_ref, e_ref, gate_ref, rank_ref, cnt_ref):
    tq = qc_ref.shape[0]
    outs = []
    for h in range(X_HEADS):
        hs = slice(h * X_DIM, (h + 1) * X_DIM)
        s = _dot_nt(qc_ref[:, hs], kc_ref[:, hs])
        m = jnp.max(s, axis=-1, keepdims=True)
        p = jnp.exp(s - m)
        l = jnp.sum(p, axis=-1, keepdims=True)
        outs.append((_dot(p.astype(BF16), vc_ref[:, hs]) / l).astype(BF16))
    o = jnp.concatenate(outs, axis=1)
    x2 = x1_ref[...] + _dot(o, wco_ref[...])
    x2_ref[...] = x2
    t = _rms(x2, g_ref[...])
    t_ref[...] = t.astype(t_ref.dtype)

    t_hi, t_mid, _ = _split3(t)
    w_hi, w_mid, _ = _split3(wrt_ref[...])
    logit = (_dot_nt(w_hi, t_hi) + _dot_nt(w_hi, t_mid) + _dot_nt(w_mid, t_hi)) + br_ref[...]
    iota_e = lax.broadcasted_iota(jnp.int32, logit.shape, 0)
    vals, idxs, sels = [], [], []
    for _ in range(TOP_K):
        mx = jnp.max(logit, axis=0, keepdims=True)
        idx = jnp.min(jnp.where(logit == mx, iota_e, N_EXPERTS), axis=0, keepdims=True)
        sel = iota_e == idx
        vals.append(mx)
        idxs.append(idx)
        sels.append(sel)
        logit = jnp.where(sel, -jnp.inf, logit)
    ex = [jnp.exp(v - vals[0]) for v in vals]
    den = ex[0] + ex[1] + ex[2] + ex[3]
    gate_ref[...] = jnp.concatenate([e / den for e in ex], axis=0)
    e_ref[...] = jnp.concatenate(idxs, axis=0)
    onehot = (sels[0] | sels[1] | sels[2] | sels[3])
    cnt = onehot.astype(BF16)
    ur = lax.broadcasted_iota(jnp.int32, (tq, tq), 0)
    uc = lax.broadcasted_iota(jnp.int32, (tq, tq), 1)
    before = (ur < uc).astype(BF16)
    base = _dot(cnt, before)
    rank_ref[...] = jnp.concatenate(
        [jnp.sum(jnp.where(s_, base, 0.0), axis=0, keepdims=True) for s_ in sels],
        axis=0).astype(jnp.int32)
    cnt_ref[...] = jnp.sum(onehot.astype(F32), axis=1, keepdims=True)


def _cross(qc, kc, vc, x1, wco, g, wrt, br, batch, seq, mem_len, tq=256):
    nq = seq // tq
    n = batch * seq
    row = lambda w: pl.BlockSpec((tq, w), lambda b, i: (b * nq + i, 0))
    col = lambda r: pl.BlockSpec((r, tq), lambda b, i: (0, b * nq + i))
    full = lambda a: pl.BlockSpec(a.shape, lambda b, i: (0,) * a.ndim)
    memb = pl.BlockSpec((mem_len, D_MODEL), lambda b, i: (b, 0))
    return pl.pallas_call(
        _cross_kernel,
        grid=(batch, nq),
        in_specs=[row(D_MODEL), memb, memb, row(D_MODEL), full(wco), full(g), full(wrt),
                  full(br)],
        out_specs=[row(D_MODEL), row(D_MODEL), col(TOP_K), col(TOP_K), col(TOP_K),
                   pl.BlockSpec((None, N_EXPERTS, 1), lambda b, i: (b * nq + i, 0, 0))],
        out_shape=[jax.ShapeDtypeStruct((n, D_MODEL), F32),
                   jax.ShapeDtypeStruct((n, D_MODEL), BF16),
                   jax.ShapeDtypeStruct((TOP_K, n), jnp.int32),
                   jax.ShapeDtypeStruct((TOP_K, n), F32),
                   jax.ShapeDtypeStruct((TOP_K, n), jnp.int32),
                   jax.ShapeDtypeStruct((n // tq, N_EXPERTS, 1), F32)],
        compiler_params=_params(("arbitrary", "arbitrary")),
        name="cross_router",
    )(qc, kc, vc, x1, wco, g, wrt, br)


def _run_copy(hbm_ref, tab_ref, loc_ref, sem, tile, e, s, to_hbm):
    base = (tile * N_EXPERTS + e) * 3
    n = pl.multiple_of(tab_ref[base + 2], ROW_GRAN)
    loc = loc_ref.at[s, pl.ds(pl.multiple_of(tab_ref[base + 1], ROW_GRAN), n)]
    far = hbm_ref.at[pl.ds(pl.multiple_of(tab_ref[base], ROW_GRAN), n)]
    return pltpu.make_async_copy(loc, far, sem.at[s]) if to_hbm else \
        pltpu.make_async_copy(far, loc, sem.at[s])


def _dispatch_kernel(dst_ref, nct_ref, tail_ref, slot_ref, t_ref, xp_ref, xloc_ref, zero_ref,
                     sem, zsem, *, blk, mc):
    i = pl.program_id(0)
    s = i % 2
    lmax = xloc_ref.shape[1]

    @pl.when(i == 0)
    def _():
        zero_ref[...] = jnp.zeros_like(zero_ref)
        tails = [pl.multiple_of(tail_ref[e], blk) for e in range(N_EXPERTS)]
        for e in range(N_EXPERTS):
            pltpu.make_async_copy(zero_ref, xp_ref.at[pl.ds(tails[e], blk)], zsem).start()
        for e in range(N_EXPERTS):
            pltpu.make_async_copy(zero_ref, xp_ref.at[pl.ds(tails[e], blk)], zsem).wait()

    rows = lax.broadcasted_iota(jnp.int32, (lmax, t_ref.shape[0]), 0)
    hit = rows == slot_ref[0:1, :]
    for k in range(1, TOP_K):
        hit = hit | (rows == slot_ref[k:k + 1, :])
    xloc_ref[s] = _dot(jnp.where(hit, 1.0, 0.0).astype(BF16), t_ref[...])

    def issue(e, carry):
        @pl.when(dst_ref[(i * N_EXPERTS + e) * 3 + 2] > 0)
        def _():
            _run_copy(xp_ref, dst_ref, xloc_ref, sem, i, e, s, True).start()
        return carry

    lax.fori_loop(0, N_EXPERTS, issue, 0)

    def drain(tile, slot):
        rows_out = nct_ref[tile] * ROW_GRAN
        pltpu.make_async_copy(xloc_ref.at[slot, pl.ds(0, rows_out)],
                              xp_ref.at[pl.ds(0, rows_out)], sem.at[slot]).wait()

    @pl.when(i > 0)
    def _():
        drain(i - 1, 1 - s)

    @pl.when(i == pl.num_programs(0) - 1)
    def _():
        drain(i, s)


def _dispatch(dst_tab, nct, tail, slot, t, cap, blk, lmax, tm):
    n, d = t.shape
    mc = lmax // ROW_GRAN
    return pl.pallas_call(
        functools.partial(_dispatch_kernel, blk=blk, mc=mc),
        grid_spec=pltpu.PrefetchScalarGridSpec(
            num_scalar_prefetch=3,
            grid=(n // tm,),
            in_specs=[pl.BlockSpec((TOP_K, tm), lambda i, *_: (0, i)),
                      pl.BlockSpec((tm, d), lambda i, *_: (i, 0))],
            out_specs=pl.BlockSpec(memory_space=pl.ANY),
            scratch_shapes=[pltpu.VMEM((2, lmax, d), F32), pltpu.VMEM((blk, d), F32),
                            pltpu.SemaphoreType.DMA((2,)), pltpu.SemaphoreType.DMA(())]),
        out_shape=jax.ShapeDtypeStruct((cap, d), F32),
        compiler_params=_params(("arbitrary",)),
        name="moe_dispatch",
    )(dst_tab, nct, tail, slot, t)


def _expert_kernel(blk_e_ref, nused_ref, x_ref, wu_ref, bu_ref, wd_ref, bd_ref, y_ref,
                   wub_ref, wdb_ref):
    i = pl.program_id(0)

    @pl.when((i == 0) | (blk_e_ref[i] != blk_e_ref[jnp.maximum(i - 1, 0)]))
    def _():
        wub_ref[...] = wu_ref[...].astype(BF16)
        wdb_ref[...] = wd_ref[...].astype(BF16)

    @pl.when(i < nused_ref[0])
    def _():
        x = x_ref[...].astype(BF16)
        u = _dot(x, wub_ref[...]) + bu_ref[...]
        glu = jnp.minimum(u[:, :D_FF], SWIGLU_LIMIT)
        lin = jnp.clip(u[:, D_FF:], -SWIGLU_LIMIT, SWIGLU_LIMIT)
        act = glu * jax.nn.sigmoid(SWIGLU_ALPHA * glu) * (lin + 1.0)
        y_ref[...] = _dot(act.astype(BF16), wdb_ref[...]) + bd_ref[...]


def _experts(blk_e, n_used, x_pad, wu, bu, wd, bd, blk):
    cap = x_pad.shape[0]
    n_blk = cap // blk
    rowmap = lambda i, be, nu: (jnp.minimum(i, nu[0] - 1), 0)
    emap = lambda i, be, nu: (be[i], 0, 0)
    return pl.pallas_call(
        _expert_kernel,
        grid_spec=pltpu.PrefetchScalarGridSpec(
            num_scalar_prefetch=2,
            grid=(n_blk,),
            in_specs=[pl.BlockSpec((blk, D_MODEL), rowmap),
                      pl.BlockSpec((None, D_MODEL, 2 * D_FF), emap),
                      pl.BlockSpec((None, 1, 2 * D_FF), emap),
                      pl.BlockSpec((None, D_FF, D_MODEL), emap),
                      pl.BlockSpec((None, 1, D_MODEL), emap)],
            out_specs=pl.BlockSpec((blk, D_MODEL), rowmap),
            scratch_shapes=[pltpu.VMEM((D_MODEL, 2 * D_FF), BF16),
                            pltpu.VMEM((D_FF, D_MODEL), BF16)]),
        out_shape=jax.ShapeDtypeStruct((cap, D_MODEL), F32),
        compiler_params=_params(("arbitrary",), vmem=56 * 1024 * 1024),
        name="moe_experts",
    )(blk_e, n_used, x_pad, wu, bu, wd, bd)


def _combine_kernel(dst_ref, nct_ref, slot_ref, gate_ref, x2_ref, g_ref, y_ref, o_ref,
                    yloc_ref, sem, *, mc):
    i = pl.program_id(0)
    s = i % 2
    lmax = yloc_ref.shape[1]

    def issue(tile, slot):
        def body(e, carry):
            @pl.when(dst_ref[(tile * N_EXPERTS + e) * 3 + 2] > 0)
            def _():
                _run_copy(y_ref, dst_ref, yloc_ref, sem, tile, e, slot, False).start()
            return carry
        lax.fori_loop(0, N_EXPERTS, body, 0)

    @pl.when(i == 0)
    def _():
        yloc_ref[...] = jnp.zeros_like(yloc_ref)
        issue(0, 0)

    @pl.when(i + 1 < pl.num_programs(0))
    def _():
        issue(i + 1, 1 - s)

    rows_in = nct_ref[i] * ROW_GRAN
    pltpu.make_async_copy(y_ref.at[pl.ds(0, rows_in)], yloc_ref.at[s, pl.ds(0, rows_in)],
                          sem.at[s]).wait()

    cols = lax.broadcasted_iota(jnp.int32, (x2_ref.shape[0], lmax), 1)
    w = jnp.where(cols == slot_ref[:, 0:1], gate_ref[:, 0:1], 0.0)
    for k in range(1, TOP_K):
        w = w + jnp.where(cols == slot_ref[:, k:k + 1], gate_ref[:, k:k + 1], 0.0)
    acc = x2_ref[...] + _dot(w.astype(BF16), yloc_ref[s].astype(BF16))
    o_ref[...] = _rms(acc, g_ref[...])


def _combine(dst_tab, nct, slot_t, gate_t, x2, g, y_pad, lmax, tm):
    n = x2.shape[0]
    mc = lmax // ROW_GRAN
    row = pl.BlockSpec((tm, D_MODEL), lambda i, *_: (i, 0))
    col = pl.BlockSpec((tm, TOP_K), lambda i, *_: (i, 0))
    return pl.pallas_call(
        functools.partial(_combine_kernel, mc=mc),
        grid_spec=pltpu.PrefetchScalarGridSpec(
            num_scalar_prefetch=2,
            grid=(n // tm,),
            in_specs=[col, col, row, pl.BlockSpec(g.shape, lambda i, *_: (0, 0)),
                      pl.BlockSpec(memory_space=pl.ANY)],
            out_specs=row,
            scratch_shapes=[pltpu.VMEM((2, lmax, D_MODEL), F32),
                            pltpu.SemaphoreType.DMA((2,))]),
        out_shape=jax.ShapeDtypeStruct((n, D_MODEL), F32),
        compiler_params=_params(("arbitrary",)),
        name="moe_combine",
    )(dst_tab, nct, slot_t, gate_t, x2, g, y_pad)


def _rope_inv_freq():
    inv = ROPE_THETA ** (-np.arange(0, ROPE_DIM, 2, dtype=np.float32) / ROPE_DIM)
    lane = np.arange(LANES) % DA_DIM
    tab = np.where(lane < ROPE_DIM, inv.astype(np.float32)[lane % (ROPE_DIM // 2)], 0.0)
    return jnp.asarray(tab.astype(np.float32)[None, :])


def kernel(x, mem, positions, norm_mix_g, w_in, lambda_q1, lambda_k1, lambda_q2, lambda_k2, diff_norm_g, w_alpha2, b_alpha, gla_norm_g, w_out, norm_cross_g, norm_mem_g, w_cq, w_ck, w_cv, w_co, norm_ffn_g, w_router, b_router, w_up, b_up, w_down, b_down, norm_final_g):
    batch, seq, d = x.shape
    mem_len = mem.shape[1]
    n = batch * seq
    moe_blk = 512
    row = lambda a: a.reshape(1, -1)

    x2d = x.reshape(n, d)
    w = w_in[0]
    wqk, wv = w[:, :1024].astype(BF16), w[:, 1024:1536].astype(BF16)
    wg, wga = w[:, 1536:3072].astype(BF16), w[:, 3072:].astype(BF16)
    q, k, v, gq, gk, gv, gr, la = _inproj(
        x2d, positions.reshape(n, 1), _rope_inv_freq(), row(norm_mix_g[0]), wqk, wv, wg, wga,
        w_alpha2[0].astype(BF16), row(b_alpha[0]))

    o_da = _diffattn(q, k, v, row(lambda_q1[0]), row(lambda_k1[0]), row(lambda_q2[0]),
                     row(lambda_k2[0]), row(diff_norm_g[0]), batch, seq)
    o_gla = _gla(gq, gk, gv, gr, la, row(gla_norm_g[0]), batch, seq)

    x1, qc = _outproj(o_da, o_gla, x2d, w_out[0].astype(BF16), row(norm_cross_g[0]),
                      w_cq[0].astype(BF16))
    kc, vc = _memkv(mem.reshape(batch * mem_len, d), row(norm_mem_g[0]),
                    w_ck[0].astype(BF16), w_cv[0].astype(BF16))
    tile = 256
    x2, t, top_e, gate, lrank, counts = _cross(
        qc, kc, vc, x1, w_co[0].astype(BF16), row(norm_ffn_g[0]), w_router[0].T,
        b_router[0].reshape(-1, 1), batch, seq, mem_len, tq=tile)

    nt = n // tile
    lmax = -(-(TOP_K * tile + N_EXPERTS * (ROW_GRAN - 1)) // LANES) * LANES
    mc = lmax // ROW_GRAN
    cnt = counts[:, :, 0].astype(jnp.int32)
    plc = (cnt + ROW_GRAN - 1) // ROW_GRAN * ROW_GRAN
    lend = jnp.cumsum(plc, axis=1)
    lstart = lend - plc
    tile_off = jnp.cumsum(plc, axis=0) - plc
    etot = jnp.sum(plc, axis=0)
    eblk = (etot + moe_blk - 1) // moe_blk * moe_blk
    gend = jnp.cumsum(eblk)
    dst0 = (gend - eblk)[None, :] + tile_off
    cap = (nt * (TOP_K * tile + N_EXPERTS * (ROW_GRAN - 1)) + N_EXPERTS * (moe_blk - 1))
    cap = -(-cap // moe_blk) * moe_blk
    n_blk = cap // moe_blk
    n_used = (gend[-1] // moe_blk).astype(jnp.int32)
    blk_ids = jnp.minimum(jnp.arange(n_blk, dtype=jnp.int32), n_used - 1)
    blk_e = jnp.minimum(jnp.sum(gend[None, :] <= (blk_ids * moe_blk)[:, None], axis=1),
                        N_EXPERTS - 1).astype(jnp.int32)
    tail = jnp.where(etot > 0, gend, gend[-1]).astype(jnp.int32) - moe_blk
    eids = jnp.arange(N_EXPERTS, dtype=jnp.int32)[:, None, None]
    lstart_tok = jnp.repeat(lstart.T, tile, axis=1)[:, None, :]
    slot = (lrank + jnp.sum(jnp.where(top_e[None] == eids, lstart_tok, 0), axis=0)
            ).astype(jnp.int32)
    dst_tab = jnp.stack([dst0, lstart, plc], axis=-1).astype(jnp.int32).reshape(-1)
    nct = (lend[:, -1] // ROW_GRAN).astype(jnp.int32)

    x_pad = _dispatch(dst_tab, nct, tail, slot, t, cap, moe_blk, lmax, tile)
    y_pad = _experts(blk_e, n_used.reshape(1), x_pad, w_up[0], b_up[0][:, None, :], w_down[0],
                     b_down[0][:, None, :], moe_blk)
    out = _combine(dst_tab, nct, slot.T, gate.T, x2, row(norm_final_g), y_pad, lmax, tile)
    return out.reshape(batch, seq, d)
```

```python
import functools

import numpy as np
import jax
import jax.numpy as jnp
from jax import lax
from jax.experimental import pallas as pl
from jax.experimental.pallas import tpu as pltpu

EPS = 1e-6
D_MODEL = 1024
DA_HEADS = 4
DA_DIM = 64
GLA_HEADS = 4
GLA_DK = 64
GLA_DV = 128
GLA_RANK = 16
GLA_TAU = 16.0
GLA_CHUNK = 64
ROPE_THETA = 500000.0
ROPE_DIM = DA_DIM // 4
X_HEADS = 4
X_DIM = D_MODEL // X_HEADS
N_EXPERTS = 32
TOP_K = 4
D_FF = D_MODEL
SWIGLU_LIMIT = 7.0
SWIGLU_ALPHA = 1.702
LAMBDA_INIT = 0.8 - 0.6 * 1.0
Q_SCALE = DA_DIM ** -0.5 * float(np.log2(np.e))

LANES = 128
ROW_GRAN = 8
BF16 = jnp.bfloat16
F32 = jnp.float32
VMEM_LIMIT = 48 * 1024 * 1024


def _params(sem, vmem=VMEM_LIMIT):
    return pltpu.CompilerParams(dimension_semantics=sem, vmem_limit_bytes=vmem)


def _rms(xf, g):
    return xf * lax.rsqrt(jnp.mean(xf * xf, axis=-1, keepdims=True) + EPS) * g


def _dot(a, b):
    return jnp.dot(a, b, preferred_element_type=F32)


def _dot_nt(a, b):
    return lax.dot_general(a, b, (((1,), (1,)), ((), ())), preferred_element_type=F32)


def _split3(a):
    hi = a.astype(BF16)
    r1 = a - hi.astype(F32)
    mid = r1.astype(BF16)
    lo = (r1 - mid.astype(F32)).astype(BF16)
    return hi, mid, lo


def _inproj_kernel(x_ref, pos_ref, invf_ref, g_ref, wqk_ref, wv_ref, wg_ref, wga_ref,
                   wa2_ref, ba_ref,
                   q_ref, k_ref, v_ref, gq_ref, gk_ref, gv_ref, gr_ref, la_ref):
    h = _rms(x_ref[...], g_ref[...]).astype(BF16)
    ang = pos_ref[...].astype(F32) * invf_ref[...]
    lane = lax.broadcasted_iota(jnp.int32, ang.shape, 1) % DA_DIM
    half = ROPE_DIM // 2
    cosv = jnp.where(lane < ROPE_DIM, jnp.cos(ang), 1.0)
    sinv = jnp.sin(ang)
    s_lo = jnp.where(lane < half, -sinv, 0.0)
    s_hi = jnp.where((lane >= half) & (lane < ROPE_DIM), sinv, 0.0)

    qk = _dot(h, wqk_ref[...])
    n_grp = qk.shape[1] // LANES
    for j in range(n_grp):
        t = qk[:, j * LANES:(j + 1) * LANES]
        rot = (t * cosv + pltpu.roll(t, LANES - half, 1) * s_lo
               + pltpu.roll(t, half, 1) * s_hi)
        if j < n_grp // 2:
            q_ref[:, j * LANES:(j + 1) * LANES] = (rot * Q_SCALE).astype(BF16)
        else:
            jj = j - n_grp // 2
            k_ref[:, jj * LANES:(jj + 1) * LANES] = rot.astype(BF16)
    v_ref[...] = _dot(h, wv_ref[...]).astype(BF16)
    gg = _dot(h, wg_ref[...])
    gq_ref[...] = gg[:, 0:256]
    gk_ref[...] = gg[:, 256:512]
    gv_ref[...] = gg[:, 512:1024]
    gr_ref[...] = gg[:, 1024:1536]
    ga = _dot(h, wga_ref[...])
    z = _dot(ga.astype(BF16), wa2_ref[...]) + ba_ref[...]
    la_ref[...] = (jnp.minimum(z, 0.0) - jnp.log1p(jnp.exp(-jnp.abs(z)))) * (1.0 / GLA_TAU)


def _inproj(x2d, pos2d, invf, g, wqk, wv, wg, wga, wa2, ba, tm=256):
    n = x2d.shape[0]
    row = lambda w: pl.BlockSpec((tm, w), lambda i: (i, 0))
    full = lambda a: pl.BlockSpec(a.shape, lambda i: (0,) * a.ndim)
    outs = [(512, BF16), (512, BF16), (512, BF16), (256, F32), (256, F32), (512, F32),
            (512, F32), (256, F32)]
    return pl.pallas_call(
        _inproj_kernel,
        grid=(n // tm,),
        in_specs=[row(D_MODEL), row(1), full(invf), full(g), full(wqk), full(wv), full(wg),
                  full(wga), full(wa2), full(ba)],
        out_specs=[row(w) for w, _ in outs],
        out_shape=[jax.ShapeDtypeStruct((n, w), dt) for w, dt in outs],
        compiler_params=_params(("arbitrary",)),
        name="inproj",
    )(x2d, pos2d, invf, g, wqk, wv, wg, wga, wa2, ba)


def _diffattn_kernel(lq1_ref, lk1_ref, lq2_ref, lk2_ref, gn_ref, q_ref, k_ref, v_ref, o_ref,
                     *, tq):
    seq = q_ref.shape[0]
    lam = (jnp.exp(jnp.sum(lq1_ref[...] * lk1_ref[...], axis=-1, keepdims=True))
           - jnp.exp(jnp.sum(lq2_ref[...] * lk2_ref[...], axis=-1, keepdims=True))
           + LAMBDA_INIT)
    lane = lax.broadcasted_iota(jnp.int32, (tq, LANES), 1)
    r = lax.broadcasted_iota(jnp.int32, (2 * tq, tq), 0) % tq
    c = lax.broadcasted_iota(jnp.int32, (2 * tq, tq), 1)
    causal = c <= r
    for qi in range(seq // tq):
        q = q_ref[qi * tq:(qi + 1) * tq, :]
        zero = jnp.zeros_like(q)
        qs = jnp.concatenate([jnp.where(lane < DA_DIM, q, zero),
                              jnp.where(lane >= DA_DIM, q, zero)], axis=0)
        past = qi * tq
        s_diag = jnp.where(causal, _dot_nt(qs, k_ref[past:past + tq, :]), -jnp.inf)
        m = jnp.max(s_diag, axis=-1, keepdims=True)
        if qi > 0:
            s_past = _dot_nt(qs, k_ref[0:past, :])
            m = jnp.maximum(m, jnp.max(s_past, axis=-1, keepdims=True))
            p = jnp.exp2(jnp.concatenate([s_past, s_diag], axis=1) - m)
        else:
            p = jnp.exp2(s_diag - m)
        l = jnp.sum(p, axis=-1, keepdims=True)
        on = _dot(p.astype(BF16), v_ref[0:past + tq, :]) / l
        o = on[0:tq] - lam * on[tq:2 * tq]
        o_ref[qi * tq:(qi + 1) * tq, :] = (
            _rms(o, gn_ref[...]) * (1.0 - LAMBDA_INIT)).astype(o_ref.dtype)


def _diffattn(q, k, v, lq1, lk1, lq2, lk2, gn, batch, seq, tq=256):
    vec = lambda a: pl.BlockSpec(a.shape, lambda b, h: (0, 0))
    blk = pl.BlockSpec((seq, LANES), lambda b, h: (b, h))
    return pl.pallas_call(
        functools.partial(_diffattn_kernel, tq=tq),
        grid=(batch, DA_HEADS),
        in_specs=[vec(lq1), vec(lk1), vec(lq2), vec(lk2), vec(gn), blk, blk, blk],
        out_specs=blk,
        out_shape=jax.ShapeDtypeStruct(q.shape, BF16),
        compiler_params=_params(("arbitrary", "arbitrary")),
        name="diffattn",
    )(lq1, lk1, lq2, lk2, gn, q, k, v)


def _gla_kernel(gq_ref, gk_ref, gv_ref, gr_ref, la_ref, gn_ref, o_ref, st_ref, *, rows):
    c = GLA_CHUNK
    nch = rows // c
    kw = GLA_HEADS * GLA_DK
    pw = 2 * GLA_DK
    vw = 2 * GLA_DV

    @pl.when(pl.program_id(1) == 0)
    def _():
        st_ref[...] = jnp.zeros_like(st_ref)

    r_i = lax.broadcasted_iota(jnp.int32, (rows, rows), 0)
    c_i = lax.broadcasted_iota(jnp.int32, (rows, rows), 1)
    causal = (r_i // c == c_i // c) & (c_i <= r_i)
    tril = causal.astype(BF16)
    sr = lax.broadcasted_iota(jnp.int32, (pw, vw), 0) // GLA_DK
    sc = lax.broadcasted_iota(jnp.int32, (pw, vw), 1) // GLA_DV
    blockdiag = sr == sc
    head_of_lane = lax.broadcasted_iota(jnp.int32, (rows, kw), 1) // GLA_DK

    hi, mid, lo = _split3(la_ref[...])
    b = _dot(tril, hi) + _dot(tril, mid) + _dot(tril, lo)
    b3 = b.reshape(nch, c, kw)
    b_last = b3[:, c - 1:c, :]
    qe = gq_ref[...] * jnp.exp(b) * (GLA_DK ** -0.5)
    gk = gk_ref[...]
    kn = (gk * jnp.exp(-b)).astype(BF16)
    kd = gk.reshape(nch, c, kw) * jnp.exp(b_last - b3)
    v = gv_ref[...].astype(BF16)
    intra = []
    for h in range(GLA_HEADS):
        qh = jnp.where(head_of_lane == h, qe, 0.0).astype(BF16)
        attn = jnp.where(causal, _dot_nt(qh, kn), 0.0)
        intra.append(_dot(attn.astype(BF16), v[:, h * GLA_DV:(h + 1) * GLA_DV]))
    qeb = qe.astype(BF16)

    inter = []
    for p in range(GLA_HEADS // 2):
        ks = slice(p * pw, (p + 1) * pw)
        vs = slice(p * vw, (p + 1) * vw)
        st = st_ref[p]
        parts = []
        for ci in range(nch):
            rs = slice(ci * c, (ci + 1) * c)
            parts.append(_dot(qeb[rs, ks], st.astype(BF16)))
            upd = _dot(kd[ci][:, ks].T.astype(BF16), v[rs, vs])
            decay = jnp.exp(b_last[ci][:, ks]).T
            st = st * decay + jnp.where(blockdiag, upd, 0.0)
        st_ref[p] = st
        inter.append(jnp.concatenate(parts, axis=0))

    for h in range(GLA_HEADS):
        hs = slice(h * GLA_DV, (h + 1) * GLA_DV)
        o = inter[h // 2][:, (h % 2) * GLA_DV:(h % 2 + 1) * GLA_DV] + intra[h]
        gr = gr_ref[:, hs]
        y = _rms(o, gn_ref[...]) * (gr * jax.nn.sigmoid(gr))
        o_ref[:, hs] = y.astype(o_ref.dtype)


def _gla(gq, gk, gv, gr, la, gn, batch, seq, rows=256):
    nb = seq // rows
    row = lambda w: pl.BlockSpec((rows, w), lambda b, i: (b * nb + i, 0))
    return pl.pallas_call(
        functools.partial(_gla_kernel, rows=rows),
        grid=(batch, nb),
        in_specs=[row(256), row(256), row(512), row(512), row(256),
                  pl.BlockSpec(gn.shape, lambda b, i: (0, 0))],
        out_specs=row(512),
        out_shape=jax.ShapeDtypeStruct(gv.shape, BF16),
        scratch_shapes=[pltpu.VMEM((GLA_HEADS // 2, 2 * GLA_DK, 2 * GLA_DV), F32)],
        compiler_params=_params(("arbitrary", "arbitrary")),
        name="gla",
    )(gq, gk, gv, gr, la, gn)


def _outproj_kernel(oda_ref, ogla_ref, x_ref, wo_ref, g_ref, wcq_ref, x1_ref, qc_ref):
    half = oda_ref.shape[1]
    mix = _dot(oda_ref[...], wo_ref[0:half, :]) + _dot(ogla_ref[...], wo_ref[half:, :])
    x1 = x_ref[...] + mix
    x1_ref[...] = x1
    hq = _rms(x1, g_ref[...]).astype(BF16)
    qc_ref[...] = (_dot(hq, wcq_ref[...]) * (X_DIM ** -0.5)).astype(BF16)


def _outproj(oda, ogla, x2d, wo, g, wcq, tm=256):
    n = x2d.shape[0]
    row = lambda w: pl.BlockSpec((tm, w), lambda i: (i, 0))
    full = lambda a: pl.BlockSpec(a.shape, lambda i: (0,) * a.ndim)
    return pl.pallas_call(
        _outproj_kernel,
        grid=(n // tm,),
        in_specs=[row(512), row(512), row(D_MODEL), full(wo), full(g), full(wcq)],
        out_specs=[row(D_MODEL), row(D_MODEL)],
        out_shape=[jax.ShapeDtypeStruct((n, D_MODEL), F32),
                   jax.ShapeDtypeStruct((n, D_MODEL), BF16)],
        compiler_params=_params(("arbitrary",)),
        name="outproj",
    )(oda, ogla, x2d, wo, g, wcq)


def _memkv_kernel(m_ref, g_ref, wk_ref, wv_ref, k_ref, v_ref):
    hm = _rms(m_ref[...], g_ref[...]).astype(BF16)
    k_ref[...] = _dot(hm, wk_ref[...]).astype(BF16)
    v_ref[...] = _dot(hm, wv_ref[...]).astype(BF16)


def _memkv(mem2d, g, wk, wv, tm=256):
    n = mem2d.shape[0]
    row = pl.BlockSpec((tm, D_MODEL), lambda i: (i, 0))
    full = lambda a: pl.BlockSpec(a.shape, lambda i: (0,) * a.ndim)
    return pl.pallas_call(
        _memkv_kernel,
        grid=(n // tm,),
        in_specs=[row, full(g), full(wk), full(wv)],
        out_specs=[row, row],
        out_shape=[jax.ShapeDtypeStruct((n, D_MODEL), BF16)] * 2,
        compiler_params=_params(("arbitrary",)),
        name="memkv",
    )(mem2d, g, wk, wv)


def _cross_kernel(qc_ref, kc_ref, vc_ref, x1_ref, wco_ref, g_ref, wrt_ref, br_ref,
                  x2_ref, t_ref, e_ref, gate_ref, rank_ref, cnt_ref, *, tile):
    outs = []
    for h in range(X_HEADS):
        hs = slice(h * X_DIM, (h + 1) * X_DIM)
        s = _dot_nt(qc_ref[:, hs], kc_ref[:, hs])
        m = jnp.max(s, axis=-1, keepdims=True)
        p = jnp.exp(s - m)
        l = jnp.sum(p, axis=-1, keepdims=True)
        outs.append((_dot(p.astype(BF16), vc_ref[:, hs]) / l).astype(BF16))
    o = jnp.concatenate(outs, axis=1)
    x2 = x1_ref[...] + _dot(o, wco_ref[...])
    x2_ref[...] = x2
    t = _rms(x2, g_ref[...])
    t_ref[...] = t.astype(t_ref.dtype)

    w_hi, w_mid, _ = _split3(wrt_ref[...])
    ur = lax.broadcasted_iota(jnp.int32, (tile, tile), 0)
    uc = lax.broadcasted_iota(jnp.int32, (tile, tile), 1)
    before = (ur < uc).astype(BF16)
    for sub in range(qc_ref.shape[0] // tile):
        cs = slice(sub * tile, (sub + 1) * tile)
        t_hi, t_mid, _ = _split3(t[cs])
        logit = (_dot_nt(w_hi, t_hi) + _dot_nt(w_hi, t_mid) + _dot_nt(w_mid, t_hi)
                 ) + br_ref[...]
        iota_e = lax.broadcasted_iota(jnp.int32, logit.shape, 0)
        vals, idxs, sels = [], [], []
        for _ in range(TOP_K):
            mx = jnp.max(logit, axis=0, keepdims=True)
            idx = jnp.min(jnp.where(logit == mx, iota_e, N_EXPERTS), axis=0, keepdims=True)
            sel = iota_e == idx
            vals.append(mx)
            idxs.append(idx)
            sels.append(sel)
            logit = jnp.where(sel, -jnp.inf, logit)
        ex = [jnp.exp(v - vals[0]) for v in vals]
        den = ex[0] + ex[1] + ex[2] + ex[3]
        gate_ref[:, cs] = jnp.concatenate([e / den for e in ex], axis=0)
        e_ref[:, cs] = jnp.concatenate(idxs, axis=0)
        onehot = (sels[0] | sels[1] | sels[2] | sels[3])
        base = _dot(onehot.astype(BF16), before)
        rank_ref[:, cs] = jnp.concatenate(
            [jnp.sum(jnp.where(s_, base, 0.0), axis=0, keepdims=True) for s_ in sels],
            axis=0).astype(jnp.int32)
        cnt_ref[sub] = jnp.sum(onehot.astype(F32), axis=1, keepdims=True)


def _cross(qc, kc, vc, x1, wco, g, wrt, br, batch, seq, mem_len, tile, tq=512):
    nq = seq // tq
    n = batch * seq
    row = lambda w: pl.BlockSpec((tq, w), lambda b, i: (b * nq + i, 0))
    col = lambda r: pl.BlockSpec((r, tq), lambda b, i: (0, b * nq + i))
    full = lambda a: pl.BlockSpec(a.shape, lambda b, i: (0,) * a.ndim)
    memb = pl.BlockSpec((mem_len, D_MODEL), lambda b, i: (b, 0))
    return pl.pallas_call(
        functools.partial(_cross_kernel, tile=tile),
        grid=(batch, nq),
        in_specs=[row(D_MODEL), memb, memb, row(D_MODEL), full(wco), full(g), full(wrt),
                  full(br)],
        out_specs=[row(D_MODEL), row(D_MODEL), col(TOP_K), col(TOP_K), col(TOP_K),
                   pl.BlockSpec((tq // tile, N_EXPERTS, 1), lambda b, i: (b * nq + i, 0, 0))],
        out_shape=[jax.ShapeDtypeStruct((n, D_MODEL), F32),
                   jax.ShapeDtypeStruct((n, D_MODEL), BF16),
                   jax.ShapeDtypeStruct((TOP_K, n), jnp.int32),
                   jax.ShapeDtypeStruct((TOP_K, n), F32),
                   jax.ShapeDtypeStruct((TOP_K, n), jnp.int32),
                   jax.ShapeDtypeStruct((n // tile, N_EXPERTS, 1), F32)],
        compiler_params=_params(("arbitrary", "arbitrary")),
        name="cross_router",
    )(qc, kc, vc, x1, wco, g, wrt, br)


def _run_copy(hbm_ref, tab_ref, loc_ref, sem, tile, e, s, to_hbm):
    base = (tile * N_EXPERTS + e) * 3
    n = pl.multiple_of(tab_ref[base + 2], ROW_GRAN)
    loc = loc_ref.at[s, pl.ds(pl.multiple_of(tab_ref[base + 1], ROW_GRAN), n)]
    far = hbm_ref.at[pl.ds(pl.multiple_of(tab_ref[base], ROW_GRAN), n)]
    return pltpu.make_async_copy(loc, far, sem.at[s]) if to_hbm else \
        pltpu.make_async_copy(far, loc, sem.at[s])


def _dispatch_kernel(dst_ref, nct_ref, tail_ref, slot_ref, t_ref, xp_ref, xloc_ref, zero_ref,
                     sem, zsem, *, blk, mc):
    i = pl.program_id(0)
    s = i % 2
    lmax = xloc_ref.shape[1]

    @pl.when(i == 0)
    def _():
        zero_ref[...] = jnp.zeros_like(zero_ref)
        tails = [pl.multiple_of(tail_ref[e], blk) for e in range(N_EXPERTS)]
        for e in range(N_EXPERTS):
            pltpu.make_async_copy(zero_ref, xp_ref.at[pl.ds(tails[e], blk)], zsem).start()
        for e in range(N_EXPERTS):
            pltpu.make_async_copy(zero_ref, xp_ref.at[pl.ds(tails[e], blk)], zsem).wait()

    rows = lax.broadcasted_iota(jnp.int32, (lmax, t_ref.shape[0]), 0)
    hit = rows == slot_ref[0:1, :]
    for k in range(1, TOP_K):
        hit = hit | (rows == slot_ref[k:k + 1, :])
    xloc_ref[s] = _dot(jnp.where(hit, 1.0, 0.0).astype(BF16), t_ref[...])

    def issue(e, carry):
        @pl.when(dst_ref[(i * N_EXPERTS + e) * 3 + 2] > 0)
        def _():
            _run_copy(xp_ref, dst_ref, xloc_ref, sem, i, e, s, True).start()
        return carry

    lax.fori_loop(0, N_EXPERTS, issue, 0)

    def drain(tile, slot):
        rows_out = nct_ref[tile] * ROW_GRAN
        pltpu.make_async_copy(xloc_ref.at[slot, pl.ds(0, rows_out)],
                              xp_ref.at[pl.ds(0, rows_out)], sem.at[slot]).wait()

    @pl.when(i > 0)
    def _():
        drain(i - 1, 1 - s)

    @pl.when(i == pl.num_programs(0) - 1)
    def _():
        drain(i, s)


def _dispatch(dst_tab, nct, tail, slot, t, cap, blk, lmax, tm):
    n, d = t.shape
    mc = lmax // ROW_GRAN
    return pl.pallas_call(
        functools.partial(_dispatch_kernel, blk=blk, mc=mc),
        grid_spec=pltpu.PrefetchScalarGridSpec(
            num_scalar_prefetch=3,
            grid=(n // tm,),
            in_specs=[pl.BlockSpec((TOP_K, tm), lambda i, *_: (0, i)),
                      pl.BlockSpec((tm, d), lambda i, *_: (i, 0))],
            out_specs=pl.BlockSpec(memory_space=pl.ANY),
            scratch_shapes=[pltpu.VMEM((2, lmax, d), F32), pltpu.VMEM((blk, d), F32),
                            pltpu.SemaphoreType.DMA((2,)), pltpu.SemaphoreType.DMA(())]),
        out_shape=jax.ShapeDtypeStruct((cap, d), F32),
        compiler_params=_params(("arbitrary",)),
        name="moe_dispatch",
    )(dst_tab, nct, tail, slot, t)


def _expert_kernel(blk_e_ref, nused_ref, x_ref, wu_ref, bu_ref, wd_ref, bd_ref, y_ref,
                   wub_ref, wdb_ref):
    i = pl.program_id(0)

    @pl.when((i == 0) | (blk_e_ref[i] != blk_e_ref[jnp.maximum(i - 1, 0)]))
    def _():
        wub_ref[...] = wu_ref[...].astype(BF16)
        wdb_ref[...] = wd_ref[...].astype(BF16)

    @pl.when(i < nused_ref[0])
    def _():
        x = x_ref[...].astype(BF16)
        u = _dot(x, wub_ref[...]) + bu_ref[...]
        glu = jnp.minimum(u[:, :D_FF], SWIGLU_LIMIT)
        lin = jnp.clip(u[:, D_FF:], -SWIGLU_LIMIT, SWIGLU_LIMIT)
        act = glu * jax.nn.sigmoid(SWIGLU_ALPHA * glu) * (lin + 1.0)
        y_ref[...] = _dot(act.astype(BF16), wdb_ref[...]) + bd_ref[...]


def _experts(blk_e, n_used, x_pad, wu, bu, wd, bd, blk):
    cap = x_pad.shape[0]
    n_blk = cap // blk
    rowmap = lambda i, be, nu: (jnp.minimum(i, nu[0] - 1), 0)
    emap = lambda i, be, nu: (be[i], 0, 0)
    return pl.pallas_call(
        _expert_kernel,
        grid_spec=pltpu.PrefetchScalarGridSpec(
            num_scalar_prefetch=2,
            grid=(n_blk,),
            in_specs=[pl.BlockSpec((blk, D_MODEL), rowmap),
                      pl.BlockSpec((None, D_MODEL, 2 * D_FF), emap),
                      pl.BlockSpec((None, 1, 2 * D_FF), emap),
                      pl.BlockSpec((None, D_FF, D_MODEL), emap),
                      pl.BlockSpec((None, 1, D_MODEL), emap)],
            out_specs=pl.BlockSpec((blk, D_MODEL), rowmap),
            scratch_shapes=[pltpu.VMEM((D_MODEL, 2 * D_FF), BF16),
                            pltpu.VMEM((D_FF, D_MODEL), BF16)]),
        out_shape=jax.ShapeDtypeStruct((cap, D_MODEL), F32),
        compiler_params=_params(("arbitrary",), vmem=56 * 1024 * 1024),
        name="moe_experts",
    )(blk_e, n_used, x_pad, wu, bu, wd, bd)


def _combine_kernel(dst_ref, nct_ref, slot_ref, gate_ref, x2_ref, g_ref, y_ref, o_ref,
                    yloc_ref, sem, *, mc):
    i = pl.program_id(0)
    s = i % 2
    lmax = yloc_ref.shape[1]

    def issue(tile, slot):
        def body(e, carry):
            @pl.when(dst_ref[(tile * N_EXPERTS + e) * 3 + 2] > 0)
            def _():
                _run_copy(y_ref, dst_ref, yloc_ref, sem, tile, e, slot, False).start()
            return carry
        lax.fori_loop(0, N_EXPERTS, body, 0)

    @pl.when(i == 0)
    def _():
        yloc_ref[...] = jnp.zeros_like(yloc_ref)
        issue(0, 0)

    @pl.when(i + 1 < pl.num_programs(0))
    def _():
        issue(i + 1, 1 - s)

    rows_in = nct_ref[i] * ROW_GRAN
    pltpu.make_async_copy(y_ref.at[pl.ds(0, rows_in)], yloc_ref.at[s, pl.ds(0, rows_in)],
                          sem.at[s]).wait()

    cols = lax.broadcasted_iota(jnp.int32, (x2_ref.shape[0], lmax), 1)
    w = jnp.where(cols == slot_ref[:, 0:1], gate_ref[:, 0:1], 0.0)
    for k in range(1, TOP_K):
        w = w + jnp.where(cols == slot_ref[:, k:k + 1], gate_ref[:, k:k + 1], 0.0)
    acc = x2_ref[...] + _dot(w.astype(BF16), yloc_ref[s].astype(BF16))
    o_ref[...] = _rms(acc, g_ref[...])


def _combine(dst_tab, nct, slot_t, gate_t, x2, g, y_pad, lmax, tm):
    n = x2.shape[0]
    mc = lmax // ROW_GRAN
    row = pl.BlockSpec((tm, D_MODEL), lambda i, *_: (i, 0))
    col = pl.BlockSpec((tm, TOP_K), lambda i, *_: (i, 0))
    return pl.pallas_call(
        functools.partial(_combine_kernel, mc=mc),
        grid_spec=pltpu.PrefetchScalarGridSpec(
            num_scalar_prefetch=2,
            grid=(n // tm,),
            in_specs=[col, col, row, pl.BlockSpec(g.shape, lambda i, *_: (0, 0)),
                      pl.BlockSpec(memory_space=pl.ANY)],
            out_specs=row,
            scratch_shapes=[pltpu.VMEM((2, lmax, D_MODEL), F32),
                            pltpu.SemaphoreType.DMA((2,))]),
        out_shape=jax.ShapeDtypeStruct((n, D_MODEL), F32),
        compiler_params=_params(("arbitrary",)),
        name="moe_combine",
    )(dst_tab, nct, slot_t, gate_t, x2, g, y_pad)


def _rope_inv_freq():
    inv = ROPE_THETA ** (-np.arange(0, ROPE_DIM, 2, dtype=np.float32) / ROPE_DIM)
    lane = np.arange(LANES) % DA_DIM
    tab = np.where(lane < ROPE_DIM, inv.astype(np.float32)[lane % (ROPE_DIM // 2)], 0.0)
    return jnp.asarray(tab.astype(np.float32)[None, :])


def kernel(x, mem, positions, norm_mix_g, w_in, lambda_q1, lambda_k1, lambda_q2, lambda_k2, diff_norm_g, w_alpha2, b_alpha, gla_norm_g, w_out, norm_cross_g, norm_mem_g, w_cq, w_ck, w_cv, w_co, norm_ffn_g, w_router, b_router, w_up, b_up, w_down, b_down, norm_final_g):
    batch, seq, d = x.shape
    mem_len = mem.shape[1]
    n = batch * seq
    moe_blk = 512
    row = lambda a: a.reshape(1, -1)

    x2d = x.reshape(n, d)
    w = w_in[0]
    wqk, wv = w[:, :1024].astype(BF16), w[:, 1024:1536].astype(BF16)
    wg, wga = w[:, 1536:3072].astype(BF16), w[:, 3072:].astype(BF16)
    q, k, v, gq, gk, gv, gr, la = _inproj(
        x2d, positions.reshape(n, 1), _rope_inv_freq(), row(norm_mix_g[0]), wqk, wv, wg, wga,
        w_alpha2[0].astype(BF16), row(b_alpha[0]))

    o_da = _diffattn(q, k, v, row(lambda_q1[0]), row(lambda_k1[0]), row(lambda_q2[0]),
                     row(lambda_k2[0]), row(diff_norm_g[0]), batch, seq)
    o_gla = _gla(gq, gk, gv, gr, la, row(gla_norm_g[0]), batch, seq)

    x1, qc = _outproj(o_da, o_gla, x2d, w_out[0].astype(BF16), row(norm_cross_g[0]),
                      w_cq[0].astype(BF16))
    kc, vc = _memkv(mem.reshape(batch * mem_len, d), row(norm_mem_g[0]),
                    w_ck[0].astype(BF16), w_cv[0].astype(BF16))
    tile = 256
    x2, t, top_e, gate, lrank, counts = _cross(
        qc, kc, vc, x1, w_co[0].astype(BF16), row(norm_ffn_g[0]), w_router[0].T,
        b_router[0].reshape(-1, 1), batch, seq, mem_len, tile)

    nt = n // tile
    lmax = -(-(TOP_K * tile + N_EXPERTS * (ROW_GRAN - 1)) // LANES) * LANES
    mc = lmax // ROW_GRAN
    cnt = counts[:, :, 0].astype(jnp.int32)
    plc = (cnt + ROW_GRAN - 1) // ROW_GRAN * ROW_GRAN
    lend = jnp.cumsum(plc, axis=1)
    lstart = lend - plc
    tile_off = jnp.cumsum(plc, axis=0) - plc
    etot = jnp.sum(plc, axis=0)
    eblk = (etot + moe_blk - 1) // moe_blk * moe_blk
    gend = jnp.cumsum(eblk)
    dst0 = (gend - eblk)[None, :] + tile_off
    cap = (nt * (TOP_K * tile + N_EXPERTS * (ROW_GRAN - 1)) + N_EXPERTS * (moe_blk - 1))
    cap = -(-cap // moe_blk) * moe_blk
    n_blk = cap // moe_blk
    n_used = (gend[-1] // moe_blk).astype(jnp.int32)
    blk_ids = jnp.minimum(jnp.arange(n_blk, dtype=jnp.int32), n_used - 1)
    blk_e = jnp.minimum(jnp.sum(gend[None, :] <= (blk_ids * moe_blk)[:, None], axis=1),
                        N_EXPERTS - 1).astype(jnp.int32)
    tail = jnp.where(etot > 0, gend, gend[-1]).astype(jnp.int32) - moe_blk
    eids = jnp.arange(N_EXPERTS, dtype=jnp.int32)[:, None, None]
    lstart_tok = jnp.repeat(lstart.T, tile, axis=1)[:, None, :]
    slot = (lrank + jnp.sum(jnp.where(top_e[None] == eids, lstart_tok, 0), axis=0)
            ).astype(jnp.int32)
    dst_tab = jnp.stack([dst0, lstart, plc], axis=-1).astype(jnp.int32).reshape(-1)
    nct = (lend[:, -1] // ROW_GRAN).astype(jnp.int32)

    x_pad = _dispatch(dst_tab, nct, tail, slot, t, cap, moe_blk, lmax, tile)
    y_pad = _experts(blk_e, n_used.reshape(1), x_pad, w_up[0], b_up[0][:, None, :], w_down[0],
                     b_down[0][:, None, :], moe_blk)
    out = _combine(dst_tab, nct, slot.T, gate.T, x2, row(norm_final_g), y_pad, lmax, tile)
    return out.reshape(batch, seq, d)
```

```python
import functools

import numpy as np
import jax
import jax.numpy as jnp
from jax import lax
from jax.experimental import pallas as pl
from jax.experimental.pallas import tpu as pltpu

EPS = 1e-6
D_MODEL = 1024
DA_HEADS = 4
DA_DIM = 64
GLA_HEADS = 4
GLA_DK = 64
GLA_DV = 128
GLA_RANK = 16
GLA_TAU = 16.0
GLA_CHUNK = 64
ROPE_THETA = 500000.0
ROPE_DIM = DA_DIM // 4
X_HEADS = 4
X_DIM = D_MODEL // X_HEADS
N_EXPERTS = 32
TOP_K = 4
D_FF = D_MODEL
SWIGLU_LIMIT = 7.0
SWIGLU_ALPHA = 1.702
LAMBDA_INIT = 0.8 - 0.6 * 1.0
Q_SCALE = DA_DIM ** -0.5 * float(np.log2(np.e))

LANES = 128
ROW_GRAN = 16
BF16 = jnp.bfloat16
F32 = jnp.float32
VMEM_LIMIT = 48 * 1024 * 1024


def _params(sem, vmem=VMEM_LIMIT):
    return pltpu.CompilerParams(dimension_semantics=sem, vmem_limit_bytes=vmem)


def _rms(xf, g):
    return xf * lax.rsqrt(jnp.mean(xf * xf, axis=-1, keepdims=True) + EPS) * g


def _dot(a, b):
    return jnp.dot(a, b, preferred_element_type=F32)


def _dot_nt(a, b):
    return lax.dot_general(a, b, (((1,), (1,)), ((), ())), preferred_element_type=F32)


def _split3(a):
    hi = a.astype(BF16)
    r1 = a - hi.astype(F32)
    mid = r1.astype(BF16)
    lo = (r1 - mid.astype(F32)).astype(BF16)
    return hi, mid, lo


def _inproj_kernel(x_ref, pos_ref, invf_ref, g_ref, wqk_ref, wv_ref, wg_ref, wga_ref,
                   wa2_ref, ba_ref,
                   q_ref, k_ref, v_ref, gq_ref, gk_ref, gv_ref, gr_ref, la_ref):
    h = _rms(x_ref[...], g_ref[...]).astype(BF16)
    ang = pos_ref[...].astype(F32) * invf_ref[...]
    lane = lax.broadcasted_iota(jnp.int32, ang.shape, 1) % DA_DIM
    half = ROPE_DIM // 2
    cosv = jnp.where(lane < ROPE_DIM, jnp.cos(ang), 1.0)
    sinv = jnp.sin(ang)
    s_lo = jnp.where(lane < half, -sinv, 0.0)
    s_hi = jnp.where((lane >= half) & (lane < ROPE_DIM), sinv, 0.0)

    qk = _dot(h, wqk_ref[...])
    n_grp = qk.shape[1] // LANES
    for j in range(n_grp):
        t = qk[:, j * LANES:(j + 1) * LANES]
        rot = (t * cosv + pltpu.roll(t, LANES - half, 1) * s_lo
               + pltpu.roll(t, half, 1) * s_hi)
        if j < n_grp // 2:
            q_ref[:, j * LANES:(j + 1) * LANES] = (rot * Q_SCALE).astype(BF16)
        else:
            jj = j - n_grp // 2
            k_ref[:, jj * LANES:(jj + 1) * LANES] = rot.astype(BF16)
    v_ref[...] = _dot(h, wv_ref[...]).astype(BF16)
    gg = _dot(h, wg_ref[...])
    gq_ref[...] = gg[:, 0:256].astype(BF16)
    gk_ref[...] = gg[:, 256:512].astype(BF16)
    gv_ref[...] = gg[:, 512:1024].astype(BF16)
    gr_ref[...] = gg[:, 1024:1536].astype(BF16)
    ga = _dot(h, wga_ref[...])
    z = _dot(ga.astype(BF16), wa2_ref[...]) + ba_ref[...]
    la_ref[...] = (jnp.minimum(z, 0.0) - jnp.log1p(jnp.exp(-jnp.abs(z)))) * (1.0 / GLA_TAU)


def _inproj(x2d, pos2d, invf, g, wqk, wv, wg, wga, wa2, ba, tm=256):
    n = x2d.shape[0]
    row = lambda w: pl.BlockSpec((tm, w), lambda i: (i, 0))
    full = lambda a: pl.BlockSpec(a.shape, lambda i: (0,) * a.ndim)
    outs = [(512, BF16), (512, BF16), (512, BF16), (256, BF16), (256, BF16), (512, BF16),
            (512, BF16), (256, F32)]
    return pl.pallas_call(
        _inproj_kernel,
        grid=(n // tm,),
        in_specs=[row(D_MODEL), row(1), full(invf), full(g), full(wqk), full(wv), full(wg),
                  full(wga), full(wa2), full(ba)],
        out_specs=[row(w) for w, _ in outs],
        out_shape=[jax.ShapeDtypeStruct((n, w), dt) for w, dt in outs],
        compiler_params=_params(("arbitrary",)),
        name="inproj",
    )(x2d, pos2d, invf, g, wqk, wv, wg, wga, wa2, ba)


def _diffattn_kernel(lq1_ref, lk1_ref, lq2_ref, lk2_ref, gn_ref, q_ref, k_ref, v_ref, o_ref,
                     *, tq):
    seq = q_ref.shape[0]
    lam = (jnp.exp(jnp.sum(lq1_ref[...] * lk1_ref[...], axis=-1, keepdims=True))
           - jnp.exp(jnp.sum(lq2_ref[...] * lk2_ref[...], axis=-1, keepdims=True))
           + LAMBDA_INIT)
    lane = lax.broadcasted_iota(jnp.int32, (tq, LANES), 1)
    r = lax.broadcasted_iota(jnp.int32, (2 * tq, tq), 0) % tq
    c = lax.broadcasted_iota(jnp.int32, (2 * tq, tq), 1)
    causal = c <= r
    for qi in range(seq // tq):
        q = q_ref[qi * tq:(qi + 1) * tq, :]
        zero = jnp.zeros_like(q)
        qs = jnp.concatenate([jnp.where(lane < DA_DIM, q, zero),
                              jnp.where(lane >= DA_DIM, q, zero)], axis=0)
        past = qi * tq
        s_diag = jnp.where(causal, _dot_nt(qs, k_ref[past:past + tq, :]), -jnp.inf)
        m = jnp.max(s_diag, axis=-1, keepdims=True)
        if qi > 0:
            s_past = _dot_nt(qs, k_ref[0:past, :])
            m = jnp.maximum(m, jnp.max(s_past, axis=-1, keepdims=True))
            p = jnp.exp2(jnp.concatenate([s_past, s_diag], axis=1) - m)
        else:
            p = jnp.exp2(s_diag - m)
        l = jnp.sum(p, axis=-1, keepdims=True)
        on = _dot(p.astype(BF16), v_ref[0:past + tq, :]) / l
        o = on[0:tq] - lam * on[tq:2 * tq]
        o_ref[qi * tq:(qi + 1) * tq, :] = (
            _rms(o, gn_ref[...]) * (1.0 - LAMBDA_INIT)).astype(o_ref.dtype)


def _diffattn(q, k, v, lq1, lk1, lq2, lk2, gn, batch, seq, tq=256):
    vec = lambda a: pl.BlockSpec(a.shape, lambda b, h: (0, 0))
    blk = pl.BlockSpec((seq, LANES), lambda b, h: (b, h))
    return pl.pallas_call(
        functools.partial(_diffattn_kernel, tq=tq),
        grid=(batch, DA_HEADS),
        in_specs=[vec(lq1), vec(lk1), vec(lq2), vec(lk2), vec(gn), blk, blk, blk],
        out_specs=blk,
        out_shape=jax.ShapeDtypeStruct(q.shape, BF16),
        compiler_params=_params(("arbitrary", "arbitrary")),
        name="diffattn",
    )(lq1, lk1, lq2, lk2, gn, q, k, v)


def _gla_kernel(gq_ref, gk_ref, gv_ref, gr_ref, la_ref, gn_ref, o_ref, st_ref, *, rows):
    c = GLA_CHUNK
    nch = rows // c
    kw = GLA_HEADS * GLA_DK
    pw = 2 * GLA_DK
    vw = 2 * GLA_DV

    @pl.when(pl.program_id(1) == 0)
    def _():
        st_ref[...] = jnp.zeros_like(st_ref)

    r_i = lax.broadcasted_iota(jnp.int32, (rows, rows), 0)
    c_i = lax.broadcasted_iota(jnp.int32, (rows, rows), 1)
    causal = (r_i // c == c_i // c) & (c_i <= r_i)
    tril = causal.astype(BF16)
    sr = lax.broadcasted_iota(jnp.int32, (pw, vw), 0) // GLA_DK
    sc = lax.broadcasted_iota(jnp.int32, (pw, vw), 1) // GLA_DV
    blockdiag = sr == sc
    head_of_lane = lax.broadcasted_iota(jnp.int32, (rows, kw), 1) // GLA_DK

    hi, mid, lo = _split3(la_ref[...])
    b = _dot(tril, hi) + _dot(tril, mid) + _dot(tril, lo)
    b3 = b.reshape(nch, c, kw)
    b_last = b3[:, c - 1:c, :]
    qe = gq_ref[...].astype(F32) * jnp.exp(b) * (GLA_DK ** -0.5)
    gk = gk_ref[...].astype(F32)
    kn = (gk * jnp.exp(-b)).astype(BF16)
    kd = gk.reshape(nch, c, kw) * jnp.exp(b_last - b3)
    v = gv_ref[...]
    intra = []
    for h in range(GLA_HEADS):
        qh = jnp.where(head_of_lane == h, qe, 0.0).astype(BF16)
        attn = jnp.where(causal, _dot_nt(qh, kn), 0.0)
        intra.append(_dot(attn.astype(BF16), v[:, h * GLA_DV:(h + 1) * GLA_DV]))
    qeb = qe.astype(BF16)

    inter = []
    for p in range(GLA_HEADS // 2):
        ks = slice(p * pw, (p + 1) * pw)
        vs = slice(p * vw, (p + 1) * vw)
        st = st_ref[p]
        parts = []
        for ci in range(nch):
            rs = slice(ci * c, (ci + 1) * c)
            parts.append(_dot(qeb[rs, ks], st.astype(BF16)))
            upd = _dot(kd[ci][:, ks].T.astype(BF16), v[rs, vs])
            decay = jnp.exp(b_last[ci][:, ks]).T
            st = st * decay + jnp.where(blockdiag, upd, 0.0)
        st_ref[p] = st
        inter.append(jnp.concatenate(parts, axis=0))

    for h in range(GLA_HEADS):
        hs = slice(h * GLA_DV, (h + 1) * GLA_DV)
        o = inter[h // 2][:, (h % 2) * GLA_DV:(h % 2 + 1) * GLA_DV] + intra[h]
        gr = gr_ref[:, hs].astype(F32)
        y = _rms(o, gn_ref[...]) * (gr * jax.nn.sigmoid(gr))
        o_ref[:, hs] = y.astype(o_ref.dtype)


def _gla(gq, gk, gv, gr, la, gn, batch, seq, rows=256):
    nb = seq // rows
    row = lambda w: pl.BlockSpec((rows, w), lambda b, i: (b * nb + i, 0))
    return pl.pallas_call(
        functools.partial(_gla_kernel, rows=rows),
        grid=(batch, nb),
        in_specs=[row(256), row(256), row(512), row(512), row(256),
                  pl.BlockSpec(gn.shape, lambda b, i: (0, 0))],
        out_specs=row(512),
        out_shape=jax.ShapeDtypeStruct(gv.shape, BF16),
        scratch_shapes=[pltpu.VMEM((GLA_HEADS // 2, 2 * GLA_DK, 2 * GLA_DV), F32)],
        compiler_params=_params(("arbitrary", "arbitrary")),
        name="gla",
    )(gq, gk, gv, gr, la, gn)


def _memkv_kernel(m_ref, g_ref, wk_ref, wv_ref, k_ref, v_ref):
    hm = _rms(m_ref[...], g_ref[...]).astype(BF16)
    k_ref[...] = _dot(hm, wk_ref[...]).astype(BF16)
    v_ref[...] = _dot(hm, wv_ref[...]).astype(BF16)


def _memkv(mem2d, g, wk, wv, tm=256):
    n = mem2d.shape[0]
    row = pl.BlockSpec((tm, D_MODEL), lambda i: (i, 0))
    full = lambda a: pl.BlockSpec(a.shape, lambda i: (0,) * a.ndim)
    return pl.pallas_call(
        _memkv_kernel,
        grid=(n // tm,),
        in_specs=[row, full(g), full(wk), full(wv)],
        out_specs=[row, row],
        out_shape=[jax.ShapeDtypeStruct((n, D_MODEL), BF16)] * 2,
        compiler_params=_params(("arbitrary",)),
        name="memkv",
    )(mem2d, g, wk, wv)


def _cross_kernel(oda_ref, ogla_ref, x_ref, kc_ref, vc_ref, wo_ref, gq_ref, wcq_ref, wco_ref,
                  g_ref, wrt_ref, br_ref,
                  x2_ref, t_ref, e_ref, gate_ref, rank_ref, cnt_ref, *, tile):
    half = oda_ref.shape[1]
    x1 = x_ref[...] + (_dot(oda_ref[...], wo_ref[0:half, :])
                       + _dot(ogla_ref[...], wo_ref[half:, :]))
    hq = _rms(x1, gq_ref[...]).astype(BF16)
    qc = (_dot(hq, wcq_ref[...]) * (X_DIM ** -0.5)).astype(BF16)
    outs = []
    for h in range(X_HEADS):
        hs = slice(h * X_DIM, (h + 1) * X_DIM)
        s = _dot_nt(qc[:, hs], kc_ref[:, hs])
        m = jnp.max(s, axis=-1, keepdims=True)
        p = jnp.exp(s - m)
        l = jnp.sum(p, axis=-1, keepdims=True)
        outs.append((_dot(p.astype(BF16), vc_ref[:, hs]) / l).astype(BF16))
    o = jnp.concatenate(outs, axis=1)
    x2 = x1 + _dot(o, wco_ref[...])
    x2_ref[...] = x2
    t = _rms(x2, g_ref[...])
    t_ref[...] = t.astype(t_ref.dtype)

    w_hi, w_mid, _ = _split3(wrt_ref[...])
    ur = lax.broadcasted_iota(jnp.int32, (tile, tile), 0)
    uc = lax.broadcasted_iota(jnp.int32, (tile, tile), 1)
    before = (ur < uc).astype(BF16)
    for sub in range(x_ref.shape[0] // tile):
        cs = slice(sub * tile, (sub + 1) * tile)
        t_hi, t_mid, _ = _split3(t[cs])
        logit = (_dot_nt(w_hi, t_hi) + _dot_nt(w_hi, t_mid) + _dot_nt(w_mid, t_hi)
                 ) + br_ref[...]
        iota_e = lax.broadcasted_iota(jnp.int32, logit.shape, 0)
        vals, idxs, sels = [], [], []
        for _ in range(TOP_K):
            mx = jnp.max(logit, axis=0, keepdims=True)
            idx = jnp.min(jnp.where(logit == mx, iota_e, N_EXPERTS), axis=0, keepdims=True)
            sel = iota_e == idx
            vals.append(mx)
            idxs.append(idx)
            sels.append(sel)
            logit = jnp.where(sel, -jnp.inf, logit)
        ex = [jnp.exp(v - vals[0]) for v in vals]
        den = ex[0] + ex[1] + ex[2] + ex[3]
        gate_ref[:, cs] = jnp.concatenate([e / den for e in ex], axis=0)
        e_ref[:, cs] = jnp.concatenate(idxs, axis=0)
        onehot = (sels[0] | sels[1] | sels[2] | sels[3])
        base = _dot(onehot.astype(BF16), before)
        rank_ref[:, cs] = jnp.concatenate(
            [jnp.sum(jnp.where(s_, base, 0.0), axis=0, keepdims=True) for s_ in sels],
            axis=0).astype(jnp.int32)
        cnt_ref[sub] = jnp.sum(onehot.astype(F32), axis=1, keepdims=True)


def _cross(oda, ogla, x2d, kc, vc, wo, gq, wcq, wco, g, wrt, br, batch, seq, mem_len, tile, tq):
    nq = seq // tq
    n = batch * seq
    row = lambda w: pl.BlockSpec((tq, w), lambda b, i: (b * nq + i, 0))
    col = lambda r: pl.BlockSpec((r, tq), lambda b, i: (0, b * nq + i))
    full = lambda a: pl.BlockSpec(a.shape, lambda b, i: (0,) * a.ndim)
    memb = pl.BlockSpec((mem_len, D_MODEL), lambda b, i: (b, 0))
    return pl.pallas_call(
        functools.partial(_cross_kernel, tile=tile),
        grid=(batch, nq),
        in_specs=[row(oda.shape[1]), row(ogla.shape[1]), row(D_MODEL), memb, memb, full(wo),
                  full(gq), full(wcq), full(wco), full(g), full(wrt), full(br)],
        out_specs=[row(D_MODEL), row(D_MODEL), col(TOP_K), col(TOP_K), col(TOP_K),
                   pl.BlockSpec((tq // tile, N_EXPERTS, 1), lambda b, i: (b * nq + i, 0, 0))],
        out_shape=[jax.ShapeDtypeStruct((n, D_MODEL), F32),
                   jax.ShapeDtypeStruct((n, D_MODEL), BF16),
                   jax.ShapeDtypeStruct((TOP_K, n), jnp.int32),
                   jax.ShapeDtypeStruct((TOP_K, n), F32),
                   jax.ShapeDtypeStruct((TOP_K, n), jnp.int32),
                   jax.ShapeDtypeStruct((n // tile, N_EXPERTS, 1), F32)],
        compiler_params=_params(("arbitrary", "arbitrary"), vmem=56 * 1024 * 1024),
        name="mix_cross_router",
    )(oda, ogla, x2d, kc, vc, wo, gq, wcq, wco, g, wrt, br)


def _run_copy(hbm_ref, tab_ref, loc_ref, sem, tile, e, s, to_hbm):
    base = (tile * N_EXPERTS + e) * 3
    n = pl.multiple_of(tab_ref[base + 2], ROW_GRAN)
    loc = loc_ref.at[s, pl.ds(pl.multiple_of(tab_ref[base + 1], ROW_GRAN), n)]
    far = hbm_ref.at[pl.ds(pl.multiple_of(tab_ref[base], ROW_GRAN), n)]
    return pltpu.make_async_copy(loc, far, sem.at[s]) if to_hbm else \
        pltpu.make_async_copy(far, loc, sem.at[s])


def _dispatch_kernel(dst_ref, nct_ref, tail_ref, slot_ref, t_ref, xp_ref, xloc_ref, zero_ref,
                     sem, zsem, *, blk, mc):
    i = pl.program_id(0)
    s = i % 2
    lmax = xloc_ref.shape[1]

    @pl.when(i == 0)
    def _():
        zero_ref[...] = jnp.zeros_like(zero_ref)
        tails = [pl.multiple_of(tail_ref[e], blk) for e in range(N_EXPERTS)]
        for e in range(N_EXPERTS):
            pltpu.make_async_copy(zero_ref, xp_ref.at[pl.ds(tails[e], blk)], zsem).start()
        for e in range(N_EXPERTS):
            pltpu.make_async_copy(zero_ref, xp_ref.at[pl.ds(tails[e], blk)], zsem).wait()

    rows = lax.broadcasted_iota(jnp.int32, (lmax, t_ref.shape[0]), 0)
    hit = rows == slot_ref[0:1, :]
    for k in range(1, TOP_K):
        hit = hit | (rows == slot_ref[k:k + 1, :])
    xloc_ref[s] = _dot(jnp.where(hit, 1.0, 0.0).astype(BF16), t_ref[...]).astype(BF16)

    def issue(e, carry):
        @pl.when(dst_ref[(i * N_EXPERTS + e) * 3 + 2] > 0)
        def _():
            _run_copy(xp_ref, dst_ref, xloc_ref, sem, i, e, s, True).start()
        return carry

    lax.fori_loop(0, N_EXPERTS, issue, 0)

    def drain(tile, slot):
        rows_out = nct_ref[tile] * ROW_GRAN
        pltpu.make_async_copy(xloc_ref.at[slot, pl.ds(0, rows_out)],
                              xp_ref.at[pl.ds(0, rows_out)], sem.at[slot]).wait()

    @pl.when(i > 0)
    def _():
        drain(i - 1, 1 - s)

    @pl.when(i == pl.num_programs(0) - 1)
    def _():
        drain(i, s)


def _dispatch(dst_tab, nct, tail, slot, t, cap, blk, lmax, tm):
    n, d = t.shape
    mc = lmax // ROW_GRAN
    return pl.pallas_call(
        functools.partial(_dispatch_kernel, blk=blk, mc=mc),
        grid_spec=pltpu.PrefetchScalarGridSpec(
            num_scalar_prefetch=3,
            grid=(n // tm,),
            in_specs=[pl.BlockSpec((TOP_K, tm), lambda i, *_: (0, i)),
                      pl.BlockSpec((tm, d), lambda i, *_: (i, 0))],
            out_specs=pl.BlockSpec(memory_space=pl.ANY),
            scratch_shapes=[pltpu.VMEM((2, lmax, d), BF16), pltpu.VMEM((blk, d), BF16),
                            pltpu.SemaphoreType.DMA((2,)), pltpu.SemaphoreType.DMA(())]),
        out_shape=jax.ShapeDtypeStruct((cap, d), BF16),
        compiler_params=_params(("arbitrary",)),
        name="moe_dispatch",
    )(dst_tab, nct, tail, slot, t)


def _expert_kernel(blk_e_ref, nused_ref, x_ref, wu_ref, bu_ref, wd_ref, bd_ref, y_ref,
                   wub_ref, wdb_ref):
    i = pl.program_id(0)

    @pl.when((i == 0) | (blk_e_ref[i] != blk_e_ref[jnp.maximum(i - 1, 0)]))
    def _():
        wub_ref[...] = wu_ref[...].astype(BF16)
        wdb_ref[...] = wd_ref[...].astype(BF16)

    @pl.when(i < nused_ref[0])
    def _():
        u = _dot(x_ref[...], wub_ref[...]) + bu_ref[...]
        glu = jnp.minimum(u[:, :D_FF], SWIGLU_LIMIT)
        lin = jnp.clip(u[:, D_FF:], -SWIGLU_LIMIT, SWIGLU_LIMIT)
        act = glu * jax.nn.sigmoid(SWIGLU_ALPHA * glu) * (lin + 1.0)
        y_ref[...] = (_dot(act.astype(BF16), wdb_ref[...]) + bd_ref[...]).astype(y_ref.dtype)


def _experts(blk_e, n_used, x_pad, wu, bu, wd, bd, blk):
    cap = x_pad.shape[0]
    n_blk = cap // blk
    rowmap = lambda i, be, nu: (jnp.minimum(i, nu[0] - 1), 0)
    emap = lambda i, be, nu: (be[i], 0, 0)
    return pl.pallas_call(
        _expert_kernel,
        grid_spec=pltpu.PrefetchScalarGridSpec(
            num_scalar_prefetch=2,
            grid=(n_blk,),
            in_specs=[pl.BlockSpec((blk, D_MODEL), rowmap),
                      pl.BlockSpec((None, D_MODEL, 2 * D_FF), emap),
                      pl.BlockSpec((None, 1, 2 * D_FF), emap),
                      pl.BlockSpec((None, D_FF, D_MODEL), emap),
                      pl.BlockSpec((None, 1, D_MODEL), emap)],
            out_specs=pl.BlockSpec((blk, D_MODEL), rowmap),
            scratch_shapes=[pltpu.VMEM((D_MODEL, 2 * D_FF), BF16),
                            pltpu.VMEM((D_FF, D_MODEL), BF16)]),
        out_shape=jax.ShapeDtypeStruct((cap, D_MODEL), BF16),
        compiler_params=_params(("arbitrary",), vmem=56 * 1024 * 1024),
        name="moe_experts",
    )(blk_e, n_used, x_pad, wu, bu, wd, bd)


def _combine_kernel(dst_ref, nct_ref, slot_ref, gate_ref, x2_ref, g_ref, y_ref, o_ref,
                    yloc_ref, sem, *, mc):
    i = pl.program_id(0)
    s = i % 2
    lmax = yloc_ref.shape[1]

    def issue(tile, slot):
        def body(e, carry):
            @pl.when(dst_ref[(tile * N_EXPERTS + e) * 3 + 2] > 0)
            def _():
                _run_copy(y_ref, dst_ref, yloc_ref, sem, tile, e, slot, False).start()
            return carry
        lax.fori_loop(0, N_EXPERTS, body, 0)

    @pl.when(i == 0)
    def _():
        yloc_ref[...] = jnp.zeros_like(yloc_ref)
        issue(0, 0)

    @pl.when(i + 1 < pl.num_programs(0))
    def _():
        issue(i + 1, 1 - s)

    rows_in = nct_ref[i] * ROW_GRAN
    pltpu.make_async_copy(y_ref.at[pl.ds(0, rows_in)], yloc_ref.at[s, pl.ds(0, rows_in)],
                          sem.at[s]).wait()

    cols = lax.broadcasted_iota(jnp.int32, (x2_ref.shape[0], lmax), 1)
    w = jnp.where(cols == slot_ref[:, 0:1], gate_ref[:, 0:1], 0.0)
    for k in range(1, TOP_K):
        w = w + jnp.where(cols == slot_ref[:, k:k + 1], gate_ref[:, k:k + 1], 0.0)
    acc = x2_ref[...] + _dot(w.astype(BF16), yloc_ref[s])
    o_ref[...] = _rms(acc, g_ref[...])


def _combine(dst_tab, nct, slot_t, gate_t, x2, g, y_pad, lmax, tm):
    n = x2.shape[0]
    mc = lmax // ROW_GRAN
    row = pl.BlockSpec((tm, D_MODEL), lambda i, *_: (i, 0))
    col = pl.BlockSpec((tm, TOP_K), lambda i, *_: (i, 0))
    return pl.pallas_call(
        functools.partial(_combine_kernel, mc=mc),
        grid_spec=pltpu.PrefetchScalarGridSpec(
            num_scalar_prefetch=2,
            grid=(n // tm,),
            in_specs=[col, col, row, pl.BlockSpec(g.shape, lambda i, *_: (0, 0)),
                      pl.BlockSpec(memory_space=pl.ANY)],
            out_specs=row,
            scratch_shapes=[pltpu.VMEM((2, lmax, D_MODEL), BF16),
                            pltpu.SemaphoreType.DMA((2,))]),
        out_shape=jax.ShapeDtypeStruct((n, D_MODEL), F32),
        compiler_params=_params(("arbitrary",)),
        name="moe_combine",
    )(dst_tab, nct, slot_t, gate_t, x2, g, y_pad)


def _rope_inv_freq():
    inv = ROPE_THETA ** (-np.arange(0, ROPE_DIM, 2, dtype=np.float32) / ROPE_DIM)
    lane = np.arange(LANES) % DA_DIM
    tab = np.where(lane < ROPE_DIM, inv.astype(np.float32)[lane % (ROPE_DIM // 2)], 0.0)
    return jnp.asarray(tab.astype(np.float32)[None, :])


def kernel(x, mem, positions, norm_mix_g, w_in, lambda_q1, lambda_k1, lambda_q2, lambda_k2, diff_norm_g, w_alpha2, b_alpha, gla_norm_g, w_out, norm_cross_g, norm_mem_g, w_cq, w_ck, w_cv, w_co, norm_ffn_g, w_router, b_router, w_up, b_up, w_down, b_down, norm_final_g):
    batch, seq, d = x.shape
    mem_len = mem.shape[1]
    n = batch * seq
    moe_blk = 512
    row = lambda a: a.reshape(1, -1)

    x2d = x.reshape(n, d)
    w = w_in[0]
    wqk, wv = w[:, :1024].astype(BF16), w[:, 1024:1536].astype(BF16)
    wg, wga = w[:, 1536:3072].astype(BF16), w[:, 3072:].astype(BF16)
    q, k, v, gq, gk, gv, gr, la = _inproj(
        x2d, positions.reshape(n, 1), _rope_inv_freq(), row(norm_mix_g[0]), wqk, wv, wg, wga,
        w_alpha2[0].astype(BF16), row(b_alpha[0]))

    o_da = _diffattn(q, k, v, row(lambda_q1[0]), row(lambda_k1[0]), row(lambda_q2[0]),
                     row(lambda_k2[0]), row(diff_norm_g[0]), batch, seq)
    o_gla = _gla(gq, gk, gv, gr, la, row(gla_norm_g[0]), batch, seq)

    kc, vc = _memkv(mem.reshape(batch * mem_len, d), row(norm_mem_g[0]),
                    w_ck[0].astype(BF16), w_cv[0].astype(BF16))
    tile = 512
    x2, t, top_e, gate, lrank, counts = _cross(
        o_da, o_gla, x2d, kc, vc, w_out[0].astype(BF16), row(norm_cross_g[0]),
        w_cq[0].astype(BF16), w_co[0].astype(BF16), row(norm_ffn_g[0]), w_router[0].T,
        b_router[0].reshape(-1, 1), batch, seq, mem_len, tile, tq=min(seq, 1024))

    nt = n // tile
    lmax = -(-(TOP_K * tile + N_EXPERTS * (ROW_GRAN - 1)) // LANES) * LANES
    mc = lmax // ROW_GRAN
    cnt = counts[:, :, 0].astype(jnp.int32)
    plc = (cnt + ROW_GRAN - 1) // ROW_GRAN * ROW_GRAN
    lend = jnp.cumsum(plc, axis=1)
    lstart = lend - plc
    tile_off = jnp.cumsum(plc, axis=0) - plc
    etot = jnp.sum(plc, axis=0)
    eblk = (etot + moe_blk - 1) // moe_blk * moe_blk
    gend = jnp.cumsum(eblk)
    dst0 = (gend - eblk)[None, :] + tile_off
    cap = (nt * (TOP_K * tile + N_EXPERTS * (ROW_GRAN - 1)) + N_EXPERTS * (moe_blk - 1))
    cap = -(-cap // moe_blk) * moe_blk
    n_blk = cap // moe_blk
    n_used = (gend[-1] // moe_blk).astype(jnp.int32)
    blk_ids = jnp.minimum(jnp.arange(n_blk, dtype=jnp.int32), n_used - 1)
    blk_e = jnp.minimum(jnp.sum(gend[None, :] <= (blk_ids * moe_blk)[:, None], axis=1),
                        N_EXPERTS - 1).astype(jnp.int32)
    tail = jnp.where(etot > 0, gend, gend[-1]).astype(jnp.int32) - moe_blk
    eids = jnp.arange(N_EXPERTS, dtype=jnp.int32)[:, None, None]
    lstart_tok = jnp.repeat(lstart.T, tile, axis=1)[:, None, :]
    slot = (lrank + jnp.sum(jnp.where(top_e[None] == eids, lstart_tok, 0), axis=0)
            ).astype(jnp.int32)
    dst_tab = jnp.stack([dst0, lstart, plc], axis=-1).astype(jnp.int32).reshape(-1)
    nct = (lend[:, -1] // ROW_GRAN).astype(jnp.int32)

    x_pad = _dispatch(dst_tab, nct, tail, slot, t, cap, moe_blk, lmax, tile)
    y_pad = _experts(blk_e, n_used.reshape(1), x_pad, w_up[0], b_up[0][:, None, :], w_down[0],
                     b_down[0][:, None, :], moe_blk)
    out = _combine(dst_tab, nct, slot.T, gate.T, x2, row(norm_final_g), y_pad, lmax, tile)
    return out.reshape(batch, seq, d)
```

```python
import functools

import numpy as np
import jax
import jax.numpy as jnp
from jax import lax
from jax.experimental import pallas as pl
from jax.experimental.pallas import tpu as pltpu

EPS = 1e-6
D_MODEL = 1024
DA_HEADS = 4
DA_DIM = 64
GLA_HEADS = 4
GLA_DK = 64
GLA_DV = 128
GLA_RANK = 16
GLA_TAU = 16.0
GLA_CHUNK = 64
GLA_GROUP = 2
ROPE_THETA = 500000.0
ROPE_DIM = DA_DIM // 4
X_HEADS = 4
X_DIM = D_MODEL // X_HEADS
N_EXPERTS = 32
TOP_K = 4
D_FF = D_MODEL
SWIGLU_LIMIT = 7.0
SWIGLU_ALPHA = 1.702
LAMBDA_INIT = 0.8 - 0.6 * 1.0
Q_SCALE = DA_DIM ** -0.5 * float(np.log2(np.e))

LANES = 128
ROW_GRAN = 16
TAIL_ROWS = 256
BF16 = jnp.bfloat16
F32 = jnp.float32
VMEM_LIMIT = 48 * 1024 * 1024


def _params(sem, vmem=VMEM_LIMIT):
    return pltpu.CompilerParams(dimension_semantics=sem, vmem_limit_bytes=vmem)


def _rms(xf, g):
    return xf * lax.rsqrt(jnp.mean(xf * xf, axis=-1, keepdims=True) + EPS) * g


def _dot(a, b):
    return jnp.dot(a, b, preferred_element_type=F32)


def _dot_nt(a, b):
    return lax.dot_general(a, b, (((1,), (1,)), ((), ())), preferred_element_type=F32)


def _split3(a):
    hi = a.astype(BF16)
    r1 = a - hi.astype(F32)
    mid = r1.astype(BF16)
    lo = (r1 - mid.astype(F32)).astype(BF16)
    return hi, mid, lo


def _inproj_kernel(x_ref, pos_ref, invf_ref, g_ref, wqk_ref, wv_ref, wg_ref, wga_ref,
                   wa2_ref, ba_ref,
                   q_ref, k_ref, v_ref, gq_ref, gk_ref, gv_ref, gr_ref, la_ref):
    h = _rms(x_ref[...], g_ref[...]).astype(BF16)
    ang = pos_ref[...].astype(F32) * invf_ref[...]
    lane = lax.broadcasted_iota(jnp.int32, ang.shape, 1) % DA_DIM
    half = ROPE_DIM // 2
    cosv = jnp.where(lane < ROPE_DIM, jnp.cos(ang), 1.0)
    sinv = jnp.sin(ang)
    s_lo = jnp.where(lane < half, -sinv, 0.0)
    s_hi = jnp.where((lane >= half) & (lane < ROPE_DIM), sinv, 0.0)

    qk = _dot(h, wqk_ref[...])
    n_grp = qk.shape[1] // LANES
    for j in range(n_grp):
        t = qk[:, j * LANES:(j + 1) * LANES]
        rot = (t * cosv + pltpu.roll(t, LANES - half, 1) * s_lo
               + pltpu.roll(t, half, 1) * s_hi)
        if j < n_grp // 2:
            q_ref[:, j * LANES:(j + 1) * LANES] = (rot * Q_SCALE).astype(BF16)
        else:
            jj = j - n_grp // 2
            k_ref[:, jj * LANES:(jj + 1) * LANES] = rot.astype(BF16)
    v_ref[...] = _dot(h, wv_ref[...]).astype(BF16)
    gg = _dot(h, wg_ref[...])
    gq_ref[...] = gg[:, 0:256].astype(BF16)
    gk_ref[...] = gg[:, 256:512].astype(BF16)
    gv_ref[...] = gg[:, 512:1024].astype(BF16)
    gr_ref[...] = gg[:, 1024:1536].astype(BF16)
    ga = _dot(h, wga_ref[...])
    z = _dot(ga.astype(BF16), wa2_ref[...]) + ba_ref[...]
    la_ref[...] = (jnp.minimum(z, 0.0) - jnp.log1p(jnp.exp(-jnp.abs(z)))) * (1.0 / GLA_TAU)


def _inproj(x2d, pos2d, invf, g, wqk, wv, wg, wga, wa2, ba, tm=256):
    n = x2d.shape[0]
    row = lambda w: pl.BlockSpec((tm, w), lambda i: (i, 0))
    full = lambda a: pl.BlockSpec(a.shape, lambda i: (0,) * a.ndim)
    outs = [(512, BF16), (512, BF16), (512, BF16), (256, BF16), (256, BF16), (512, BF16),
            (512, BF16), (256, F32)]
    return pl.pallas_call(
        _inproj_kernel,
        grid=(n // tm,),
        in_specs=[row(D_MODEL), row(1), full(invf), full(g), full(wqk), full(wv), full(wg),
                  full(wga), full(wa2), full(ba)],
        out_specs=[row(w) for w, _ in outs],
        out_shape=[jax.ShapeDtypeStruct((n, w), dt) for w, dt in outs],
        compiler_params=_params(("arbitrary",)),
        name="inproj",
    )(x2d, pos2d, invf, g, wqk, wv, wg, wga, wa2, ba)


def _diffattn_kernel(lq1_ref, lk1_ref, lq2_ref, lk2_ref, gn_ref, q_ref, k_ref, v_ref, o_ref,
                     *, tq):
    seq = q_ref.shape[0]
    lam = (jnp.exp(jnp.sum(lq1_ref[...] * lk1_ref[...], axis=-1, keepdims=True))
           - jnp.exp(jnp.sum(lq2_ref[...] * lk2_ref[...], axis=-1, keepdims=True))
           + LAMBDA_INIT)
    lane = lax.broadcasted_iota(jnp.int32, (tq, LANES), 1)
    r = lax.broadcasted_iota(jnp.int32, (2 * tq, tq), 0) % tq
    c = lax.broadcasted_iota(jnp.int32, (2 * tq, tq), 1)
    causal = c <= r

    def scores(qi):
        q = q_ref[qi * tq:(qi + 1) * tq, :]
        zero = jnp.zeros_like(q)
        qs = jnp.concatenate([jnp.where(lane < DA_DIM, q, zero),
                              jnp.where(lane >= DA_DIM, q, zero)], axis=0)
        past = qi * tq
        s_diag = jnp.where(causal, _dot_nt(qs, k_ref[past:past + tq, :]), -jnp.inf)
        if qi == 0:
            return s_diag
        return jnp.concatenate([_dot_nt(qs, k_ref[0:past, :]), s_diag], axis=1)

    nq = seq // tq
    s_next = scores(0)
    for qi in range(nq):
        s = s_next
        if qi + 1 < nq:
            s_next = scores(qi + 1)
        p = jnp.exp2(s - jnp.max(s, axis=-1, keepdims=True))
        l = jnp.sum(p, axis=-1, keepdims=True)
        a = p[0:tq] - p[tq:2 * tq] * (lam * l[0:tq] / l[tq:2 * tq])
        o = _dot(a.astype(BF16), v_ref[0:(qi + 1) * tq, :]) / l[0:tq]
        o_ref[qi * tq:(qi + 1) * tq, :] = (
            _rms(o, gn_ref[...]) * (1.0 - LAMBDA_INIT)).astype(o_ref.dtype)


def _diffattn(q, k, v, lq1, lk1, lq2, lk2, gn, batch, seq, tq=256):
    vec = lambda a: pl.BlockSpec(a.shape, lambda b, h: (0, 0))
    blk = pl.BlockSpec((seq, LANES), lambda b, h: (b, h))
    return pl.pallas_call(
        functools.partial(_diffattn_kernel, tq=tq),
        grid=(batch, DA_HEADS),
        in_specs=[vec(lq1), vec(lk1), vec(lq2), vec(lk2), vec(gn), blk, blk, blk],
        out_specs=blk,
        out_shape=jax.ShapeDtypeStruct(q.shape, BF16),
        compiler_params=_params(("arbitrary", "arbitrary")),
        name="diffattn",
    )(lq1, lk1, lq2, lk2, gn, q, k, v)


def _gla_kernel(gq_ref, gk_ref, gv_ref, gr_ref, la_ref, gn_ref, o_ref, st_ref, *, rows):
    c = GLA_CHUNK
    nch = rows // c
    kw = GLA_HEADS * GLA_DK
    pw = 2 * GLA_DK
    vw = 2 * GLA_DV

    @pl.when(pl.program_id(1) == 0)
    def _():
        st_ref[...] = jnp.zeros_like(st_ref)

    r_i = lax.broadcasted_iota(jnp.int32, (rows, rows), 0)
    c_i = lax.broadcasted_iota(jnp.int32, (rows, rows), 1)
    causal = (r_i // c == c_i // c) & (c_i <= r_i)
    tril = causal.astype(BF16)
    sr = lax.broadcasted_iota(jnp.int32, (pw, vw), 0) // GLA_DK
    sc = lax.broadcasted_iota(jnp.int32, (pw, vw), 1) // GLA_DV
    blockdiag = sr == sc
    head_of_lane = lax.broadcasted_iota(jnp.int32, (rows, kw), 1) // GLA_DK

    for bb in range(GLA_GROUP):
        hi, mid, lo = _split3(la_ref[bb])
        b = _dot(tril, hi) + _dot(tril, mid) + _dot(tril, lo)
        b3 = b.reshape(nch, c, kw)
        b_last = b3[:, c - 1:c, :]
        qe = gq_ref[bb].astype(F32) * jnp.exp(b) * (GLA_DK ** -0.5)
        gk = gk_ref[bb].astype(F32)
        kn = (gk * jnp.exp(-b)).astype(BF16)
        kd = gk.reshape(nch, c, kw) * jnp.exp(b_last - b3)
        v = gv_ref[bb]
        intra = []
        for h in range(GLA_HEADS):
            qh = jnp.where(head_of_lane == h, qe, 0.0).astype(BF16)
            attn = jnp.where(causal, _dot_nt(qh, kn), 0.0)
            intra.append(_dot(attn.astype(BF16), v[:, h * GLA_DV:(h + 1) * GLA_DV]))
        qeb = qe.astype(BF16)

        inter = []
        for p in range(GLA_HEADS // 2):
            ks = slice(p * pw, (p + 1) * pw)
            vs = slice(p * vw, (p + 1) * vw)
            st = st_ref[bb, p]
            parts = []
            for ci in range(nch):
                rs = slice(ci * c, (ci + 1) * c)
                parts.append(_dot(qeb[rs, ks], st.astype(BF16)))
                upd = _dot(kd[ci][:, ks].T.astype(BF16), v[rs, vs])
                decay = jnp.exp(b_last[ci][:, ks]).T
                st = st * decay + jnp.where(blockdiag, upd, 0.0)
            st_ref[bb, p] = st
            inter.append(jnp.concatenate(parts, axis=0))

        for h in range(GLA_HEADS):
            hs = slice(h * GLA_DV, (h + 1) * GLA_DV)
            o = inter[h // 2][:, (h % 2) * GLA_DV:(h % 2 + 1) * GLA_DV] + intra[h]
            gr = gr_ref[bb, :, hs].astype(F32)
            y = _rms(o, gn_ref[...]) * (gr * jax.nn.sigmoid(gr))
            o_ref[bb, :, hs] = y.astype(o_ref.dtype)


def _gla(gq, gk, gv, gr, la, gn, batch, seq, rows=256):
    nb = seq // rows
    seqs = lambda a: a.reshape(batch, seq, a.shape[-1])
    blk = lambda w: pl.BlockSpec((GLA_GROUP, rows, w), lambda g, i: (g, i, 0))
    out = pl.pallas_call(
        functools.partial(_gla_kernel, rows=rows),
        grid=(batch // GLA_GROUP, nb),
        in_specs=[blk(256), blk(256), blk(512), blk(512), blk(256),
                  pl.BlockSpec(gn.shape, lambda g, i: (0, 0))],
        out_specs=blk(512),
        out_shape=jax.ShapeDtypeStruct((batch, seq, gv.shape[-1]), BF16),
        scratch_shapes=[pltpu.VMEM((GLA_GROUP, GLA_HEADS // 2, 2 * GLA_DK, 2 * GLA_DV), F32)],
        compiler_params=_params(("arbitrary", "arbitrary")),
        name="gla",
    )(seqs(gq), seqs(gk), seqs(gv), seqs(gr), seqs(la), gn)
    return out.reshape(gv.shape)


def _memkv_kernel(m_ref, g_ref, wk_ref, wv_ref, k_ref, v_ref):
    hm = _rms(m_ref[...], g_ref[...]).astype(BF16)
    k_ref[...] = _dot(hm, wk_ref[...]).astype(BF16)
    v_ref[...] = _dot(hm, wv_ref[...]).astype(BF16)


def _memkv(mem2d, g, wk, wv, tm=256):
    n = mem2d.shape[0]
    row = pl.BlockSpec((tm, D_MODEL), lambda i: (i, 0))
    full = lambda a: pl.BlockSpec(a.shape, lambda i: (0,) * a.ndim)
    return pl.pallas_call(
        _memkv_kernel,
        grid=(n // tm,),
        in_specs=[row, full(g), full(wk), full(wv)],
        out_specs=[row, row],
        out_shape=[jax.ShapeDtypeStruct((n, D_MODEL), BF16)] * 2,
        compiler_params=_params(("arbitrary",)),
        name="memkv",
    )(mem2d, g, wk, wv)


def _cross_kernel(oda_ref, ogla_ref, x_ref, kc_ref, vc_ref, wo_ref, gq_ref, wcq_ref, wco_ref,
                  g_ref, wrt_ref, br_ref,
                  x2_ref, t_ref, e_ref, gate_ref, rank_ref, cnt_ref, *, tile):
    half = oda_ref.shape[1]
    x1 = x_ref[...] + (_dot(oda_ref[...], wo_ref[0:half, :])
                       + _dot(ogla_ref[...], wo_ref[half:, :]))
    hq = _rms(x1, gq_ref[...]).astype(BF16)
    qc = (_dot(hq, wcq_ref[...]) * (X_DIM ** -0.5)).astype(BF16)
    outs = []
    for h in range(X_HEADS):
        hs = slice(h * X_DIM, (h + 1) * X_DIM)
        s = _dot_nt(qc[:, hs], kc_ref[:, hs])
        m = jnp.max(s, axis=-1, keepdims=True)
        p = jnp.exp(s - m)
        l = jnp.sum(p, axis=-1, keepdims=True)
        outs.append((_dot(p.astype(BF16), vc_ref[:, hs]) / l).astype(BF16))
    o = jnp.concatenate(outs, axis=1)
    x2 = x1 + _dot(o, wco_ref[...])
    x2_ref[...] = x2
    t = _rms(x2, g_ref[...])
    t_ref[...] = t.astype(t_ref.dtype)

    w_hi, w_mid, _ = _split3(wrt_ref[...])
    ur = lax.broadcasted_iota(jnp.int32, (tile, tile), 0)
    uc = lax.broadcasted_iota(jnp.int32, (tile, tile), 1)
    before = (ur < uc).astype(BF16)
    for sub in range(x_ref.shape[0] // tile):
        cs = slice(sub * tile, (sub + 1) * tile)
        t_hi, t_mid, _ = _split3(t[cs])
        logit = (_dot_nt(w_hi, t_hi) + _dot_nt(w_hi, t_mid) + _dot_nt(w_mid, t_hi)
                 ) + br_ref[...]
        iota_e = lax.broadcasted_iota(jnp.int32, logit.shape, 0)
        vals, idxs, sels = [], [], []
        for _ in range(TOP_K):
            mx = jnp.max(logit, axis=0, keepdims=True)
            idx = jnp.min(jnp.where(logit == mx, iota_e, N_EXPERTS), axis=0, keepdims=True)
            sel = iota_e == idx
            vals.append(mx)
            idxs.append(idx)
            sels.append(sel)
            logit = jnp.where(sel, -jnp.inf, logit)
        ex = [jnp.exp(v - vals[0]) for v in vals]
        den = ex[0] + ex[1] + ex[2] + ex[3]
        gate_ref[:, cs] = jnp.concatenate([e / den for e in ex], axis=0)
        e_ref[:, cs] = jnp.concatenate(idxs, axis=0)
        onehot = (sels[0] | sels[1] | sels[2] | sels[3])
        base = _dot(onehot.astype(BF16), before)
        rank_ref[:, cs] = jnp.concatenate(
            [jnp.sum(jnp.where(s_, base, 0.0), axis=0, keepdims=True) for s_ in sels],
            axis=0).astype(jnp.int32)
        cnt_ref[sub] = jnp.sum(onehot.astype(F32), axis=1, keepdims=True)


def _cross(oda, ogla, x2d, kc, vc, wo, gq, wcq, wco, g, wrt, br, batch, seq, mem_len, tile, tq):
    nq = seq // tq
    n = batch * seq
    row = lambda w: pl.BlockSpec((tq, w), lambda b, i: (b * nq + i, 0))
    col = lambda r: pl.BlockSpec((r, tq), lambda b, i: (0, b * nq + i))
    full = lambda a: pl.BlockSpec(a.shape, lambda b, i: (0,) * a.ndim)
    memb = pl.BlockSpec((mem_len, D_MODEL), lambda b, i: (b, 0))
    return pl.pallas_call(
        functools.partial(_cross_kernel, tile=tile),
        grid=(batch, nq),
        in_specs=[row(oda.shape[1]), row(ogla.shape[1]), row(D_MODEL), memb, memb, full(wo),
                  full(gq), full(wcq), full(wco), full(g), full(wrt), full(br)],
        out_specs=[row(D_MODEL), row(D_MODEL), col(TOP_K), col(TOP_K), col(TOP_K),
                   pl.BlockSpec((tq // tile, N_EXPERTS, 1), lambda b, i: (b * nq + i, 0, 0))],
        out_shape=[jax.ShapeDtypeStruct((n, D_MODEL), F32),
                   jax.ShapeDtypeStruct((n, D_MODEL), BF16),
                   jax.ShapeDtypeStruct((TOP_K, n), jnp.int32),
                   jax.ShapeDtypeStruct((TOP_K, n), F32),
                   jax.ShapeDtypeStruct((TOP_K, n), jnp.int32),
                   jax.ShapeDtypeStruct((n // tile, N_EXPERTS, 1), F32)],
        compiler_params=_params(("arbitrary", "arbitrary"), vmem=56 * 1024 * 1024),
        name="mix_cross_router",
    )(oda, ogla, x2d, kc, vc, wo, gq, wcq, wco, g, wrt, br)


def _run_copy(hbm_ref, tab_ref, loc_ref, sem, tile, e, s, to_hbm):
    base = (tile * N_EXPERTS + e) * 3
    n = pl.multiple_of(tab_ref[base + 2], ROW_GRAN)
    loc = loc_ref.at[s, pl.ds(pl.multiple_of(tab_ref[base + 1], ROW_GRAN), n)]
    far = hbm_ref.at[pl.ds(pl.multiple_of(tab_ref[base], ROW_GRAN), n)]
    return pltpu.make_async_copy(loc, far, sem.at[s]) if to_hbm else \
        pltpu.make_async_copy(far, loc, sem.at[s])


def _dispatch_kernel(dst_ref, nct_ref, tail_ref, slot_ref, t_ref, xp_ref, xloc_ref, zero_ref,
                     sem, zsem, *, blk, mc):
    i = pl.program_id(0)
    s = i % 2
    lmax = xloc_ref.shape[1]

    @pl.when(i == 0)
    def _():
        zero_ref[...] = jnp.zeros_like(zero_ref)
        tails = [pl.multiple_of(tail_ref[e], blk) for e in range(N_EXPERTS)]
        for e in range(N_EXPERTS):
            pltpu.make_async_copy(zero_ref, xp_ref.at[pl.ds(tails[e], blk)], zsem).start()
        for e in range(N_EXPERTS):
            pltpu.make_async_copy(zero_ref, xp_ref.at[pl.ds(tails[e], blk)], zsem).wait()

    def compact(r0, r1):
        rows = lax.broadcasted_iota(jnp.int32, (r1 - r0, t_ref.shape[0]), 0) + r0
        hit = rows == slot_ref[0:1, :]
        for k in range(1, TOP_K):
            hit = hit | (rows == slot_ref[k:k + 1, :])
        xloc_ref[s, r0:r1] = _dot(jnp.where(hit, 1.0, 0.0).astype(BF16),
                                  t_ref[...]).astype(BF16)

    base = TOP_K * t_ref.shape[0]
    compact(0, base)
    for r0 in range(base, lmax, TAIL_ROWS):
        @pl.when(nct_ref[i] * ROW_GRAN > r0)
        def _():
            compact(r0, min(r0 + TAIL_ROWS, lmax))

    def issue(e, carry):
        @pl.when(dst_ref[(i * N_EXPERTS + e) * 3 + 2] > 0)
        def _():
            _run_copy(xp_ref, dst_ref, xloc_ref, sem, i, e, s, True).start()
        return carry

    lax.fori_loop(0, N_EXPERTS, issue, 0)

    def drain(tile, slot):
        rows_out = nct_ref[tile] * ROW_GRAN
        pltpu.make_async_copy(xloc_ref.at[slot, pl.ds(0, rows_out)],
                              xp_ref.at[pl.ds(0, rows_out)], sem.at[slot]).wait()

    @pl.when(i > 0)
    def _():
        drain(i - 1, 1 - s)

    @pl.when(i == pl.num_programs(0) - 1)
    def _():
        drain(i, s)


def _dispatch(dst_tab, nct, tail, slot, t, cap, blk, lmax, tm):
    n, d = t.shape
    mc = lmax // ROW_GRAN
    return pl.pallas_call(
        functools.partial(_dispatch_kernel, blk=blk, mc=mc),
        grid_spec=pltpu.PrefetchScalarGridSpec(
            num_scalar_prefetch=3,
            grid=(n // tm,),
            in_specs=[pl.BlockSpec((TOP_K, tm), lambda i, *_: (0, i)),
                      pl.BlockSpec((tm, d), lambda i, *_: (i, 0))],
            out_specs=pl.BlockSpec(memory_space=pl.ANY),
            scratch_shapes=[pltpu.VMEM((2, lmax, d), BF16), pltpu.VMEM((blk, d), BF16),
                            pltpu.SemaphoreType.DMA((2,)), pltpu.SemaphoreType.DMA(())]),
        out_shape=jax.ShapeDtypeStruct((cap, d), BF16),
        compiler_params=_params(("arbitrary",)),
        name="moe_dispatch",
    )(dst_tab, nct, tail, slot, t)


def _expert_kernel(blk_e_ref, nused_ref, x_ref, wu_ref, bu_ref, wd_ref, bd_ref, y_ref,
                   wub_ref, wdb_ref):
    i = pl.program_id(0)

    @pl.when((i == 0) | (blk_e_ref[i] != blk_e_ref[jnp.maximum(i - 1, 0)]))
    def _():
        wub_ref[...] = wu_ref[...].astype(BF16)
        wdb_ref[...] = wd_ref[...].astype(BF16)

    @pl.when(i < nused_ref[0])
    def _():
        u = _dot(x_ref[...], wub_ref[...]) + bu_ref[...]
        glu = jnp.minimum(u[:, :D_FF], SWIGLU_LIMIT)
        lin = jnp.clip(u[:, D_FF:], -SWIGLU_LIMIT, SWIGLU_LIMIT)
        act = glu * jax.nn.sigmoid(SWIGLU_ALPHA * glu) * (lin + 1.0)
        y_ref[...] = (_dot(act.astype(BF16), wdb_ref[...]) + bd_ref[...]).astype(y_ref.dtype)


def _experts(blk_e, n_used, x_pad, wu, bu, wd, bd, blk):
    cap = x_pad.shape[0]
    n_blk = cap // blk
    rowmap = lambda i, be, nu: (jnp.minimum(i, nu[0] - 1), 0)
    emap = lambda i, be, nu: (be[i], 0, 0)
    return pl.pallas_call(
        _expert_kernel,
        grid_spec=pltpu.PrefetchScalarGridSpec(
            num_scalar_prefetch=2,
            grid=(n_blk,),
            in_specs=[pl.BlockSpec((blk, D_MODEL), rowmap),
                      pl.BlockSpec((None, D_MODEL, 2 * D_FF), emap),
                      pl.BlockSpec((None, 1, 2 * D_FF), emap),
                      pl.BlockSpec((None, D_FF, D_MODEL), emap),
                      pl.BlockSpec((None, 1, D_MODEL), emap)],
            out_specs=pl.BlockSpec((blk, D_MODEL), rowmap),
            scratch_shapes=[pltpu.VMEM((D_MODEL, 2 * D_FF), BF16),
                            pltpu.VMEM((D_FF, D_MODEL), BF16)]),
        out_shape=jax.ShapeDtypeStruct((cap, D_MODEL), BF16),
        compiler_params=_params(("arbitrary",), vmem=56 * 1024 * 1024),
        name="moe_experts",
    )(blk_e, n_used, x_pad, wu, bu, wd, bd)


def _combine_kernel(dst_ref, nct_ref, slot_ref, gate_ref, x2_ref, g_ref, y_ref, o_ref,
                    yloc_ref, sem, *, mc):
    i = pl.program_id(0)
    s = i % 2
    lmax = yloc_ref.shape[1]

    def issue(tile, slot):
        def body(e, carry):
            @pl.when(dst_ref[(tile * N_EXPERTS + e) * 3 + 2] > 0)
            def _():
                _run_copy(y_ref, dst_ref, yloc_ref, sem, tile, e, slot, False).start()
            return carry
        lax.fori_loop(0, N_EXPERTS, body, 0)

    @pl.when(i == 0)
    def _():
        yloc_ref[...] = jnp.zeros_like(yloc_ref)
        issue(0, 0)

    @pl.when(i + 1 < pl.num_programs(0))
    def _():
        issue(i + 1, 1 - s)

    rows_in = nct_ref[i] * ROW_GRAN
    pltpu.make_async_copy(y_ref.at[pl.ds(0, rows_in)], yloc_ref.at[s, pl.ds(0, rows_in)],
                          sem.at[s]).wait()

    cols = lax.broadcasted_iota(jnp.int32, (x2_ref.shape[0], lmax), 1)
    w = jnp.where(cols == slot_ref[:, 0:1], gate_ref[:, 0:1], 0.0)
    for k in range(1, TOP_K):
        w = w + jnp.where(cols == slot_ref[:, k:k + 1], gate_ref[:, k:k + 1], 0.0)
    acc = x2_ref[...] + _dot(w.astype(BF16), yloc_ref[s])
    o_ref[...] = _rms(acc, g_ref[...])


def _combine(dst_tab, nct, slot_t, gate_t, x2, g, y_pad, lmax, tm):
    n = x2.shape[0]
    mc = lmax // ROW_GRAN
    row = pl.BlockSpec((tm, D_MODEL), lambda i, *_: (i, 0))
    col = pl.BlockSpec((tm, TOP_K), lambda i, *_: (i, 0))
    return pl.pallas_call(
        functools.partial(_combine_kernel, mc=mc),
        grid_spec=pltpu.PrefetchScalarGridSpec(
            num_scalar_prefetch=2,
            grid=(n // tm,),
            in_specs=[col, col, row, pl.BlockSpec(g.shape, lambda i, *_: (0, 0)),
                      pl.BlockSpec(memory_space=pl.ANY)],
            out_specs=row,
            scratch_shapes=[pltpu.VMEM((2, lmax, D_MODEL), BF16),
                            pltpu.SemaphoreType.DMA((2,))]),
        out_shape=jax.ShapeDtypeStruct((n, D_MODEL), F32),
        compiler_params=_params(("arbitrary",)),
        name="moe_combine",
    )(dst_tab, nct, slot_t, gate_t, x2, g, y_pad)


def _rope_inv_freq():
    inv = ROPE_THETA ** (-np.arange(0, ROPE_DIM, 2, dtype=np.float32) / ROPE_DIM)
    lane = np.arange(LANES) % DA_DIM
    tab = np.where(lane < ROPE_DIM, inv.astype(np.float32)[lane % (ROPE_DIM // 2)], 0.0)
    return jnp.asarray(tab.astype(np.float32)[None, :])


def kernel(x, mem, positions, norm_mix_g, w_in, lambda_q1, lambda_k1, lambda_q2, lambda_k2, diff_norm_g, w_alpha2, b_alpha, gla_norm_g, w_out, norm_cross_g, norm_mem_g, w_cq, w_ck, w_cv, w_co, norm_ffn_g, w_router, b_router, w_up, b_up, w_down, b_down, norm_final_g):
    batch, seq, d = x.shape
    mem_len = mem.shape[1]
    n = batch * seq
    moe_blk = 512
    row = lambda a: a.reshape(1, -1)

    x2d = x.reshape(n, d)
    w = w_in[0]
    wqk, wv = w[:, :1024].astype(BF16), w[:, 1024:1536].astype(BF16)
    wg, wga = w[:, 1536:3072].astype(BF16), w[:, 3072:].astype(BF16)
    q, k, v, gq, gk, gv, gr, la = _inproj(
        x2d, positions.reshape(n, 1), _rope_inv_freq(), row(norm_mix_g[0]), wqk, wv, wg, wga,
        w_alpha2[0].astype(BF16), row(b_alpha[0]))

    o_da = _diffattn(q, k, v, row(lambda_q1[0]), row(lambda_k1[0]), row(lambda_q2[0]),
                     row(lambda_k2[0]), row(diff_norm_g[0]), batch, seq)
    o_gla = _gla(gq, gk, gv, gr, la, row(gla_norm_g[0]), batch, seq)

    kc, vc = _memkv(mem.reshape(batch * mem_len, d), row(norm_mem_g[0]),
                    w_ck[0].astype(BF16), w_cv[0].astype(BF16))
    tile = 512
    x2, t, top_e, gate, lrank, counts = _cross(
        o_da, o_gla, x2d, kc, vc, w_out[0].astype(BF16), row(norm_cross_g[0]),
        w_cq[0].astype(BF16), w_co[0].astype(BF16), row(norm_ffn_g[0]), w_router[0].T,
        b_router[0].reshape(-1, 1), batch, seq, mem_len, tile, tq=min(seq, 1024))

    nt = n // tile
    lmax = -(-(TOP_K * tile + N_EXPERTS * (ROW_GRAN - 1)) // LANES) * LANES
    mc = lmax // ROW_GRAN
    cnt = counts[:, :, 0].astype(jnp.int32)
    plc = (cnt + ROW_GRAN - 1) // ROW_GRAN * ROW_GRAN
    lend = jnp.cumsum(plc, axis=1)
    lstart = lend - plc
    tile_off = jnp.cumsum(plc, axis=0) - plc
    etot = jnp.sum(plc, axis=0)
    eblk = (etot + moe_blk - 1) // moe_blk * moe_blk
    gend = jnp.cumsum(eblk)
    dst0 = (gend - eblk)[None, :] + tile_off
    cap = (nt * (TOP_K * tile + N_EXPERTS * (ROW_GRAN - 1)) + N_EXPERTS * (moe_blk - 1))
    cap = -(-cap // moe_blk) * moe_blk
    n_blk = cap // moe_blk
    n_used = (gend[-1] // moe_blk).astype(jnp.int32)
    blk_ids = jnp.minimum(jnp.arange(n_blk, dtype=jnp.int32), n_used - 1)
    blk_e = jnp.minimum(jnp.sum(gend[None, :] <= (blk_ids * moe_blk)[:, None], axis=1),
                        N_EXPERTS - 1).astype(jnp.int32)
    tail = jnp.where(etot > 0, gend, gend[-1]).astype(jnp.int32) - moe_blk
    eids = jnp.arange(N_EXPERTS, dtype=jnp.int32)[:, None, None]
    lstart_tok = jnp.repeat(lstart.T, tile, axis=1)[:, None, :]
    slot = (lrank + jnp.sum(jnp.where(top_e[None] == eids, lstart_tok, 0), axis=0)
            ).astype(jnp.int32)
    dst_tab = jnp.stack([dst0, lstart, plc], axis=-1).astype(jnp.int32).reshape(-1)
    nct = (lend[:, -1] // ROW_GRAN).astype(jnp.int32)

    x_pad = _dispatch(dst_tab, nct, tail, slot, t, cap, moe_blk, lmax, tile)
    y_pad = _experts(blk_e, n_used.reshape(1), x_pad, w_up[0], b_up[0][:, None, :], w_down[0],
                     b_down[0][:, None, :], moe_blk)
    out = _combine(dst_tab, nct, slot.T, gate.T, x2, row(norm_final_g), y_pad, lmax, tile)
    return out.reshape(batch, seq, d)
```

```python
import functools
import math

import numpy as np
import jax
import jax.numpy as jnp
from jax import lax
from jax.experimental import pallas as pl
from jax.experimental.pallas import tpu as pltpu

EPS = 1e-6
D_MODEL = 1024
DA_HEADS = 4
DA_DIM = 64
GLA_HEADS = 4
GLA_DK = 64
GLA_DV = 128
GLA_RANK = 16
GLA_TAU = 16.0
GLA_CHUNK = 64
GLA_GROUP = 4
ROPE_THETA = 500000.0
ROPE_DIM = DA_DIM // 4
X_HEADS = 4
X_DIM = D_MODEL // X_HEADS
N_EXPERTS = 32
TOP_K = 4
D_FF = D_MODEL
SWIGLU_LIMIT = 7.0
SWIGLU_ALPHA = 1.702
LAMBDA_INIT = 0.8 - 0.6 * 1.0
Q_SCALE = DA_DIM ** -0.5 * float(np.log2(np.e))

LANES = 128
ROW_GRAN = 16
TAIL_ROWS = 256
BF16 = jnp.bfloat16
F32 = jnp.float32
VMEM_LIMIT = 48 * 1024 * 1024


def _params(sem, vmem=VMEM_LIMIT):
    return pltpu.CompilerParams(dimension_semantics=sem, vmem_limit_bytes=vmem)


def _rms(xf, g):
    return xf * lax.rsqrt(jnp.mean(xf * xf, axis=-1, keepdims=True) + EPS) * g


def _dot(a, b):
    return jnp.dot(a, b, preferred_element_type=F32)


def _dot_nt(a, b):
    return lax.dot_general(a, b, (((1,), (1,)), ((), ())), preferred_element_type=F32)


def _split3(a):
    hi = a.astype(BF16)
    r1 = a - hi.astype(F32)
    mid = r1.astype(BF16)
    lo = (r1 - mid.astype(F32)).astype(BF16)
    return hi, mid, lo


def _inproj_kernel(x_ref, pos_ref, invf_ref, g_ref, wqk_ref, wv_ref, wg_ref, wga_ref,
                   wa2_ref, ba_ref,
                   q_ref, k_ref, v_ref, gq_ref, gk_ref, gv_ref, gr_ref, la_ref, *, sub):
    groups = [slice(j * sub, (j + 1) * sub) for j in range(x_ref.shape[0] // sub)]
    half = ROPE_DIM // 2
    lane = lax.broadcasted_iota(jnp.int32, (sub, LANES), 1) % DA_DIM
    h = [_rms(x_ref[rs, :], g_ref[...]).astype(BF16) for rs in groups]
    tabs = []
    for rs in groups:
        ang = pos_ref[rs, :].astype(F32) * invf_ref[...]
        sinv = jnp.sin(ang)
        tabs.append((jnp.where(lane < ROPE_DIM, jnp.cos(ang), 1.0),
                     jnp.where(lane < half, -sinv, 0.0),
                     jnp.where((lane >= half) & (lane < ROPE_DIM), sinv, 0.0)))
    proj = [(_dot(a, wqk_ref[...]), _dot(a, wg_ref[...]), _dot(a, wv_ref[...]),
             _dot(a, wga_ref[...])) for a in h]
    for rs, (qk, gg, pv, ga), (cosv, s_lo, s_hi) in zip(groups, proj, tabs):
        n_grp = qk.shape[1] // LANES
        for j in range(n_grp):
            t = qk[:, j * LANES:(j + 1) * LANES]
            rot = (t * cosv + pltpu.roll(t, LANES - half, 1) * s_lo
                   + pltpu.roll(t, half, 1) * s_hi)
            if j < n_grp // 2:
                q_ref[rs, j * LANES:(j + 1) * LANES] = (rot * Q_SCALE).astype(BF16)
            else:
                jj = j - n_grp // 2
                k_ref[rs, jj * LANES:(jj + 1) * LANES] = rot.astype(BF16)
        v_ref[rs, :] = pv.astype(BF16)
        gq_ref[rs, :] = gg[:, 0:256].astype(BF16)
        gk_ref[rs, :] = gg[:, 256:512].astype(BF16)
        gv_ref[rs, :] = gg[:, 512:1024].astype(BF16)
        gr_ref[rs, :] = gg[:, 1024:1536].astype(BF16)
        z = _dot(ga.astype(BF16), wa2_ref[...]) + ba_ref[...]
        la_ref[rs, :] = (jnp.minimum(z, 0.0) - jnp.log1p(jnp.exp(-jnp.abs(z)))) * (1.0 / GLA_TAU)


def _inproj(x2d, pos2d, invf, g, wqk, wv, wg, wga, wa2, ba, tm=1024, sub=512):
    n = x2d.shape[0]
    tm, sub = min(tm, n), min(sub, n)
    row = lambda w: pl.BlockSpec((tm, w), lambda i: (i, 0))
    full = lambda a: pl.BlockSpec(a.shape, lambda i: (0,) * a.ndim)
    outs = [(512, BF16), (512, BF16), (512, BF16), (256, BF16), (256, BF16), (512, BF16),
            (512, BF16), (256, F32)]
    return pl.pallas_call(
        functools.partial(_inproj_kernel, sub=sub),
        grid=(n // tm,),
        in_specs=[row(D_MODEL), row(1), full(invf), full(g), full(wqk), full(wv), full(wg),
                  full(wga), full(wa2), full(ba)],
        out_specs=[row(w) for w, _ in outs],
        out_shape=[jax.ShapeDtypeStruct((n, w), dt) for w, dt in outs],
        compiler_params=_params(("arbitrary",)),
        name="inproj",
    )(x2d, pos2d, invf, g, wqk, wv, wg, wga, wa2, ba)


def _diffattn_kernel(lq1_ref, lk1_ref, lq2_ref, lk2_ref, gn_ref, q_ref, k_ref, v_ref, o_ref,
                     *, tq):
    seq = q_ref.shape[0]
    lam = (jnp.exp(jnp.sum(lq1_ref[...] * lk1_ref[...], axis=-1, keepdims=True))
           - jnp.exp(jnp.sum(lq2_ref[...] * lk2_ref[...], axis=-1, keepdims=True))
           + LAMBDA_INIT)
    lane = lax.broadcasted_iota(jnp.int32, (tq, LANES), 1)
    r = lax.broadcasted_iota(jnp.int32, (2 * tq, tq), 0) % tq
    c = lax.broadcasted_iota(jnp.int32, (2 * tq, tq), 1)
    causal = c <= r

    def scores(qi):
        q = q_ref[qi * tq:(qi + 1) * tq, :]
        zero = jnp.zeros_like(q)
        qs = jnp.concatenate([jnp.where(lane < DA_DIM, q, zero),
                              jnp.where(lane >= DA_DIM, q, zero)], axis=0)
        past = qi * tq
        s_diag = jnp.where(causal, _dot_nt(qs, k_ref[past:past + tq, :]), -jnp.inf)
        if qi == 0:
            return s_diag
        return jnp.concatenate([_dot_nt(qs, k_ref[0:past, :]), s_diag], axis=1)

    nq = seq // tq
    s_next = scores(0)
    for qi in range(nq):
        s = s_next
        if qi + 1 < nq:
            s_next = scores(qi + 1)
        p = jnp.exp2(s - jnp.max(s, axis=-1, keepdims=True))
        l = jnp.sum(p, axis=-1, keepdims=True)
        a = p[0:tq] - p[tq:2 * tq] * (lam * l[0:tq] / l[tq:2 * tq])
        o = _dot(a.astype(BF16), v_ref[0:(qi + 1) * tq, :]) / l[0:tq]
        o_ref[qi * tq:(qi + 1) * tq, :] = (
            _rms(o, gn_ref[...]) * (1.0 - LAMBDA_INIT)).astype(o_ref.dtype)


def _diffattn(q, k, v, lq1, lk1, lq2, lk2, gn, batch, seq, tq=256):
    vec = lambda a: pl.BlockSpec(a.shape, lambda b, h: (0, 0))
    blk = pl.BlockSpec((seq, LANES), lambda b, h: (b, h))
    return pl.pallas_call(
        functools.partial(_diffattn_kernel, tq=tq),
        grid=(batch, DA_HEADS),
        in_specs=[vec(lq1), vec(lk1), vec(lq2), vec(lk2), vec(gn), blk, blk, blk],
        out_specs=blk,
        out_shape=jax.ShapeDtypeStruct(q.shape, BF16),
        compiler_params=_params(("arbitrary", "arbitrary")),
        name="diffattn",
    )(lq1, lk1, lq2, lk2, gn, q, k, v)


def _gla_kernel(gq_ref, gk_ref, gv_ref, gr_ref, la_ref, gn_ref, o_ref, st_ref, *, rows):
    c = GLA_CHUNK
    nch = rows // c
    kw = GLA_HEADS * GLA_DK
    pw = 2 * GLA_DK
    vw = 2 * GLA_DV

    @pl.when(pl.program_id(1) == 0)
    def _():
        st_ref[...] = jnp.zeros_like(st_ref)

    r_i = lax.broadcasted_iota(jnp.int32, (rows, rows), 0)
    c_i = lax.broadcasted_iota(jnp.int32, (rows, rows), 1)
    causal = (r_i // c == c_i // c) & (c_i <= r_i)
    tril = causal.astype(BF16)
    sr = lax.broadcasted_iota(jnp.int32, (pw, vw), 0) // GLA_DK
    sc = lax.broadcasted_iota(jnp.int32, (pw, vw), 1) // GLA_DV
    blockdiag = sr == sc
    head_of_lane = lax.broadcasted_iota(jnp.int32, (rows, kw), 1) // GLA_DK

    grp = range(o_ref.shape[0])
    b, qe, kn, kd, b_last, v = [], [], [], [], [], []
    for bb in grp:
        hi, mid, lo = _split3(la_ref[bb])
        b.append(_dot(tril, hi) + _dot(tril, mid) + _dot(tril, lo))
    for bb in grp:
        b3 = b[bb].reshape(nch, c, kw)
        b_last.append(b3[:, c - 1:c, :])
        qe.append(gq_ref[bb].astype(F32) * jnp.exp(b[bb]) * (GLA_DK ** -0.5))
        gk = gk_ref[bb].astype(F32)
        kn.append((gk * jnp.exp(-b[bb])).astype(BF16))
        kd.append(gk.reshape(nch, c, kw) * jnp.exp(b_last[bb] - b3))
        v.append(gv_ref[bb])
    attn = [[None] * GLA_HEADS for _ in grp]
    for h in range(GLA_HEADS):
        for bb in grp:
            qh = jnp.where(head_of_lane == h, qe[bb], 0.0).astype(BF16)
            attn[bb][h] = jnp.where(causal, _dot_nt(qh, kn[bb]), 0.0).astype(BF16)
    intra = [[None] * GLA_HEADS for _ in grp]
    for h in range(GLA_HEADS):
        for bb in grp:
            intra[bb][h] = _dot(attn[bb][h], v[bb][:, h * GLA_DV:(h + 1) * GLA_DV])

    pairs = [(bb, p) for p in range(GLA_HEADS // 2) for bb in grp]
    qeb = [q_.astype(BF16) for q_ in qe]
    upd, decay = {}, {}
    for bb, p in pairs:
        ks = slice(p * pw, (p + 1) * pw)
        vs = slice(p * vw, (p + 1) * vw)
        for ci in range(nch):
            rs = slice(ci * c, (ci + 1) * c)
            u = _dot(kd[bb][ci][:, ks].T.astype(BF16), v[bb][rs, vs])
            upd[bb, p, ci] = jnp.where(blockdiag, u, 0.0)
            decay[bb, p, ci] = jnp.exp(b_last[bb][ci][:, ks]).T
    inter = {}
    for bb, p in pairs:
        ks = slice(p * pw, (p + 1) * pw)
        st = st_ref[bb, p]
        parts = []
        for ci in range(nch):
            rs = slice(ci * c, (ci + 1) * c)
            parts.append(_dot(qeb[bb][rs, ks], st.astype(BF16)))
            st = st * decay[bb, p, ci] + upd[bb, p, ci]
        st_ref[bb, p] = st
        inter[bb, p] = jnp.concatenate(parts, axis=0)

    for bb in grp:
        for h in range(GLA_HEADS):
            hs = slice(h * GLA_DV, (h + 1) * GLA_DV)
            o = inter[bb, h // 2][:, (h % 2) * GLA_DV:(h % 2 + 1) * GLA_DV] + intra[bb][h]
            gr = gr_ref[bb, :, hs].astype(F32)
            y = _rms(o, gn_ref[...]) * (gr * jax.nn.sigmoid(gr))
            o_ref[bb, :, hs] = y.astype(o_ref.dtype)


def _gla(gq, gk, gv, gr, la, gn, batch, seq, rows=256):
    nb = seq // rows
    grp = math.gcd(batch, GLA_GROUP)
    seqs = lambda a: a.reshape(batch, seq, a.shape[-1])
    blk = lambda w: pl.BlockSpec((grp, rows, w), lambda g, i: (g, i, 0))
    out = pl.pallas_call(
        functools.partial(_gla_kernel, rows=rows),
        grid=(batch // grp, nb),
        in_specs=[blk(256), blk(256), blk(512), blk(512), blk(256),
                  pl.BlockSpec(gn.shape, lambda g, i: (0, 0))],
        out_specs=blk(512),
        out_shape=jax.ShapeDtypeStruct((batch, seq, gv.shape[-1]), BF16),
        scratch_shapes=[pltpu.VMEM((grp, GLA_HEADS // 2, 2 * GLA_DK, 2 * GLA_DV), F32)],
        compiler_params=_params(("arbitrary", "arbitrary")),
        name="gla",
    )(seqs(gq), seqs(gk), seqs(gv), seqs(gr), seqs(la), gn)
    return out.reshape(gv.shape)


def _memkv_kernel(m_ref, g_ref, wk_ref, wv_ref, k_ref, v_ref):
    hm = _rms(m_ref[...], g_ref[...]).astype(BF16)
    k_ref[...] = _dot(hm, wk_ref[...]).astype(BF16)
    v_ref[...] = _dot(hm, wv_ref[...]).astype(BF16)


def _memkv(mem2d, g, wk, wv, tm=256):
    n = mem2d.shape[0]
    row = pl.BlockSpec((tm, D_MODEL), lambda i: (i, 0))
    full = lambda a: pl.BlockSpec(a.shape, lambda i: (0,) * a.ndim)
    return pl.pallas_call(
        _memkv_kernel,
        grid=(n // tm,),
        in_specs=[row, full(g), full(wk), full(wv)],
        out_specs=[row, row],
        out_shape=[jax.ShapeDtypeStruct((n, D_MODEL), BF16)] * 2,
        compiler_params=_params(("arbitrary",)),
        name="memkv",
    )(mem2d, g, wk, wv)


def _cross_kernel(oda_ref, ogla_ref, x_ref, kc_ref, vc_ref, wo_ref, gq_ref, wcq_ref, wco_ref,
                  g_ref, wrt_ref, br_ref,
                  x2_ref, t_ref, e_ref, gate_ref, rank_ref, cnt_ref, *, tile):
    subs = [slice(j * tile, (j + 1) * tile) for j in range(x_ref.shape[0] // tile)]
    half = oda_ref.shape[1]
    x1 = [x_ref[cs, :] + (_dot(oda_ref[cs, :], wo_ref[0:half, :])
                          + _dot(ogla_ref[cs, :], wo_ref[half:, :])) for cs in subs]
    hq = [_rms(a, gq_ref[...]).astype(BF16) for a in x1]
    qc = [(_dot(a, wcq_ref[...]) * (X_DIM ** -0.5)).astype(BF16) for a in hq]
    heads = [slice(h * X_DIM, (h + 1) * X_DIM) for h in range(X_HEADS)]
    s = [[_dot_nt(q[:, hs], kc_ref[:, hs]) for hs in heads] for q in qc]
    p = [[jnp.exp(a - jnp.max(a, axis=-1, keepdims=True)) for a in row] for row in s]
    o = [jnp.concatenate(
        [(_dot(a.astype(BF16), vc_ref[:, hs]) / jnp.sum(a, axis=-1, keepdims=True)).astype(BF16)
         for a, hs in zip(row, heads)], axis=1) for row in p]
    x2 = [a + _dot(b, wco_ref[...]) for a, b in zip(x1, o)]
    t = [_rms(a, g_ref[...]) for a in x2]
    for cs, a, b in zip(subs, x2, t):
        x2_ref[cs, :] = a
        t_ref[cs, :] = b.astype(t_ref.dtype)

    w_hi, w_mid, _ = _split3(wrt_ref[...])
    ur = lax.broadcasted_iota(jnp.int32, (tile, tile), 0)
    uc = lax.broadcasted_iota(jnp.int32, (tile, tile), 1)
    before = (ur < uc).astype(BF16)
    iota_e = lax.broadcasted_iota(jnp.int32, (N_EXPERTS, tile), 0)
    logit = []
    for a in t:
        t_hi, t_mid, _ = _split3(a)
        logit.append((_dot_nt(w_hi, t_hi) + _dot_nt(w_hi, t_mid) + _dot_nt(w_mid, t_hi))
                     + br_ref[...])
    vals, idxs, sels = [[] for _ in subs], [[] for _ in subs], [[] for _ in subs]
    for _ in range(TOP_K):
        for j in range(len(subs)):
            mx = jnp.max(logit[j], axis=0, keepdims=True)
            idx = jnp.min(jnp.where(logit[j] == mx, iota_e, N_EXPERTS), axis=0, keepdims=True)
            sel = iota_e == idx
            vals[j].append(mx)
            idxs[j].append(idx)
            sels[j].append(sel)
            logit[j] = jnp.where(sel, -jnp.inf, logit[j])
    for j, cs in enumerate(subs):
        ex = [jnp.exp(v - vals[j][0]) for v in vals[j]]
        den = ex[0] + ex[1] + ex[2] + ex[3]
        gate_ref[:, cs] = jnp.concatenate([e / den for e in ex], axis=0)
        e_ref[:, cs] = jnp.concatenate(idxs[j], axis=0)
        onehot = (sels[j][0] | sels[j][1] | sels[j][2] | sels[j][3])
        base = _dot(onehot.astype(BF16), before)
        rank_ref[:, cs] = jnp.concatenate(
            [jnp.sum(jnp.where(s_, base, 0.0), axis=0, keepdims=True) for s_ in sels[j]],
            axis=0).astype(jnp.int32)
        cnt_ref[j] = jnp.sum(onehot.astype(F32), axis=1, keepdims=True)


def _cross(oda, ogla, x2d, kc, vc, wo, gq, wcq, wco, g, wrt, br, batch, seq, mem_len, tile, tq):
    nq = seq // tq
    n = batch * seq
    row = lambda w: pl.BlockSpec((tq, w), lambda b, i: (b * nq + i, 0))
    col = lambda r: pl.BlockSpec((r, tq), lambda b, i: (0, b * nq + i))
    full = lambda a: pl.BlockSpec(a.shape, lambda b, i: (0,) * a.ndim)
    memb = pl.BlockSpec((mem_len, D_MODEL), lambda b, i: (b, 0))
    return pl.pallas_call(
        functools.partial(_cross_kernel, tile=tile),
        grid=(batch, nq),
        in_specs=[row(oda.shape[1]), row(ogla.shape[1]), row(D_MODEL), memb, memb, full(wo),
                  full(gq), full(wcq), full(wco), full(g), full(wrt), full(br)],
        out_specs=[row(D_MODEL), row(D_MODEL), col(TOP_K), col(TOP_K), col(TOP_K),
                   pl.BlockSpec((tq // tile, N_EXPERTS, 1), lambda b, i: (b * nq + i, 0, 0))],
        out_shape=[jax.ShapeDtypeStruct((n, D_MODEL), F32),
                   jax.ShapeDtypeStruct((n, D_MODEL), BF16),
                   jax.ShapeDtypeStruct((TOP_K, n), jnp.int32),
                   jax.ShapeDtypeStruct((TOP_K, n), F32),
                   jax.ShapeDtypeStruct((TOP_K, n), jnp.int32),
                   jax.ShapeDtypeStruct((n // tile, N_EXPERTS, 1), F32)],
        compiler_params=_params(("arbitrary", "arbitrary"), vmem=56 * 1024 * 1024),
        name="mix_cross_router",
    )(oda, ogla, x2d, kc, vc, wo, gq, wcq, wco, g, wrt, br)


def _run_copy(hbm_ref, tab_ref, loc_ref, sem, tile, e, s, to_hbm):
    base = (tile * N_EXPERTS + e) * 3
    n = pl.multiple_of(tab_ref[base + 2], ROW_GRAN)
    loc = loc_ref.at[s, pl.ds(pl.multiple_of(tab_ref[base + 1], ROW_GRAN), n)]
    far = hbm_ref.at[pl.ds(pl.multiple_of(tab_ref[base], ROW_GRAN), n)]
    return pltpu.make_async_copy(loc, far, sem.at[s]) if to_hbm else \
        pltpu.make_async_copy(far, loc, sem.at[s])


def _dispatch_kernel(dst_ref, nct_ref, tail_ref, slot_ref, t_ref, xp_ref, xloc_ref, zero_ref,
                     sem, zsem, *, blk, mc):
    i = pl.program_id(0)
    s = i % 2
    lmax = xloc_ref.shape[1]

    @pl.when(i == 0)
    def _():
        zero_ref[...] = jnp.zeros_like(zero_ref)
        tails = [pl.multiple_of(tail_ref[e], blk) for e in range(N_EXPERTS)]
        for e in range(N_EXPERTS):
            pltpu.make_async_copy(zero_ref, xp_ref.at[pl.ds(tails[e], blk)], zsem).start()
        for e in range(N_EXPERTS):
            pltpu.make_async_copy(zero_ref, xp_ref.at[pl.ds(tails[e], blk)], zsem).wait()

    def compact(r0, r1):
        rows = lax.broadcasted_iota(jnp.int32, (r1 - r0, t_ref.shape[0]), 0) + r0
        hit = rows == slot_ref[0:1, :]
        for k in range(1, TOP_K):
            hit = hit | (rows == slot_ref[k:k + 1, :])
        xloc_ref[s, r0:r1] = _dot(jnp.where(hit, 1.0, 0.0).astype(BF16),
                                  t_ref[...]).astype(BF16)

    base = TOP_K * t_ref.shape[0]
    compact(0, base)
    for r0 in range(base, lmax, TAIL_ROWS):
        @pl.when(nct_ref[i] * ROW_GRAN > r0)
        def _():
            compact(r0, min(r0 + TAIL_ROWS, lmax))

    def issue(e, carry):
        @pl.when(dst_ref[(i * N_EXPERTS + e) * 3 + 2] > 0)
        def _():
            _run_copy(xp_ref, dst_ref, xloc_ref, sem, i, e, s, True).start()
        return carry

    lax.fori_loop(0, N_EXPERTS, issue, 0)

    def drain(tile, slot):
        rows_out = nct_ref[tile] * ROW_GRAN
        pltpu.make_async_copy(xloc_ref.at[slot, pl.ds(0, rows_out)],
                              xp_ref.at[pl.ds(0, rows_out)], sem.at[slot]).wait()

    @pl.when(i > 0)
    def _():
        drain(i - 1, 1 - s)

    @pl.when(i == pl.num_programs(0) - 1)
    def _():
        drain(i, s)


def _dispatch(dst_tab, nct, tail, slot, t, cap, blk, lmax, tm):
    n, d = t.shape
    mc = lmax // ROW_GRAN
    return pl.pallas_call(
        functools.partial(_dispatch_kernel, blk=blk, mc=mc),
        grid_spec=pltpu.PrefetchScalarGridSpec(
            num_scalar_prefetch=3,
            grid=(n // tm,),
            in_specs=[pl.BlockSpec((TOP_K, tm), lambda i, *_: (0, i)),
                      pl.BlockSpec((tm, d), lambda i, *_: (i, 0))],
            out_specs=pl.BlockSpec(memory_space=pl.ANY),
            scratch_shapes=[pltpu.VMEM((2, lmax, d), BF16), pltpu.VMEM((blk, d), BF16),
                            pltpu.SemaphoreType.DMA((2,)), pltpu.SemaphoreType.DMA(())]),
        out_shape=jax.ShapeDtypeStruct((cap, d), BF16),
        compiler_params=_params(("arbitrary",)),
        name="moe_dispatch",
    )(dst_tab, nct, tail, slot, t)


def _expert_kernel(blk_e_ref, nused_ref, x_ref, wu_ref, bu_ref, wd_ref, bd_ref, y_ref,
                   wub_ref, wdb_ref):
    i = pl.program_id(0)

    @pl.when((i == 0) | (blk_e_ref[i] != blk_e_ref[jnp.maximum(i - 1, 0)]))
    def _():
        wub_ref[...] = wu_ref[...].astype(BF16)
        wdb_ref[...] = wd_ref[...].astype(BF16)

    @pl.when(i < nused_ref[0])
    def _():
        u = _dot(x_ref[...], wub_ref[...]) + bu_ref[...]
        glu = jnp.minimum(u[:, :D_FF], SWIGLU_LIMIT)
        lin = jnp.clip(u[:, D_FF:], -SWIGLU_LIMIT, SWIGLU_LIMIT)
        act = glu * jax.nn.sigmoid(SWIGLU_ALPHA * glu) * (lin + 1.0)
        y_ref[...] = (_dot(act.astype(BF16), wdb_ref[...]) + bd_ref[...]).astype(y_ref.dtype)


def _experts(blk_e, n_used, x_pad, wu, bu, wd, bd, blk):
    cap = x_pad.shape[0]
    n_blk = cap // blk
    rowmap = lambda i, be, nu: (jnp.minimum(i, nu[0] - 1), 0)
    emap = lambda i, be, nu: (be[i], 0, 0)
    return pl.pallas_call(
        _expert_kernel,
        grid_spec=pltpu.PrefetchScalarGridSpec(
            num_scalar_prefetch=2,
            grid=(n_blk,),
            in_specs=[pl.BlockSpec((blk, D_MODEL), rowmap),
                      pl.BlockSpec((None, D_MODEL, 2 * D_FF), emap),
                      pl.BlockSpec((None, 1, 2 * D_FF), emap),
                      pl.BlockSpec((None, D_FF, D_MODEL), emap),
                      pl.BlockSpec((None, 1, D_MODEL), emap)],
            out_specs=pl.BlockSpec((blk, D_MODEL), rowmap),
            scratch_shapes=[pltpu.VMEM((D_MODEL, 2 * D_FF), BF16),
                            pltpu.VMEM((D_FF, D_MODEL), BF16)]),
        out_shape=jax.ShapeDtypeStruct((cap, D_MODEL), BF16),
        compiler_params=_params(("arbitrary",), vmem=56 * 1024 * 1024),
        name="moe_experts",
    )(blk_e, n_used, x_pad, wu, bu, wd, bd)


def _combine_kernel(dst_ref, nct_ref, slot_ref, gate_ref, x2_ref, g_ref, y_ref, o_ref,
                    yloc_ref, sem, *, mc):
    i = pl.program_id(0)
    s = i % 2
    lmax = yloc_ref.shape[1]

    def issue(tile, slot):
        def body(e, carry):
            @pl.when(dst_ref[(tile * N_EXPERTS + e) * 3 + 2] > 0)
            def _():
                _run_copy(y_ref, dst_ref, yloc_ref, sem, tile, e, slot, False).start()
            return carry
        lax.fori_loop(0, N_EXPERTS, body, 0)

    @pl.when(i == 0)
    def _():
        yloc_ref[...] = jnp.zeros_like(yloc_ref)
        issue(0, 0)

    @pl.when(i + 1 < pl.num_programs(0))
    def _():
        issue(i + 1, 1 - s)

    rows_in = nct_ref[i] * ROW_GRAN
    pltpu.make_async_copy(y_ref.at[pl.ds(0, rows_in)], yloc_ref.at[s, pl.ds(0, rows_in)],
                          sem.at[s]).wait()

    cols = lax.broadcasted_iota(jnp.int32, (x2_ref.shape[0], lmax), 1)
    w = jnp.where(cols == slot_ref[:, 0:1], gate_ref[:, 0:1], 0.0)
    for k in range(1, TOP_K):
        w = w + jnp.where(cols == slot_ref[:, k:k + 1], gate_ref[:, k:k + 1], 0.0)
    acc = x2_ref[...] + _dot(w.astype(BF16), yloc_ref[s])
    o_ref[...] = _rms(acc, g_ref[...])


def _combine(dst_tab, nct, slot_t, gate_t, x2, g, y_pad, lmax, tm):
    n = x2.shape[0]
    mc = lmax // ROW_GRAN
    row = pl.BlockSpec((tm, D_MODEL), lambda i, *_: (i, 0))
    col = pl.BlockSpec((tm, TOP_K), lambda i, *_: (i, 0))
    return pl.pallas_call(
        functools.partial(_combine_kernel, mc=mc),
        grid_spec=pltpu.PrefetchScalarGridSpec(
            num_scalar_prefetch=2,
            grid=(n // tm,),
            in_specs=[col, col, row, pl.BlockSpec(g.shape, lambda i, *_: (0, 0)),
                      pl.BlockSpec(memory_space=pl.ANY)],
            out_specs=row,
            scratch_shapes=[pltpu.VMEM((2, lmax, D_MODEL), BF16),
                            pltpu.SemaphoreType.DMA((2,))]),
        out_shape=jax.ShapeDtypeStruct((n, D_MODEL), F32),
        compiler_params=_params(("arbitrary",)),
        name="moe_combine",
    )(dst_tab, nct, slot_t, gate_t, x2, g, y_pad)


def _rope_inv_freq():
    inv = ROPE_THETA ** (-np.arange(0, ROPE_DIM, 2, dtype=np.float32) / ROPE_DIM)
    lane = np.arange(LANES) % DA_DIM
    tab = np.where(lane < ROPE_DIM, inv.astype(np.float32)[lane % (ROPE_DIM // 2)], 0.0)
    return jnp.asarray(tab.astype(np.float32)[None, :])


def kernel(x, mem, positions, norm_mix_g, w_in, lambda_q1, lambda_k1, lambda_q2, lambda_k2, diff_norm_g, w_alpha2, b_alpha, gla_norm_g, w_out, norm_cross_g, norm_mem_g, w_cq, w_ck, w_cv, w_co, norm_ffn_g, w_router, b_router, w_up, b_up, w_down, b_down, norm_final_g):
    batch, seq, d = x.shape
    mem_len = mem.shape[1]
    n = batch * seq
    moe_blk = 512
    row = lambda a: a.reshape(1, -1)

    x2d = x.reshape(n, d)
    w = w_in[0]
    wqk, wv = w[:, :1024].astype(BF16), w[:, 1024:1536].astype(BF16)
    wg, wga = w[:, 1536:3072].astype(BF16), w[:, 3072:].astype(BF16)
    q, k, v, gq, gk, gv, gr, la = _inproj(
        x2d, positions.reshape(n, 1), _rope_inv_freq(), row(norm_mix_g[0]), wqk, wv, wg, wga,
        w_alpha2[0].astype(BF16), row(b_alpha[0]))

    o_da = _diffattn(q, k, v, row(lambda_q1[0]), row(lambda_k1[0]), row(lambda_q2[0]),
                     row(lambda_k2[0]), row(diff_norm_g[0]), batch, seq)
    o_gla = _gla(gq, gk, gv, gr, la, row(gla_norm_g[0]), batch, seq)

    kc, vc = _memkv(mem.reshape(batch * mem_len, d), row(norm_mem_g[0]),
                    w_ck[0].astype(BF16), w_cv[0].astype(BF16))
    tile = 512
    x2, t, top_e, gate, lrank, counts = _cross(
        o_da, o_gla, x2d, kc, vc, w_out[0].astype(BF16), row(norm_cross_g[0]),
        w_cq[0].astype(BF16), w_co[0].astype(BF16), row(norm_ffn_g[0]), w_router[0].T,
        b_router[0].reshape(-1, 1), batch, seq, mem_len, tile, tq=min(seq, 1024))

    nt = n // tile
    lmax = -(-(TOP_K * tile + N_EXPERTS * (ROW_GRAN - 1)) // LANES) * LANES
    mc = lmax // ROW_GRAN
    cnt = counts[:, :, 0].astype(jnp.int32)
    plc = (cnt + ROW_GRAN - 1) // ROW_GRAN * ROW_GRAN
    lend = jnp.cumsum(plc, axis=1)
    lstart = lend - plc
    tile_off = jnp.cumsum(plc, axis=0) - plc
    etot = jnp.sum(plc, axis=0)
    eblk = (etot + moe_blk - 1) // moe_blk * moe_blk
    gend = jnp.cumsum(eblk)
    dst0 = (gend - eblk)[None, :] + tile_off
    cap = (nt * (TOP_K * tile + N_EXPERTS * (ROW_GRAN - 1)) + N_EXPERTS * (moe_blk - 1))
    cap = -(-cap // moe_blk) * moe_blk
    n_blk = cap // moe_blk
    n_used = (gend[-1] // moe_blk).astype(jnp.int32)
    blk_ids = jnp.minimum(jnp.arange(n_blk, dtype=jnp.int32), n_used - 1)
    blk_e = jnp.minimum(jnp.sum(gend[None, :] <= (blk_ids * moe_blk)[:, None], axis=1),
                        N_EXPERTS - 1).astype(jnp.int32)
    tail = jnp.where(etot > 0, gend, gend[-1]).astype(jnp.int32) - moe_blk
    eids = jnp.arange(N_EXPERTS, dtype=jnp.int32)[:, None, None]
    lstart_tok = jnp.repeat(lstart.T, tile, axis=1)[:, None, :]
    slot = (lrank + jnp.sum(jnp.where(top_e[None] == eids, lstart_tok, 0), axis=0)
            ).astype(jnp.int32)
    dst_tab = jnp.stack([dst0, lstart, plc], axis=-1).astype(jnp.int32).reshape(-1)
    nct = (lend[:, -1] // ROW_GRAN).astype(jnp.int32)

    x_pad = _dispatch(dst_tab, nct, tail, slot, t, cap, moe_blk, lmax, tile)
    y_pad = _experts(blk_e, n_used.reshape(1), x_pad, w_up[0], b_up[0][:, None, :], w_down[0],
                     b_down[0][:, None, :], moe_blk)
    out = _combine(dst_tab, nct, slot.T, gate.T, x2, row(norm_final_g), y_pad, lmax, tile)
    return out.reshape(batch, seq, d)
```

```python
import functools
import math

import numpy as np
import jax
import jax.numpy as jnp
from jax import lax
from jax.experimental import pallas as pl
from jax.experimental.pallas import tpu as pltpu

EPS = 1e-6
D_MODEL = 1024
DA_HEADS = 4
DA_DIM = 64
GLA_HEADS = 4
GLA_DK = 64
GLA_DV = 128
GLA_RANK = 16
GLA_TAU = 16.0
GLA_CHUNK = 64
GLA_GROUP = 4
ROPE_THETA = 500000.0
ROPE_DIM = DA_DIM // 4
X_HEADS = 4
X_DIM = D_MODEL // X_HEADS
N_EXPERTS = 32
TOP_K = 4
D_FF = D_MODEL
SWIGLU_LIMIT = 7.0
SWIGLU_ALPHA = 1.702
LAMBDA_INIT = 0.8 - 0.6 * 1.0
Q_SCALE = DA_DIM ** -0.5 * float(np.log2(np.e))

LANES = 128
ROW_GRAN = 16
TAIL_ROWS = 256
BF16 = jnp.bfloat16
F32 = jnp.float32
VMEM_LIMIT = 48 * 1024 * 1024


def _params(sem, vmem=VMEM_LIMIT):
    return pltpu.CompilerParams(dimension_semantics=sem, vmem_limit_bytes=vmem)


def _rms(xf, g):
    return xf * lax.rsqrt(jnp.mean(xf * xf, axis=-1, keepdims=True) + EPS) * g


def _dot(a, b):
    return jnp.dot(a, b, preferred_element_type=F32)


def _dot_nt(a, b):
    return lax.dot_general(a, b, (((1,), (1,)), ((), ())), preferred_element_type=F32)


def _split3(a):
    hi = a.astype(BF16)
    r1 = a - hi.astype(F32)
    mid = r1.astype(BF16)
    lo = (r1 - mid.astype(F32)).astype(BF16)
    return hi, mid, lo


def _inproj_kernel(x_ref, pos_ref, invf_ref, g_ref, wqk_ref, wv_ref, wg_ref, wga_ref,
                   wa2_ref, ba_ref,
                   q_ref, k_ref, v_ref, gq_ref, gk_ref, gv_ref, gr_ref, la_ref, *, sub):
    groups = [slice(j * sub, (j + 1) * sub) for j in range(x_ref.shape[0] // sub)]
    half = ROPE_DIM // 2
    lane = lax.broadcasted_iota(jnp.int32, (sub, LANES), 1) % DA_DIM
    h = [_rms(x_ref[rs, :], g_ref[...]).astype(BF16) for rs in groups]
    tabs = []
    for rs in groups:
        ang = pos_ref[rs, :].astype(F32) * invf_ref[...]
        sinv = jnp.sin(ang)
        tabs.append((jnp.where(lane < ROPE_DIM, jnp.cos(ang), 1.0),
                     jnp.where(lane < half, -sinv, 0.0),
                     jnp.where((lane >= half) & (lane < ROPE_DIM), sinv, 0.0)))
    proj = [(_dot(a, wqk_ref[...]), _dot(a, wg_ref[...]), _dot(a, wv_ref[...]),
             _dot(a, wga_ref[...])) for a in h]
    for rs, (qk, gg, pv, ga), (cosv, s_lo, s_hi) in zip(groups, proj, tabs):
        n_grp = qk.shape[1] // LANES
        for j in range(n_grp):
            t = qk[:, j * LANES:(j + 1) * LANES]
            rot = (t * cosv + pltpu.roll(t, LANES - half, 1) * s_lo
                   + pltpu.roll(t, half, 1) * s_hi)
            if j < n_grp // 2:
                q_ref[rs, j * LANES:(j + 1) * LANES] = (rot * Q_SCALE).astype(BF16)
            else:
                jj = j - n_grp // 2
                k_ref[rs, jj * LANES:(jj + 1) * LANES] = rot.astype(BF16)
        v_ref[rs, :] = pv.astype(BF16)
        gq_ref[rs, :] = gg[:, 0:256].astype(BF16)
        gk_ref[rs, :] = gg[:, 256:512].astype(BF16)
        gv_ref[rs, :] = gg[:, 512:1024].astype(BF16)
        gr_ref[rs, :] = gg[:, 1024:1536].astype(BF16)
        z = _dot(ga.astype(BF16), wa2_ref[...]) + ba_ref[...]
        la_ref[rs, :] = (jnp.minimum(z, 0.0) - jnp.log1p(jnp.exp(-jnp.abs(z)))) * (1.0 / GLA_TAU)


def _inproj(x2d, pos2d, invf, g, wqk, wv, wg, wga, wa2, ba, tm=1024, sub=512):
    n = x2d.shape[0]
    tm, sub = min(tm, n), min(sub, n)
    row = lambda w: pl.BlockSpec((tm, w), lambda i: (i, 0))
    full = lambda a: pl.BlockSpec(a.shape, lambda i: (0,) * a.ndim)
    outs = [(512, BF16), (512, BF16), (512, BF16), (256, BF16), (256, BF16), (512, BF16),
            (512, BF16), (256, F32)]
    return pl.pallas_call(
        functools.partial(_inproj_kernel, sub=sub),
        grid=(n // tm,),
        in_specs=[row(D_MODEL), row(1), full(invf), full(g), full(wqk), full(wv), full(wg),
                  full(wga), full(wa2), full(ba)],
        out_specs=[row(w) for w, _ in outs],
        out_shape=[jax.ShapeDtypeStruct((n, w), dt) for w, dt in outs],
        compiler_params=_params(("arbitrary",)),
        name="inproj",
    )(x2d, pos2d, invf, g, wqk, wv, wg, wga, wa2, ba)


def _diffattn_kernel(lq1_ref, lk1_ref, lq2_ref, lk2_ref, gn_ref, q_ref, k_ref, v_ref, o_ref,
                     *, tq):
    seq = q_ref.shape[0]
    lam = (jnp.exp(jnp.sum(lq1_ref[...] * lk1_ref[...], axis=-1, keepdims=True))
           - jnp.exp(jnp.sum(lq2_ref[...] * lk2_ref[...], axis=-1, keepdims=True))
           + LAMBDA_INIT)
    lane = lax.broadcasted_iota(jnp.int32, (tq, LANES), 1)
    r = lax.broadcasted_iota(jnp.int32, (2 * tq, tq), 0) % tq
    c = lax.broadcasted_iota(jnp.int32, (2 * tq, tq), 1)
    causal = c <= r

    def scores(qi):
        q = q_ref[qi * tq:(qi + 1) * tq, :]
        zero = jnp.zeros_like(q)
        qs = jnp.concatenate([jnp.where(lane < DA_DIM, q, zero),
                              jnp.where(lane >= DA_DIM, q, zero)], axis=0)
        past = qi * tq
        s_diag = jnp.where(causal, _dot_nt(qs, k_ref[past:past + tq, :]), -jnp.inf)
        if qi == 0:
            return s_diag
        return jnp.concatenate([_dot_nt(qs, k_ref[0:past, :]), s_diag], axis=1)

    nq = seq // tq
    s_next = scores(0)
    for qi in range(nq):
        s = s_next
        if qi + 1 < nq:
            s_next = scores(qi + 1)
        p = jnp.exp2(s - jnp.max(s, axis=-1, keepdims=True))
        l = jnp.sum(p, axis=-1, keepdims=True)
        a = p[0:tq] - p[tq:2 * tq] * (lam * l[0:tq] / l[tq:2 * tq])
        o = _dot(a.astype(BF16), v_ref[0:(qi + 1) * tq, :]) / l[0:tq]
        o_ref[qi * tq:(qi + 1) * tq, :] = (
            _rms(o, gn_ref[...]) * (1.0 - LAMBDA_INIT)).astype(o_ref.dtype)


def _diffattn(q, k, v, lq1, lk1, lq2, lk2, gn, batch, seq, tq=256):
    vec = lambda a: pl.BlockSpec(a.shape, lambda b, h: (0, 0))
    blk = pl.BlockSpec((seq, LANES), lambda b, h: (b, h))
    return pl.pallas_call(
        functools.partial(_diffattn_kernel, tq=tq),
        grid=(batch, DA_HEADS),
        in_specs=[vec(lq1), vec(lk1), vec(lq2), vec(lk2), vec(gn), blk, blk, blk],
        out_specs=blk,
        out_shape=jax.ShapeDtypeStruct(q.shape, BF16),
        compiler_params=_params(("arbitrary", "arbitrary")),
        name="diffattn",
    )(lq1, lk1, lq2, lk2, gn, q, k, v)


def _gla_kernel(gq_ref, gk_ref, gv_ref, gr_ref, la_ref, gn_ref, o_ref, st_ref, *, rows):
    c = GLA_CHUNK
    nch = rows // c
    kw = GLA_HEADS * GLA_DK
    pw = 2 * GLA_DK
    vw = 2 * GLA_DV

    @pl.when(pl.program_id(1) == 0)
    def _():
        st_ref[...] = jnp.zeros_like(st_ref)

    r_i = lax.broadcasted_iota(jnp.int32, (rows, rows), 0)
    c_i = lax.broadcasted_iota(jnp.int32, (rows, rows), 1)
    causal = (r_i // c == c_i // c) & (c_i <= r_i)
    tril = causal.astype(BF16)
    sr = lax.broadcasted_iota(jnp.int32, (pw, vw), 0) // GLA_DK
    sc = lax.broadcasted_iota(jnp.int32, (pw, vw), 1) // GLA_DV
    blockdiag = sr == sc
    head_of_lane = lax.broadcasted_iota(jnp.int32, (rows, kw), 1) // GLA_DK

    grp = range(o_ref.shape[0])
    b, qe, kn, kd, b_last, v = [], [], [], [], [], []
    for bb in grp:
        hi, mid, lo = _split3(la_ref[bb])
        b.append(_dot(tril, hi) + _dot(tril, mid) + _dot(tril, lo))
    for bb in grp:
        b3 = b[bb].reshape(nch, c, kw)
        b_last.append(b3[:, c - 1:c, :])
        qe.append(gq_ref[bb].astype(F32) * jnp.exp(b[bb]) * (GLA_DK ** -0.5))
        gk = gk_ref[bb].astype(F32)
        kn.append((gk * jnp.exp(-b[bb])).astype(BF16))
        kd.append(gk.reshape(nch, c, kw) * jnp.exp(b_last[bb] - b3))
        v.append(gv_ref[bb])
    attn = [[None] * GLA_HEADS for _ in grp]
    for h in range(GLA_HEADS):
        for bb in grp:
            qh = jnp.where(head_of_lane == h, qe[bb], 0.0).astype(BF16)
            attn[bb][h] = jnp.where(causal, _dot_nt(qh, kn[bb]), 0.0).astype(BF16)
    intra = [[None] * GLA_HEADS for _ in grp]
    for h in range(GLA_HEADS):
        for bb in grp:
            intra[bb][h] = _dot(attn[bb][h], v[bb][:, h * GLA_DV:(h + 1) * GLA_DV])

    pairs = [(bb, p) for p in range(GLA_HEADS // 2) for bb in grp]
    qeb = [q_.astype(BF16) for q_ in qe]
    upd, decay = {}, {}
    for bb, p in pairs:
        ks = slice(p * pw, (p + 1) * pw)
        vs = slice(p * vw, (p + 1) * vw)
        for ci in range(nch):
            rs = slice(ci * c, (ci + 1) * c)
            u = _dot(kd[bb][ci][:, ks].T.astype(BF16), v[bb][rs, vs])
            upd[bb, p, ci] = jnp.where(blockdiag, u, 0.0)
            decay[bb, p, ci] = jnp.exp(b_last[bb][ci][:, ks]).T
    inter = {}
    for bb, p in pairs:
        ks = slice(p * pw, (p + 1) * pw)
        st = st_ref[bb, p]
        parts = []
        for ci in range(nch):
            rs = slice(ci * c, (ci + 1) * c)
            parts.append(_dot(qeb[bb][rs, ks], st.astype(BF16)))
            st = st * decay[bb, p, ci] + upd[bb, p, ci]
        st_ref[bb, p] = st
        inter[bb, p] = jnp.concatenate(parts, axis=0)

    for bb in grp:
        for h in range(GLA_HEADS):
            hs = slice(h * GLA_DV, (h + 1) * GLA_DV)
            o = inter[bb, h // 2][:, (h % 2) * GLA_DV:(h % 2 + 1) * GLA_DV] + intra[bb][h]
            gr = gr_ref[bb, :, hs].astype(F32)
            y = _rms(o, gn_ref[...]) * (gr * jax.nn.sigmoid(gr))
            o_ref[bb, :, hs] = y.astype(o_ref.dtype)


def _gla(gq, gk, gv, gr, la, gn, batch, seq, rows=256):
    nb = seq // rows
    grp = math.gcd(batch, GLA_GROUP)
    seqs = lambda a: a.reshape(batch, seq, a.shape[-1])
    blk = lambda w: pl.BlockSpec((grp, rows, w), lambda g, i: (g, i, 0))
    out = pl.pallas_call(
        functools.partial(_gla_kernel, rows=rows),
        grid=(batch // grp, nb),
        in_specs=[blk(256), blk(256), blk(512), blk(512), blk(256),
                  pl.BlockSpec(gn.shape, lambda g, i: (0, 0))],
        out_specs=blk(512),
        out_shape=jax.ShapeDtypeStruct((batch, seq, gv.shape[-1]), BF16),
        scratch_shapes=[pltpu.VMEM((grp, GLA_HEADS // 2, 2 * GLA_DK, 2 * GLA_DV), F32)],
        compiler_params=_params(("arbitrary", "arbitrary")),
        name="gla",
    )(seqs(gq), seqs(gk), seqs(gv), seqs(gr), seqs(la), gn)
    return out.reshape(gv.shape)


def _memkv_kernel(m_ref, g_ref, wk_ref, wv_ref, k_ref, v_ref):
    hm = _rms(m_ref[...], g_ref[...]).astype(BF16)
    k_ref[...] = _dot(hm, wk_ref[...]).astype(BF16)
    v_ref[...] = _dot(hm, wv_ref[...]).astype(BF16)


def _memkv(mem2d, g, wk, wv, tm=256):
    n = mem2d.shape[0]
    row = pl.BlockSpec((tm, D_MODEL), lambda i: (i, 0))
    full = lambda a: pl.BlockSpec(a.shape, lambda i: (0,) * a.ndim)
    return pl.pallas_call(
        _memkv_kernel,
        grid=(n // tm,),
        in_specs=[row, full(g), full(wk), full(wv)],
        out_specs=[row, row],
        out_shape=[jax.ShapeDtypeStruct((n, D_MODEL), BF16)] * 2,
        compiler_params=_params(("arbitrary",)),
        name="memkv",
    )(mem2d, g, wk, wv)


def _cross_kernel(oda_ref, ogla_ref, x_ref, kc_ref, vc_ref, wo_ref, gq_ref, wcq_ref, wco_ref,
                  g_ref, wrt_ref, br_ref,
                  x2_ref, t_ref, e_ref, gate_ref, rank_ref, cnt_ref, *, tile):
    subs = [slice(j * tile, (j + 1) * tile) for j in range(x_ref.shape[0] // tile)]
    half = oda_ref.shape[1]
    x1 = [x_ref[cs, :] + (_dot(oda_ref[cs, :], wo_ref[0:half, :])
                          + _dot(ogla_ref[cs, :], wo_ref[half:, :])) for cs in subs]
    hq = [_rms(a, gq_ref[...]).astype(BF16) for a in x1]
    qc = [(_dot(a, wcq_ref[...]) * (X_DIM ** -0.5)).astype(BF16) for a in hq]
    heads = [slice(h * X_DIM, (h + 1) * X_DIM) for h in range(X_HEADS)]
    s = [[_dot_nt(q[:, hs], kc_ref[:, hs]) for hs in heads] for q in qc]
    p = [[jnp.exp(a - jnp.max(a, axis=-1, keepdims=True)) for a in row] for row in s]
    o = [jnp.concatenate(
        [(_dot(a.astype(BF16), vc_ref[:, hs]) / jnp.sum(a, axis=-1, keepdims=True)).astype(BF16)
         for a, hs in zip(row, heads)], axis=1) for row in p]
    x2 = [a + _dot(b, wco_ref[...]) for a, b in zip(x1, o)]
    t = [_rms(a, g_ref[...]) for a in x2]
    for cs, a, b in zip(subs, x2, t):
        x2_ref[cs, :] = a
        t_ref[cs, :] = b.astype(t_ref.dtype)

    w_hi, w_mid, _ = _split3(wrt_ref[...])
    ur = lax.broadcasted_iota(jnp.int32, (tile, tile), 0)
    uc = lax.broadcasted_iota(jnp.int32, (tile, tile), 1)
    before = (ur < uc).astype(BF16)
    iota_e = lax.broadcasted_iota(jnp.int32, (N_EXPERTS, tile), 0)
    logit = []
    for a in t:
        t_hi, t_mid, _ = _split3(a)
        logit.append((_dot_nt(w_hi, t_hi) + _dot_nt(w_hi, t_mid) + _dot_nt(w_mid, t_hi))
                     + br_ref[...])
    vals, idxs, sels = [[] for _ in subs], [[] for _ in subs], [[] for _ in subs]
    for _ in range(TOP_K):
        for j in range(len(subs)):
            mx = jnp.max(logit[j], axis=0, keepdims=True)
            idx = jnp.min(jnp.where(logit[j] == mx, iota_e, N_EXPERTS), axis=0, keepdims=True)
            sel = iota_e == idx
            vals[j].append(mx)
            idxs[j].append(idx)
            sels[j].append(sel)
            logit[j] = jnp.where(sel, -jnp.inf, logit[j])
    for j, cs in enumerate(subs):
        ex = [jnp.exp(v - vals[j][0]) for v in vals[j]]
        den = ex[0] + ex[1] + ex[2] + ex[3]
        gate_ref[:, cs] = jnp.concatenate([e / den for e in ex], axis=0)
        e_ref[:, cs] = jnp.concatenate(idxs[j], axis=0)
        onehot = (sels[j][0] | sels[j][1] | sels[j][2] | sels[j][3])
        base = _dot(onehot.astype(BF16), before)
        rank_ref[:, cs] = jnp.concatenate(
            [jnp.sum(jnp.where(s_, base, 0.0), axis=0, keepdims=True) for s_ in sels[j]],
            axis=0).astype(jnp.int32)
        cnt_ref[j] = jnp.sum(onehot.astype(F32), axis=1, keepdims=True)


def _cross(oda, ogla, x2d, kc, vc, wo, gq, wcq, wco, g, wrt, br, batch, seq, mem_len, tile, tq):
    nq = seq // tq
    n = batch * seq
    row = lambda w: pl.BlockSpec((tq, w), lambda b, i: (b * nq + i, 0))
    col = lambda r: pl.BlockSpec((r, tq), lambda b, i: (0, b * nq + i))
    full = lambda a: pl.BlockSpec(a.shape, lambda b, i: (0,) * a.ndim)
    memb = pl.BlockSpec((mem_len, D_MODEL), lambda b, i: (b, 0))
    return pl.pallas_call(
        functools.partial(_cross_kernel, tile=tile),
        grid=(batch, nq),
        in_specs=[row(oda.shape[1]), row(ogla.shape[1]), row(D_MODEL), memb, memb, full(wo),
                  full(gq), full(wcq), full(wco), full(g), full(wrt), full(br)],
        out_specs=[row(D_MODEL), row(D_MODEL), col(TOP_K), col(TOP_K), col(TOP_K),
                   pl.BlockSpec((tq // tile, N_EXPERTS, 1), lambda b, i: (b * nq + i, 0, 0))],
        out_shape=[jax.ShapeDtypeStruct((n, D_MODEL), F32),
                   jax.ShapeDtypeStruct((n, D_MODEL), BF16),
                   jax.ShapeDtypeStruct((TOP_K, n), jnp.int32),
                   jax.ShapeDtypeStruct((TOP_K, n), F32),
                   jax.ShapeDtypeStruct((TOP_K, n), jnp.int32),
                   jax.ShapeDtypeStruct((n // tile, N_EXPERTS, 1), F32)],
        compiler_params=_params(("arbitrary", "arbitrary"), vmem=56 * 1024 * 1024),
        name="mix_cross_router",
    )(oda, ogla, x2d, kc, vc, wo, gq, wcq, wco, g, wrt, br)


def _run_copy(hbm_ref, tab_ref, loc_ref, sem, tile, e, s, to_hbm):
    base = (tile * N_EXPERTS + e) * 3
    n = pl.multiple_of(tab_ref[base + 2], ROW_GRAN)
    loc = loc_ref.at[s, pl.ds(pl.multiple_of(tab_ref[base + 1], ROW_GRAN), n)]
    far = hbm_ref.at[pl.ds(pl.multiple_of(tab_ref[base], ROW_GRAN), n)]
    return pltpu.make_async_copy(loc, far, sem.at[s]) if to_hbm else \
        pltpu.make_async_copy(far, loc, sem.at[s])


def _dispatch_kernel(dst_ref, nct_ref, tail_ref, slot_ref, t_ref, xp_ref, xloc_ref, zero_ref,
                     sem, zsem, *, blk, mc):
    i = pl.program_id(0)
    s = i % 2
    lmax = xloc_ref.shape[1]

    @pl.when(i == 0)
    def _():
        zero_ref[...] = jnp.zeros_like(zero_ref)
        tails = [pl.multiple_of(tail_ref[e], blk) for e in range(N_EXPERTS)]
        for e in range(N_EXPERTS):
            pltpu.make_async_copy(zero_ref, xp_ref.at[pl.ds(tails[e], blk)], zsem).start()
        for e in range(N_EXPERTS):
            pltpu.make_async_copy(zero_ref, xp_ref.at[pl.ds(tails[e], blk)], zsem).wait()

    def compact(r0, r1):
        rows = lax.broadcasted_iota(jnp.int32, (r1 - r0, t_ref.shape[0]), 0) + r0
        hit = rows == slot_ref[0:1, :]
        for k in range(1, TOP_K):
            hit = hit | (rows == slot_ref[k:k + 1, :])
        xloc_ref[s, r0:r1] = _dot(jnp.where(hit, 1.0, 0.0).astype(BF16),
                                  t_ref[...]).astype(BF16)

    base = TOP_K * t_ref.shape[0]
    compact(0, base)
    for r0 in range(base, lmax, TAIL_ROWS):
        @pl.when(nct_ref[i] * ROW_GRAN > r0)
        def _():
            compact(r0, min(r0 + TAIL_ROWS, lmax))

    for e in range(N_EXPERTS):
        _run_copy(xp_ref, dst_ref, xloc_ref, sem, i, e, s, True).start()

    def drain(tile, slot):
        rows_out = nct_ref[tile] * ROW_GRAN
        pltpu.make_async_copy(xloc_ref.at[slot, pl.ds(0, rows_out)],
                              xp_ref.at[pl.ds(0, rows_out)], sem.at[slot]).wait()

    @pl.when(i > 0)
    def _():
        drain(i - 1, 1 - s)

    @pl.when(i == pl.num_programs(0) - 1)
    def _():
        drain(i, s)


def _dispatch(dst_tab, nct, tail, slot, t, cap, blk, lmax, tm):
    n, d = t.shape
    mc = lmax // ROW_GRAN
    return pl.pallas_call(
        functools.partial(_dispatch_kernel, blk=blk, mc=mc),
        grid_spec=pltpu.PrefetchScalarGridSpec(
            num_scalar_prefetch=3,
            grid=(n // tm,),
            in_specs=[pl.BlockSpec((TOP_K, tm), lambda i, *_: (0, i)),
                      pl.BlockSpec((tm, d), lambda i, *_: (i, 0))],
            out_specs=pl.BlockSpec(memory_space=pl.ANY),
            scratch_shapes=[pltpu.VMEM((2, lmax, d), BF16), pltpu.VMEM((blk, d), BF16),
                            pltpu.SemaphoreType.DMA((2,)), pltpu.SemaphoreType.DMA(())]),
        out_shape=jax.ShapeDtypeStruct((cap, d), BF16),
        compiler_params=_params(("arbitrary",)),
        name="moe_dispatch",
    )(dst_tab, nct, tail, slot, t)


def _expert_kernel(blk_e_ref, nused_ref, x_ref, wu_ref, bu_ref, wd_ref, bd_ref, y_ref,
                   wub_ref, wdb_ref):
    i = pl.program_id(0)

    @pl.when((i == 0) | (blk_e_ref[i] != blk_e_ref[jnp.maximum(i - 1, 0)]))
    def _():
        wub_ref[...] = wu_ref[...].astype(BF16)
        wdb_ref[...] = wd_ref[...].astype(BF16)

    @pl.when(i < nused_ref[0])
    def _():
        u = _dot(x_ref[...], wub_ref[...]) + bu_ref[...]
        glu = jnp.minimum(u[:, :D_FF], SWIGLU_LIMIT)
        lin = jnp.clip(u[:, D_FF:], -SWIGLU_LIMIT, SWIGLU_LIMIT)
        act = glu * jax.nn.sigmoid(SWIGLU_ALPHA * glu) * (lin + 1.0)
        y_ref[...] = (_dot(act.astype(BF16), wdb_ref[...]) + bd_ref[...]).astype(y_ref.dtype)


def _experts(blk_e, n_used, x_pad, wu, bu, wd, bd, blk):
    cap = x_pad.shape[0]
    n_blk = cap // blk
    rowmap = lambda i, be, nu: (jnp.minimum(i, nu[0] - 1), 0)
    emap = lambda i, be, nu: (be[i], 0, 0)
    return pl.pallas_call(
        _expert_kernel,
        grid_spec=pltpu.PrefetchScalarGridSpec(
            num_scalar_prefetch=2,
            grid=(n_blk,),
            in_specs=[pl.BlockSpec((blk, D_MODEL), rowmap),
                      pl.BlockSpec((None, D_MODEL, 2 * D_FF), emap),
                      pl.BlockSpec((None, 1, 2 * D_FF), emap),
                      pl.BlockSpec((None, D_FF, D_MODEL), emap),
                      pl.BlockSpec((None, 1, D_MODEL), emap)],
            out_specs=pl.BlockSpec((blk, D_MODEL), rowmap),
            scratch_shapes=[pltpu.VMEM((D_MODEL, 2 * D_FF), BF16),
                            pltpu.VMEM((D_FF, D_MODEL), BF16)]),
        out_shape=jax.ShapeDtypeStruct((cap, D_MODEL), BF16),
        compiler_params=_params(("arbitrary",), vmem=56 * 1024 * 1024),
        name="moe_experts",
    )(blk_e, n_used, x_pad, wu, bu, wd, bd)


def _combine_kernel(dst_ref, nct_ref, slot_ref, gate_ref, x2_ref, g_ref, y_ref, o_ref,
                    yloc_ref, sem, *, mc):
    i = pl.program_id(0)
    s = i % 2
    lmax = yloc_ref.shape[1]

    def issue(tile, slot):
        for e in range(N_EXPERTS):
            _run_copy(y_ref, dst_ref, yloc_ref, sem, tile, e, slot, False).start()

    @pl.when(i == 0)
    def _():
        yloc_ref[...] = jnp.zeros_like(yloc_ref)
        issue(0, 0)

    @pl.when(i + 1 < pl.num_programs(0))
    def _():
        issue(i + 1, 1 - s)

    rows_in = nct_ref[i] * ROW_GRAN
    pltpu.make_async_copy(y_ref.at[pl.ds(0, rows_in)], yloc_ref.at[s, pl.ds(0, rows_in)],
                          sem.at[s]).wait()

    cols = lax.broadcasted_iota(jnp.int32, (x2_ref.shape[0], lmax), 1)
    w = jnp.zeros(cols.shape, F32)
    for k in range(TOP_K):
        w = jnp.where(cols == slot_ref[:, k:k + 1], gate_ref[:, k:k + 1], w)
    acc = x2_ref[...] + _dot(w.astype(BF16), yloc_ref[s])
    o_ref[...] = _rms(acc, g_ref[...])


def _combine(dst_tab, nct, slot_t, gate_t, x2, g, y_pad, lmax, tm):
    n = x2.shape[0]
    mc = lmax // ROW_GRAN
    row = pl.BlockSpec((tm, D_MODEL), lambda i, *_: (i, 0))
    col = pl.BlockSpec((tm, TOP_K), lambda i, *_: (i, 0))
    return pl.pallas_call(
        functools.partial(_combine_kernel, mc=mc),
        grid_spec=pltpu.PrefetchScalarGridSpec(
            num_scalar_prefetch=2,
            grid=(n // tm,),
            in_specs=[col, col, row, pl.BlockSpec(g.shape, lambda i, *_: (0, 0)),
                      pl.BlockSpec(memory_space=pl.ANY)],
            out_specs=row,
            scratch_shapes=[pltpu.VMEM((2, lmax, D_MODEL), BF16),
                            pltpu.SemaphoreType.DMA((2,))]),
        out_shape=jax.ShapeDtypeStruct((n, D_MODEL), F32),
        compiler_params=_params(("arbitrary",)),
        name="moe_combine",
    )(dst_tab, nct, slot_t, gate_t, x2, g, y_pad)


def _rope_inv_freq():
    inv = ROPE_THETA ** (-np.arange(0, ROPE_DIM, 2, dtype=np.float32) / ROPE_DIM)
    lane = np.arange(LANES) % DA_DIM
    tab = np.where(lane < ROPE_DIM, inv.astype(np.float32)[lane % (ROPE_DIM // 2)], 0.0)
    return jnp.asarray(tab.astype(np.float32)[None, :])


def kernel(x, mem, positions, norm_mix_g, w_in, lambda_q1, lambda_k1, lambda_q2, lambda_k2, diff_norm_g, w_alpha2, b_alpha, gla_norm_g, w_out, norm_cross_g, norm_mem_g, w_cq, w_ck, w_cv, w_co, norm_ffn_g, w_router, b_router, w_up, b_up, w_down, b_down, norm_final_g):
    batch, seq, d = x.shape
    mem_len = mem.shape[1]
    n = batch * seq
    moe_blk = 512
    row = lambda a: a.reshape(1, -1)

    x2d = x.reshape(n, d)
    w = w_in[0]
    wqk, wv = w[:, :1024].astype(BF16), w[:, 1024:1536].astype(BF16)
    wg, wga = w[:, 1536:3072].astype(BF16), w[:, 3072:].astype(BF16)
    q, k, v, gq, gk, gv, gr, la = _inproj(
        x2d, positions.reshape(n, 1), _rope_inv_freq(), row(norm_mix_g[0]), wqk, wv, wg, wga,
        w_alpha2[0].astype(BF16), row(b_alpha[0]))

    o_da = _diffattn(q, k, v, row(lambda_q1[0]), row(lambda_k1[0]), row(lambda_q2[0]),
                     row(lambda_k2[0]), row(diff_norm_g[0]), batch, seq)
    o_gla = _gla(gq, gk, gv, gr, la, row(gla_norm_g[0]), batch, seq)

    kc, vc = _memkv(mem.reshape(batch * mem_len, d), row(norm_mem_g[0]),
                    w_ck[0].astype(BF16), w_cv[0].astype(BF16))
    tile = 512
    x2, t, top_e, gate, lrank, counts = _cross(
        o_da, o_gla, x2d, kc, vc, w_out[0].astype(BF16), row(norm_cross_g[0]),
        w_cq[0].astype(BF16), w_co[0].astype(BF16), row(norm_ffn_g[0]), w_router[0].T,
        b_router[0].reshape(-1, 1), batch, seq, mem_len, tile, tq=min(seq, 1024))

    nt = n // tile
    lmax = -(-(TOP_K * tile + N_EXPERTS * ROW_GRAN) // LANES) * LANES
    mc = lmax // ROW_GRAN
    cnt = counts[:, :, 0].astype(jnp.int32)
    plc = jnp.maximum((cnt + ROW_GRAN - 1) // ROW_GRAN * ROW_GRAN, ROW_GRAN)
    lend = jnp.cumsum(plc, axis=1)
    lstart = lend - plc
    tile_off = jnp.cumsum(plc, axis=0) - plc
    etot = jnp.sum(plc, axis=0)
    eblk = (etot + moe_blk - 1) // moe_blk * moe_blk
    gend = jnp.cumsum(eblk)
    dst0 = (gend - eblk)[None, :] + tile_off
    cap = nt * (TOP_K * tile + N_EXPERTS * ROW_GRAN) + N_EXPERTS * (moe_blk - 1)
    cap = -(-cap // moe_blk) * moe_blk
    n_blk = cap // moe_blk
    n_used = (gend[-1] // moe_blk).astype(jnp.int32)
    blk_ids = jnp.minimum(jnp.arange(n_blk, dtype=jnp.int32), n_used - 1)
    blk_e = jnp.minimum(jnp.sum(gend[None, :] <= (blk_ids * moe_blk)[:, None], axis=1),
                        N_EXPERTS - 1).astype(jnp.int32)
    tail = jnp.where(etot > 0, gend, gend[-1]).astype(jnp.int32) - moe_blk
    eids = jnp.arange(N_EXPERTS, dtype=jnp.int32)[:, None, None]
    lstart_tok = jnp.repeat(lstart.T, tile, axis=1)[:, None, :]
    slot = (lrank + jnp.sum(jnp.where(top_e[None] == eids, lstart_tok, 0), axis=0)
            ).astype(jnp.int32)
    dst_tab = jnp.stack([dst0, lstart, plc], axis=-1).astype(jnp.int32).reshape(-1)
    nct = (lend[:, -1] // ROW_GRAN).astype(jnp.int32)

    x_pad = _dispatch(dst_tab, nct, tail, slot, t, cap, moe_blk, lmax, tile)
    y_pad = _experts(blk_e, n_used.reshape(1), x_pad, w_up[0], b_up[0][:, None, :], w_down[0],
                     b_down[0][:, None, :], moe_blk)
    out = _combine(dst_tab, nct, slot.T, gate.T, x2, row(norm_final_g), y_pad, lmax, tile)
    return out.reshape(batch, seq, d)
```

```python
import functools
import math

import numpy as np
import jax
import jax.numpy as jnp
from jax import lax
from jax.experimental import pallas as pl
from jax.experimental.pallas import tpu as pltpu

EPS = 1e-6
D_MODEL = 1024
DA_HEADS = 4
DA_DIM = 64
DA_GROUP = 2
GLA_HEADS = 4
GLA_DK = 64
GLA_DV = 128
GLA_RANK = 16
GLA_TAU = 16.0
GLA_CHUNK = 64
GLA_GROUP = 4
ROPE_THETA = 500000.0
ROPE_DIM = DA_DIM // 4
X_HEADS = 4
X_DIM = D_MODEL // X_HEADS
N_EXPERTS = 32
TOP_K = 4
D_FF = D_MODEL
SWIGLU_LIMIT = 7.0
SWIGLU_ALPHA = 1.702
LAMBDA_INIT = 0.8 - 0.6 * 1.0
Q_SCALE = DA_DIM ** -0.5 * float(np.log2(np.e))

LANES = 128
ROW_GRAN = 16
TAIL_ROWS = 256
BF16 = jnp.bfloat16
F32 = jnp.float32
VMEM_LIMIT = 48 * 1024 * 1024


def _params(sem, vmem=VMEM_LIMIT):
    return pltpu.CompilerParams(dimension_semantics=sem, vmem_limit_bytes=vmem)


def _rms(xf, g):
    return xf * lax.rsqrt(jnp.mean(xf * xf, axis=-1, keepdims=True) + EPS) * g


def _dot(a, b):
    return jnp.dot(a, b, preferred_element_type=F32)


def _dot_nt(a, b):
    return lax.dot_general(a, b, (((1,), (1,)), ((), ())), preferred_element_type=F32)


def _split3(a):
    hi = a.astype(BF16)
    r1 = a - hi.astype(F32)
    mid = r1.astype(BF16)
    lo = (r1 - mid.astype(F32)).astype(BF16)
    return hi, mid, lo


def _inproj_kernel(x_ref, pos_ref, invf_ref, g_ref, wqk_ref, wv_ref, wg_ref, wga_ref,
                   wa2_ref, ba_ref,
                   q_ref, k_ref, v_ref, gq_ref, gk_ref, gv_ref, gr_ref, la_ref, *, sub):
    groups = [slice(j * sub, (j + 1) * sub) for j in range(x_ref.shape[0] // sub)]
    half = ROPE_DIM // 2
    lane = lax.broadcasted_iota(jnp.int32, (sub, LANES), 1) % DA_DIM
    h = [_rms(x_ref[rs, :], g_ref[...]).astype(BF16) for rs in groups]
    tabs = []
    for rs in groups:
        ang = pos_ref[rs, :].astype(F32) * invf_ref[...]
        sinv = jnp.sin(ang)
        tabs.append((jnp.where(lane < ROPE_DIM, jnp.cos(ang), 1.0),
                     jnp.where(lane < half, -sinv, 0.0),
                     jnp.where((lane >= half) & (lane < ROPE_DIM), sinv, 0.0)))
    proj = [(_dot(a, wqk_ref[...]), _dot(a, wg_ref[...]), _dot(a, wv_ref[...]),
             _dot(a, wga_ref[...])) for a in h]
    for rs, (qk, gg, pv, ga), (cosv, s_lo, s_hi) in zip(groups, proj, tabs):
        n_grp = qk.shape[1] // LANES
        for j in range(n_grp):
            t = qk[:, j * LANES:(j + 1) * LANES]
            rot = (t * cosv + pltpu.roll(t, LANES - half, 1) * s_lo
                   + pltpu.roll(t, half, 1) * s_hi)
            if j < n_grp // 2:
                q_ref[rs, j * LANES:(j + 1) * LANES] = (rot * Q_SCALE).astype(BF16)
            else:
                jj = j - n_grp // 2
                k_ref[rs, jj * LANES:(jj + 1) * LANES] = rot.astype(BF16)
        v_ref[rs, :] = pv.astype(BF16)
        gq_ref[rs, :] = gg[:, 0:256].astype(BF16)
        gk_ref[rs, :] = gg[:, 256:512].astype(BF16)
        gv_ref[rs, :] = gg[:, 512:1024].astype(BF16)
        gr_ref[rs, :] = gg[:, 1024:1536].astype(BF16)
        z = _dot(ga.astype(BF16), wa2_ref[...]) + ba_ref[...]
        la_ref[rs, :] = (jnp.minimum(z, 0.0) - jnp.log1p(jnp.exp(-jnp.abs(z)))) * (1.0 / GLA_TAU)


def _inproj(x2d, pos2d, invf, g, wqk, wv, wg, wga, wa2, ba, tm=1024, sub=512):
    n = x2d.shape[0]
    tm, sub = min(tm, n), min(sub, n)
    row = lambda w: pl.BlockSpec((tm, w), lambda i: (i, 0))
    full = lambda a: pl.BlockSpec(a.shape, lambda i: (0,) * a.ndim)
    outs = [(512, BF16), (512, BF16), (512, BF16), (256, BF16), (256, BF16), (512, BF16),
            (512, BF16), (256, F32)]
    return pl.pallas_call(
        functools.partial(_inproj_kernel, sub=sub),
        grid=(n // tm,),
        in_specs=[row(D_MODEL), row(1), full(invf), full(g), full(wqk), full(wv), full(wg),
                  full(wga), full(wa2), full(ba)],
        out_specs=[row(w) for w, _ in outs],
        out_shape=[jax.ShapeDtypeStruct((n, w), dt) for w, dt in outs],
        compiler_params=_params(("arbitrary",)),
        name="inproj",
    )(x2d, pos2d, invf, g, wqk, wv, wg, wga, wa2, ba)


def _diffattn_kernel(lq1_ref, lk1_ref, lq2_ref, lk2_ref, gn_ref, q_ref, k_ref, v_ref, o_ref,
                     *, tq):
    seq = q_ref.shape[0]
    lam = (jnp.exp(jnp.sum(lq1_ref[...] * lk1_ref[...], axis=-1, keepdims=True))
           - jnp.exp(jnp.sum(lq2_ref[...] * lk2_ref[...], axis=-1, keepdims=True))
           + LAMBDA_INIT)
    lane = lax.broadcasted_iota(jnp.int32, (tq, LANES), 1)
    r = lax.broadcasted_iota(jnp.int32, (2 * tq, tq), 0) % tq
    c = lax.broadcasted_iota(jnp.int32, (2 * tq, tq), 1)
    causal = c <= r
    heads = [slice(g * LANES, (g + 1) * LANES) for g in range(q_ref.shape[1] // LANES)]

    def scores(hs, qi):
        q = q_ref[qi * tq:(qi + 1) * tq, hs]
        zero = jnp.zeros_like(q)
        qs = jnp.concatenate([jnp.where(lane < DA_DIM, q, zero),
                              jnp.where(lane >= DA_DIM, q, zero)], axis=0)
        past = qi * tq
        s_diag = jnp.where(causal, _dot_nt(qs, k_ref[past:past + tq, hs]), -jnp.inf)
        if qi == 0:
            return s_diag
        return jnp.concatenate([_dot_nt(qs, k_ref[0:past, hs]), s_diag], axis=1)

    nq = seq // tq
    s_next = [scores(hs, 0) for hs in heads]
    for qi in range(nq):
        s_now = s_next
        if qi + 1 < nq:
            s_next = [scores(hs, qi + 1) for hs in heads]
        for hs, s in zip(heads, s_now):
            p = jnp.exp2(s - jnp.max(s, axis=-1, keepdims=True))
            l = jnp.sum(p, axis=-1, keepdims=True)
            a = p[0:tq] - p[tq:2 * tq] * (lam * l[0:tq] / l[tq:2 * tq])
            o = _dot(a.astype(BF16), v_ref[0:(qi + 1) * tq, hs]) / l[0:tq]
            o_ref[qi * tq:(qi + 1) * tq, hs] = (
                _rms(o, gn_ref[...]) * (1.0 - LAMBDA_INIT)).astype(o_ref.dtype)


def _diffattn(q, k, v, lq1, lk1, lq2, lk2, gn, batch, seq, tq=256):
    vec = lambda a: pl.BlockSpec(a.shape, lambda b, h: (0, 0))
    blk = pl.BlockSpec((seq, DA_GROUP * LANES), lambda b, h: (b, h))
    return pl.pallas_call(
        functools.partial(_diffattn_kernel, tq=tq),
        grid=(batch, DA_HEADS // DA_GROUP),
        in_specs=[vec(lq1), vec(lk1), vec(lq2), vec(lk2), vec(gn), blk, blk, blk],
        out_specs=blk,
        out_shape=jax.ShapeDtypeStruct(q.shape, BF16),
        compiler_params=_params(("arbitrary", "arbitrary")),
        name="diffattn",
    )(lq1, lk1, lq2, lk2, gn, q, k, v)


def _gla_kernel(gq_ref, gk_ref, gv_ref, gr_ref, la_ref, gn_ref, o_ref, st_ref, *, rows):
    c = GLA_CHUNK
    nch = rows // c
    kw = GLA_HEADS * GLA_DK
    pw = 2 * GLA_DK
    vw = 2 * GLA_DV

    @pl.when(pl.program_id(1) == 0)
    def _():
        st_ref[...] = jnp.zeros_like(st_ref)

    r_i = lax.broadcasted_iota(jnp.int32, (rows, rows), 0)
    c_i = lax.broadcasted_iota(jnp.int32, (rows, rows), 1)
    causal = (r_i // c == c_i // c) & (c_i <= r_i)
    tril = causal.astype(BF16)
    sr = lax.broadcasted_iota(jnp.int32, (pw, vw), 0) // GLA_DK
    sc = lax.broadcasted_iota(jnp.int32, (pw, vw), 1) // GLA_DV
    blockdiag = sr == sc
    head_of_lane = lax.broadcasted_iota(jnp.int32, (rows, kw), 1) // GLA_DK

    grp = range(o_ref.shape[0])
    b, qe, kn, kd, b_last, v = [], [], [], [], [], []
    for bb in grp:
        hi, mid, lo = _split3(la_ref[bb])
        b.append(_dot(tril, hi) + _dot(tril, mid) + _dot(tril, lo))
    for bb in grp:
        b3 = b[bb].reshape(nch, c, kw)
        b_last.append(b3[:, c - 1:c, :])
        qe.append(gq_ref[bb].astype(F32) * jnp.exp(b[bb]) * (GLA_DK ** -0.5))
        gk = gk_ref[bb].astype(F32)
        kn.append((gk * jnp.exp(-b[bb])).astype(BF16))
        kd.append(gk.reshape(nch, c, kw) * jnp.exp(b_last[bb] - b3))
        v.append(gv_ref[bb])
    attn = [[None] * GLA_HEADS for _ in grp]
    for h in range(GLA_HEADS):
        for bb in grp:
            qh = jnp.where(head_of_lane == h, qe[bb], 0.0).astype(BF16)
            attn[bb][h] = jnp.where(causal, _dot_nt(qh, kn[bb]), 0.0).astype(BF16)
    intra = [[None] * GLA_HEADS for _ in grp]
    for h in range(GLA_HEADS):
        for bb in grp:
            intra[bb][h] = _dot(attn[bb][h], v[bb][:, h * GLA_DV:(h + 1) * GLA_DV])

    pairs = [(bb, p) for p in range(GLA_HEADS // 2) for bb in grp]
    qeb = [q_.astype(BF16) for q_ in qe]
    upd, decay = {}, {}
    for bb, p in pairs:
        ks = slice(p * pw, (p + 1) * pw)
        vs = slice(p * vw, (p + 1) * vw)
        for ci in range(nch):
            rs = slice(ci * c, (ci + 1) * c)
            u = _dot(kd[bb][ci][:, ks].T.astype(BF16), v[bb][rs, vs])
            upd[bb, p, ci] = jnp.where(blockdiag, u, 0.0)
            decay[bb, p, ci] = jnp.exp(b_last[bb][ci][:, ks]).T
    inter = {}
    for bb, p in pairs:
        ks = slice(p * pw, (p + 1) * pw)
        st = st_ref[bb, p]
        parts = []
        for ci in range(nch):
            rs = slice(ci * c, (ci + 1) * c)
            parts.append(_dot(qeb[bb][rs, ks], st.astype(BF16)))
            st = st * decay[bb, p, ci] + upd[bb, p, ci]
        st_ref[bb, p] = st
        inter[bb, p] = jnp.concatenate(parts, axis=0)

    for bb in grp:
        for h in range(GLA_HEADS):
            hs = slice(h * GLA_DV, (h + 1) * GLA_DV)
            o = inter[bb, h // 2][:, (h % 2) * GLA_DV:(h % 2 + 1) * GLA_DV] + intra[bb][h]
            gr = gr_ref[bb, :, hs].astype(F32)
            y = _rms(o, gn_ref[...]) * (gr * jax.nn.sigmoid(gr))
            o_ref[bb, :, hs] = y.astype(o_ref.dtype)


def _gla(gq, gk, gv, gr, la, gn, batch, seq, rows=256):
    nb = seq // rows
    grp = math.gcd(batch, GLA_GROUP)
    seqs = lambda a: a.reshape(batch, seq, a.shape[-1])
    blk = lambda w: pl.BlockSpec((grp, rows, w), lambda g, i: (g, i, 0))
    out = pl.pallas_call(
        functools.partial(_gla_kernel, rows=rows),
        grid=(batch // grp, nb),
        in_specs=[blk(256), blk(256), blk(512), blk(512), blk(256),
                  pl.BlockSpec(gn.shape, lambda g, i: (0, 0))],
        out_specs=blk(512),
        out_shape=jax.ShapeDtypeStruct((batch, seq, gv.shape[-1]), BF16),
        scratch_shapes=[pltpu.VMEM((grp, GLA_HEADS // 2, 2 * GLA_DK, 2 * GLA_DV), F32)],
        compiler_params=_params(("arbitrary", "arbitrary")),
        name="gla",
    )(seqs(gq), seqs(gk), seqs(gv), seqs(gr), seqs(la), gn)
    return out.reshape(gv.shape)


def _memkv_kernel(m_ref, g_ref, wk_ref, wv_ref, k_ref, v_ref):
    hm = _rms(m_ref[...], g_ref[...]).astype(BF16)
    k_ref[...] = _dot(hm, wk_ref[...]).astype(BF16)
    v_ref[...] = _dot(hm, wv_ref[...]).astype(BF16)


def _memkv(mem2d, g, wk, wv, tm=256):
    n = mem2d.shape[0]
    row = pl.BlockSpec((tm, D_MODEL), lambda i: (i, 0))
    full = lambda a: pl.BlockSpec(a.shape, lambda i: (0,) * a.ndim)
    return pl.pallas_call(
        _memkv_kernel,
        grid=(n // tm,),
        in_specs=[row, full(g), full(wk), full(wv)],
        out_specs=[row, row],
        out_shape=[jax.ShapeDtypeStruct((n, D_MODEL), BF16)] * 2,
        compiler_params=_params(("arbitrary",)),
        name="memkv",
    )(mem2d, g, wk, wv)


def _cross_kernel(oda_ref, ogla_ref, x_ref, kc_ref, vc_ref, wo_ref, gq_ref, wcq_ref, wco_ref,
                  g_ref, wrt_ref, br_ref,
                  x2_ref, t_ref, e_ref, gate_ref, rank_ref, cnt_ref, *, tile):
    subs = [slice(j * tile, (j + 1) * tile) for j in range(x_ref.shape[0] // tile)]
    half = oda_ref.shape[1]
    x1 = [x_ref[cs, :] + (_dot(oda_ref[cs, :], wo_ref[0:half, :])
                          + _dot(ogla_ref[cs, :], wo_ref[half:, :])) for cs in subs]
    hq = [_rms(a, gq_ref[...]).astype(BF16) for a in x1]
    qc = [(_dot(a, wcq_ref[...]) * (X_DIM ** -0.5)).astype(BF16) for a in hq]
    heads = [slice(h * X_DIM, (h + 1) * X_DIM) for h in range(X_HEADS)]
    s = [[_dot_nt(q[:, hs], kc_ref[:, hs]) for hs in heads] for q in qc]
    p = [[jnp.exp(a - jnp.max(a, axis=-1, keepdims=True)) for a in row] for row in s]
    o = [jnp.concatenate(
        [(_dot(a.astype(BF16), vc_ref[:, hs]) / jnp.sum(a, axis=-1, keepdims=True)).astype(BF16)
         for a, hs in zip(row, heads)], axis=1) for row in p]
    x2 = [a + _dot(b, wco_ref[...]) for a, b in zip(x1, o)]
    t = [_rms(a, g_ref[...]) for a in x2]
    for cs, a, b in zip(subs, x2, t):
        x2_ref[cs, :] = a
        t_ref[cs, :] = b.astype(t_ref.dtype)

    w_hi, w_mid, _ = _split3(wrt_ref[...])
    ur = lax.broadcasted_iota(jnp.int32, (tile, tile), 0)
    uc = lax.broadcasted_iota(jnp.int32, (tile, tile), 1)
    before = (ur < uc).astype(BF16)
    iota_e = lax.broadcasted_iota(jnp.int32, (N_EXPERTS, tile), 0)
    logit = []
    for a in t:
        t_hi, t_mid, _ = _split3(a)
        logit.append((_dot_nt(w_hi, t_hi) + _dot_nt(w_hi, t_mid) + _dot_nt(w_mid, t_hi))
                     + br_ref[...])
    vals, idxs, sels = [[] for _ in subs], [[] for _ in subs], [[] for _ in subs]
    for _ in range(TOP_K):
        for j in range(len(subs)):
            mx = jnp.max(logit[j], axis=0, keepdims=True)
            idx = jnp.min(jnp.where(logit[j] == mx, iota_e, N_EXPERTS), axis=0, keepdims=True)
            sel = iota_e == idx
            vals[j].append(mx)
            idxs[j].append(idx)
            sels[j].append(sel)
            logit[j] = jnp.where(sel, -jnp.inf, logit[j])
    for j, cs in enumerate(subs):
        ex = [jnp.exp(v - vals[j][0]) for v in vals[j]]
        den = ex[0] + ex[1] + ex[2] + ex[3]
        gate_ref[:, cs] = jnp.concatenate([e / den for e in ex], axis=0)
        e_ref[:, cs] = jnp.concatenate(idxs[j], axis=0)
        onehot = (sels[j][0] | sels[j][1] | sels[j][2] | sels[j][3])
        base = _dot(onehot.astype(BF16), before)
        rank_ref[:, cs] = jnp.concatenate(
            [jnp.sum(jnp.where(s_, base, 0.0), axis=0, keepdims=True) for s_ in sels[j]],
            axis=0).astype(jnp.int32)
        cnt_ref[j] = jnp.sum(onehot.astype(F32), axis=1, keepdims=True)


def _cross(oda, ogla, x2d, kc, vc, wo, gq, wcq, wco, g, wrt, br, batch, seq, mem_len, tile, tq):
    nq = seq // tq
    n = batch * seq
    row = lambda w: pl.BlockSpec((tq, w), lambda b, i: (b * nq + i, 0))
    col = lambda r: pl.BlockSpec((r, tq), lambda b, i: (0, b * nq + i))
    full = lambda a: pl.BlockSpec(a.shape, lambda b, i: (0,) * a.ndim)
    memb = pl.BlockSpec((mem_len, D_MODEL), lambda b, i: (b, 0))
    return pl.pallas_call(
        functools.partial(_cross_kernel, tile=tile),
        grid=(batch, nq),
        in_specs=[row(oda.shape[1]), row(ogla.shape[1]), row(D_MODEL), memb, memb, full(wo),
                  full(gq), full(wcq), full(wco), full(g), full(wrt), full(br)],
        out_specs=[row(D_MODEL), row(D_MODEL), col(TOP_K), col(TOP_K), col(TOP_K),
                   pl.BlockSpec((tq // tile, N_EXPERTS, 1), lambda b, i: (b * nq + i, 0, 0))],
        out_shape=[jax.ShapeDtypeStruct((n, D_MODEL), F32),
                   jax.ShapeDtypeStruct((n, D_MODEL), BF16),
                   jax.ShapeDtypeStruct((TOP_K, n), jnp.int32),
                   jax.ShapeDtypeStruct((TOP_K, n), F32),
                   jax.ShapeDtypeStruct((TOP_K, n), jnp.int32),
                   jax.ShapeDtypeStruct((n // tile, N_EXPERTS, 1), F32)],
        compiler_params=_params(("arbitrary", "arbitrary"), vmem=56 * 1024 * 1024),
        name="mix_cross_router",
    )(oda, ogla, x2d, kc, vc, wo, gq, wcq, wco, g, wrt, br)


def _run_copy(hbm_ref, tab_ref, loc_ref, sem, tile, e, s, to_hbm):
    base = (tile * N_EXPERTS + e) * 3
    n = pl.multiple_of(tab_ref[base + 2], ROW_GRAN)
    loc = loc_ref.at[s, pl.ds(pl.multiple_of(tab_ref[base + 1], ROW_GRAN), n)]
    far = hbm_ref.at[pl.ds(pl.multiple_of(tab_ref[base], ROW_GRAN), n)]
    return pltpu.make_async_copy(loc, far, sem.at[s]) if to_hbm else \
        pltpu.make_async_copy(far, loc, sem.at[s])


def _dispatch_kernel(dst_ref, nct_ref, tail_ref, slot_ref, t_ref, xp_ref, xloc_ref, zero_ref,
                     sem, zsem, *, blk, mc):
    i = pl.program_id(0)
    s = i % 2
    lmax = xloc_ref.shape[1]

    @pl.when(i == 0)
    def _():
        zero_ref[...] = jnp.zeros_like(zero_ref)
        tails = [pl.multiple_of(tail_ref[e], blk) for e in range(N_EXPERTS)]
        for e in range(N_EXPERTS):
            pltpu.make_async_copy(zero_ref, xp_ref.at[pl.ds(tails[e], blk)], zsem).start()
        for e in range(N_EXPERTS):
            pltpu.make_async_copy(zero_ref, xp_ref.at[pl.ds(tails[e], blk)], zsem).wait()

    def compact(r0, r1):
        rows = lax.broadcasted_iota(jnp.int32, (r1 - r0, t_ref.shape[0]), 0) + r0
        hit = rows == slot_ref[0:1, :]
        for k in range(1, TOP_K):
            hit = hit | (rows == slot_ref[k:k + 1, :])
        xloc_ref[s, r0:r1] = _dot(jnp.where(hit, 1.0, 0.0).astype(BF16),
                                  t_ref[...]).astype(BF16)

    base = TOP_K * t_ref.shape[0]
    compact(0, base)
    for r0 in range(base, lmax, TAIL_ROWS):
        @pl.when(nct_ref[i] * ROW_GRAN > r0)
        def _():
            compact(r0, min(r0 + TAIL_ROWS, lmax))

    for e in range(N_EXPERTS):
        _run_copy(xp_ref, dst_ref, xloc_ref, sem, i, e, s, True).start()

    def drain(tile, slot):
        rows_out = nct_ref[tile] * ROW_GRAN
        pltpu.make_async_copy(xloc_ref.at[slot, pl.ds(0, rows_out)],
                              xp_ref.at[pl.ds(0, rows_out)], sem.at[slot]).wait()

    @pl.when(i > 0)
    def _():
        drain(i - 1, 1 - s)

    @pl.when(i == pl.num_programs(0) - 1)
    def _():
        drain(i, s)


def _dispatch(dst_tab, nct, tail, slot, t, cap, blk, lmax, tm):
    n, d = t.shape
    mc = lmax // ROW_GRAN
    return pl.pallas_call(
        functools.partial(_dispatch_kernel, blk=blk, mc=mc),
        grid_spec=pltpu.PrefetchScalarGridSpec(
            num_scalar_prefetch=3,
            grid=(n // tm,),
            in_specs=[pl.BlockSpec((TOP_K, tm), lambda i, *_: (0, i)),
                      pl.BlockSpec((tm, d), lambda i, *_: (i, 0))],
            out_specs=pl.BlockSpec(memory_space=pl.ANY),
            scratch_shapes=[pltpu.VMEM((2, lmax, d), BF16), pltpu.VMEM((blk, d), BF16),
                            pltpu.SemaphoreType.DMA((2,)), pltpu.SemaphoreType.DMA(())]),
        out_shape=jax.ShapeDtypeStruct((cap, d), BF16),
        compiler_params=_params(("arbitrary",)),
        name="moe_dispatch",
    )(dst_tab, nct, tail, slot, t)


def _expert_kernel(blk_e_ref, nused_ref, x_ref, wu_ref, bu_ref, wd_ref, bd_ref, y_ref,
                   wub_ref, wdb_ref):
    i = pl.program_id(0)

    @pl.when((i == 0) | (blk_e_ref[i] != blk_e_ref[jnp.maximum(i - 1, 0)]))
    def _():
        wub_ref[...] = wu_ref[...].astype(BF16)
        wdb_ref[...] = wd_ref[...].astype(BF16)

    @pl.when(i < nused_ref[0])
    def _():
        u = _dot(x_ref[...], wub_ref[...]) + bu_ref[...]
        glu = jnp.minimum(u[:, :D_FF], SWIGLU_LIMIT)
        lin = jnp.clip(u[:, D_FF:], -SWIGLU_LIMIT, SWIGLU_LIMIT)
        act = glu * jax.nn.sigmoid(SWIGLU_ALPHA * glu) * (lin + 1.0)
        y_ref[...] = (_dot(act.astype(BF16), wdb_ref[...]) + bd_ref[...]).astype(y_ref.dtype)


def _experts(blk_e, n_used, x_pad, wu, bu, wd, bd, blk):
    cap = x_pad.shape[0]
    n_blk = cap // blk
    rowmap = lambda i, be, nu: (jnp.minimum(i, nu[0] - 1), 0)
    emap = lambda i, be, nu: (be[i], 0, 0)
    return pl.pallas_call(
        _expert_kernel,
        grid_spec=pltpu.PrefetchScalarGridSpec(
            num_scalar_prefetch=2,
            grid=(n_blk,),
            in_specs=[pl.BlockSpec((blk, D_MODEL), rowmap),
                      pl.BlockSpec((None, D_MODEL, 2 * D_FF), emap),
                      pl.BlockSpec((None, 1, 2 * D_FF), emap),
                      pl.BlockSpec((None, D_FF, D_MODEL), emap),
                      pl.BlockSpec((None, 1, D_MODEL), emap)],
            out_specs=pl.BlockSpec((blk, D_MODEL), rowmap),
            scratch_shapes=[pltpu.VMEM((D_MODEL, 2 * D_FF), BF16),
                            pltpu.VMEM((D_FF, D_MODEL), BF16)]),
        out_shape=jax.ShapeDtypeStruct((cap, D_MODEL), BF16),
        compiler_params=_params(("arbitrary",), vmem=56 * 1024 * 1024),
        name="moe_experts",
    )(blk_e, n_used, x_pad, wu, bu, wd, bd)


def _combine_kernel(dst_ref, nct_ref, slot_ref, gate_ref, x2_ref, g_ref, y_ref, o_ref,
                    yloc_ref, sem, *, mc):
    i = pl.program_id(0)
    s = i % 2
    lmax = yloc_ref.shape[1]

    def issue(tile, slot):
        for e in range(N_EXPERTS):
            _run_copy(y_ref, dst_ref, yloc_ref, sem, tile, e, slot, False).start()

    @pl.when(i == 0)
    def _():
        yloc_ref[...] = jnp.zeros_like(yloc_ref)
        issue(0, 0)

    @pl.when(i + 1 < pl.num_programs(0))
    def _():
        issue(i + 1, 1 - s)

    rows_in = nct_ref[i] * ROW_GRAN
    pltpu.make_async_copy(y_ref.at[pl.ds(0, rows_in)], yloc_ref.at[s, pl.ds(0, rows_in)],
                          sem.at[s]).wait()

    cols = lax.broadcasted_iota(jnp.int32, (x2_ref.shape[0], lmax), 1)
    w = jnp.zeros(cols.shape, F32)
    for k in range(TOP_K):
        w = jnp.where(cols == slot_ref[:, k:k + 1], gate_ref[:, k:k + 1], w)
    acc = x2_ref[...] + _dot(w.astype(BF16), yloc_ref[s])
    o_ref[...] = _rms(acc, g_ref[...])


def _combine(dst_tab, nct, slot_t, gate_t, x2, g, y_pad, lmax, tm):
    n = x2.shape[0]
    mc = lmax // ROW_GRAN
    row = pl.BlockSpec((tm, D_MODEL), lambda i, *_: (i, 0))
    col = pl.BlockSpec((tm, TOP_K), lambda i, *_: (i, 0))
    return pl.pallas_call(
        functools.partial(_combine_kernel, mc=mc),
        grid_spec=pltpu.PrefetchScalarGridSpec(
            num_scalar_prefetch=2,
            grid=(n // tm,),
            in_specs=[col, col, row, pl.BlockSpec(g.shape, lambda i, *_: (0, 0)),
                      pl.BlockSpec(memory_space=pl.ANY)],
            out_specs=row,
            scratch_shapes=[pltpu.VMEM((2, lmax, D_MODEL), BF16),
                            pltpu.SemaphoreType.DMA((2,))]),
        out_shape=jax.ShapeDtypeStruct((n, D_MODEL), F32),
        compiler_params=_params(("arbitrary",)),
        name="moe_combine",
    )(dst_tab, nct, slot_t, gate_t, x2, g, y_pad)


def _rope_inv_freq():
    inv = ROPE_THETA ** (-np.arange(0, ROPE_DIM, 2, dtype=np.float32) / ROPE_DIM)
    lane = np.arange(LANES) % DA_DIM
    tab = np.where(lane < ROPE_DIM, inv.astype(np.float32)[lane % (ROPE_DIM // 2)], 0.0)
    return jnp.asarray(tab.astype(np.float32)[None, :])


def kernel(x, mem, positions, norm_mix_g, w_in, lambda_q1, lambda_k1, lambda_q2, lambda_k2, diff_norm_g, w_alpha2, b_alpha, gla_norm_g, w_out, norm_cross_g, norm_mem_g, w_cq, w_ck, w_cv, w_co, norm_ffn_g, w_router, b_router, w_up, b_up, w_down, b_down, norm_final_g):
    batch, seq, d = x.shape
    mem_len = mem.shape[1]
    n = batch * seq
    moe_blk = 512
    row = lambda a: a.reshape(1, -1)

    x2d = x.reshape(n, d)
    w = w_in[0]
    wqk, wv = w[:, :1024].astype(BF16), w[:, 1024:1536].astype(BF16)
    wg, wga = w[:, 1536:3072].astype(BF16), w[:, 3072:].astype(BF16)
    q, k, v, gq, gk, gv, gr, la = _inproj(
        x2d, positions.reshape(n, 1), _rope_inv_freq(), row(norm_mix_g[0]), wqk, wv, wg, wga,
        w_alpha2[0].astype(BF16), row(b_alpha[0]))

    o_da = _diffattn(q, k, v, row(lambda_q1[0]), row(lambda_k1[0]), row(lambda_q2[0]),
                     row(lambda_k2[0]), row(diff_norm_g[0]), batch, seq)
    o_gla = _gla(gq, gk, gv, gr, la, row(gla_norm_g[0]), batch, seq)

    kc, vc = _memkv(mem.reshape(batch * mem_len, d), row(norm_mem_g[0]),
                    w_ck[0].astype(BF16), w_cv[0].astype(BF16))
    tile = 512
    x2, t, top_e, gate, lrank, counts = _cross(
        o_da, o_gla, x2d, kc, vc, w_out[0].astype(BF16), row(norm_cross_g[0]),
        w_cq[0].astype(BF16), w_co[0].astype(BF16), row(norm_ffn_g[0]), w_router[0].T,
        b_router[0].reshape(-1, 1), batch, seq, mem_len, tile, tq=min(seq, 1024))

    nt = n // tile
    lmax = -(-(TOP_K * tile + N_EXPERTS * ROW_GRAN) // LANES) * LANES
    mc = lmax // ROW_GRAN
    cnt = counts[:, :, 0].astype(jnp.int32)
    plc = jnp.maximum((cnt + ROW_GRAN - 1) // ROW_GRAN * ROW_GRAN, ROW_GRAN)
    lend = jnp.cumsum(plc, axis=1)
    lstart = lend - plc
    tile_off = jnp.cumsum(plc, axis=0) - plc
    etot = jnp.sum(plc, axis=0)
    eblk = (etot + moe_blk - 1) // moe_blk * moe_blk
    gend = jnp.cumsum(eblk)
    dst0 = (gend - eblk)[None, :] + tile_off
    cap = nt * (TOP_K * tile + N_EXPERTS * ROW_GRAN) + N_EXPERTS * (moe_blk - 1)
    cap = -(-cap // moe_blk) * moe_blk
    n_blk = cap // moe_blk
    n_used = (gend[-1] // moe_blk).astype(jnp.int32)
    blk_ids = jnp.minimum(jnp.arange(n_blk, dtype=jnp.int32), n_used - 1)
    blk_e = jnp.minimum(jnp.sum(gend[None, :] <= (blk_ids * moe_blk)[:, None], axis=1),
                        N_EXPERTS - 1).astype(jnp.int32)
    tail = jnp.where(etot > 0, gend, gend[-1]).astype(jnp.int32) - moe_blk
    eids = jnp.arange(N_EXPERTS, dtype=jnp.int32)[:, None, None]
    lstart_tok = jnp.repeat(lstart.T, tile, axis=1)[:, None, :]
    slot = (lrank + jnp.sum(jnp.where(top_e[None] == eids, lstart_tok, 0), axis=0)
            ).astype(jnp.int32)
    dst_tab = jnp.stack([dst0, lstart, plc], axis=-1).astype(jnp.int32).reshape(-1)
    nct = (lend[:, -1] // ROW_GRAN).astype(jnp.int32)

    x_pad = _dispatch(dst_tab, nct, tail, slot, t, cap, moe_blk, lmax, tile)
    y_pad = _experts(blk_e, n_used.reshape(1), x_pad, w_up[0], b_up[0][:, None, :], w_down[0],
                     b_down[0][:, None, :], moe_blk)
    out = _combine(dst_tab, nct, slot.T, gate.T, x2, row(norm_final_g), y_pad, lmax, tile)
    return out.reshape(batch, seq, d)
```

```python
import functools
import math

import numpy as np
import jax
import jax.numpy as jnp
from jax import lax
from jax.experimental import pallas as pl
from jax.experimental.pallas import tpu as pltpu

EPS = 1e-6
D_MODEL = 1024
DA_HEADS = 4
DA_DIM = 64
GLA_HEADS = 4
GLA_DK = 64
GLA_DV = 128
GLA_RANK = 16
GLA_TAU = 16.0
GLA_CHUNK = 64
GLA_GROUP = 4
ROPE_THETA = 500000.0
ROPE_DIM = DA_DIM // 4
X_HEADS = 4
X_DIM = D_MODEL // X_HEADS
N_EXPERTS = 32
TOP_K = 4
D_FF = D_MODEL
SWIGLU_LIMIT = 7.0
SWIGLU_ALPHA = 1.702
LAMBDA_INIT = 0.8 - 0.6 * 1.0
Q_SCALE = DA_DIM ** -0.5 * float(np.log2(np.e))

LANES = 128
ROW_GRAN = 16
TAIL_ROWS = 256
BF16 = jnp.bfloat16
F32 = jnp.float32
VMEM_LIMIT = 48 * 1024 * 1024


def _params(sem, vmem=VMEM_LIMIT):
    return pltpu.CompilerParams(dimension_semantics=sem, vmem_limit_bytes=vmem)


def _rms(xf, g):
    return xf * lax.rsqrt(jnp.mean(xf * xf, axis=-1, keepdims=True) + EPS) * g


def _dot(a, b):
    return jnp.dot(a, b, preferred_element_type=F32)


def _dot_nt(a, b):
    return lax.dot_general(a, b, (((1,), (1,)), ((), ())), preferred_element_type=F32)


def _split3(a):
    hi = a.astype(BF16)
    r1 = a - hi.astype(F32)
    mid = r1.astype(BF16)
    lo = (r1 - mid.astype(F32)).astype(BF16)
    return hi, mid, lo


def _inproj_kernel(x_ref, pos_ref, invf_ref, g_ref, wqk_ref, wv_ref, wg_ref, wga_ref,
                   wa2_ref, ba_ref,
                   q_ref, k_ref, v_ref, gq_ref, gk_ref, gv_ref, gr_ref, la_ref, *, sub):
    groups = [slice(j * sub, (j + 1) * sub) for j in range(x_ref.shape[0] // sub)]
    half = ROPE_DIM // 2
    lane = lax.broadcasted_iota(jnp.int32, (sub, LANES), 1) % DA_DIM
    h = [_rms(x_ref[rs, :], g_ref[...]).astype(BF16) for rs in groups]
    tabs = []
    for rs in groups:
        ang = pos_ref[rs, :].astype(F32) * invf_ref[...]
        sinv = jnp.sin(ang)
        tabs.append((jnp.where(lane < ROPE_DIM, jnp.cos(ang), 1.0),
                     jnp.where(lane < half, -sinv, 0.0),
                     jnp.where((lane >= half) & (lane < ROPE_DIM), sinv, 0.0)))
    proj = [(_dot(a, wqk_ref[...]), _dot(a, wg_ref[...]), _dot(a, wv_ref[...]),
             _dot(a, wga_ref[...])) for a in h]
    for rs, (qk, gg, pv, ga), (cosv, s_lo, s_hi) in zip(groups, proj, tabs):
        n_grp = qk.shape[1] // LANES
        for j in range(n_grp):
            t = qk[:, j * LANES:(j + 1) * LANES]
            rot = (t * cosv + pltpu.roll(t, LANES - half, 1) * s_lo
                   + pltpu.roll(t, half, 1) * s_hi)
            if j < n_grp // 2:
                q_ref[rs, j * LANES:(j + 1) * LANES] = (rot * Q_SCALE).astype(BF16)
            else:
                jj = j - n_grp // 2
                k_ref[rs, jj * LANES:(jj + 1) * LANES] = rot.astype(BF16)
        v_ref[rs, :] = pv.astype(BF16)
        gq_ref[rs, :] = gg[:, 0:256].astype(BF16)
        gk_ref[rs, :] = gg[:, 256:512].astype(BF16)
        gv_ref[rs, :] = gg[:, 512:1024].astype(BF16)
        gr_ref[rs, :] = gg[:, 1024:1536].astype(BF16)
        z = _dot(ga.astype(BF16), wa2_ref[...]) + ba_ref[...]
        la_ref[rs, :] = (jnp.minimum(z, 0.0) - jnp.log1p(jnp.exp(-jnp.abs(z)))) * (1.0 / GLA_TAU)


def _inproj(x2d, pos2d, invf, g, wqk, wv, wg, wga, wa2, ba, tm=1024, sub=512):
    n = x2d.shape[0]
    tm, sub = min(tm, n), min(sub, n)
    row = lambda w: pl.BlockSpec((tm, w), lambda i: (i, 0))
    full = lambda a: pl.BlockSpec(a.shape, lambda i: (0,) * a.ndim)
    outs = [(512, BF16), (512, BF16), (512, BF16), (256, BF16), (256, BF16), (512, BF16),
            (512, BF16), (256, F32)]
    return pl.pallas_call(
        functools.partial(_inproj_kernel, sub=sub),
        grid=(n // tm,),
        in_specs=[row(D_MODEL), row(1), full(invf), full(g), full(wqk), full(wv), full(wg),
                  full(wga), full(wa2), full(ba)],
        out_specs=[row(w) for w, _ in outs],
        out_shape=[jax.ShapeDtypeStruct((n, w), dt) for w, dt in outs],
        compiler_params=_params(("arbitrary",)),
        name="inproj",
    )(x2d, pos2d, invf, g, wqk, wv, wg, wga, wa2, ba)


def _diffattn_kernel(lq1_ref, lk1_ref, lq2_ref, lk2_ref, gn_ref, q_ref, k_ref, v_ref, o_ref,
                     *, tq):
    seq = q_ref.shape[0]
    lam = (jnp.exp(jnp.sum(lq1_ref[...] * lk1_ref[...], axis=-1, keepdims=True))
           - jnp.exp(jnp.sum(lq2_ref[...] * lk2_ref[...], axis=-1, keepdims=True))
           + LAMBDA_INIT)
    lane = lax.broadcasted_iota(jnp.int32, (tq, LANES), 1)
    r = lax.broadcasted_iota(jnp.int32, (2 * tq, tq), 0) % tq
    c = lax.broadcasted_iota(jnp.int32, (2 * tq, tq), 1)
    causal = c <= r

    def scores(qi):
        q = q_ref[qi * tq:(qi + 1) * tq, :]
        zero = jnp.zeros_like(q)
        qs = jnp.concatenate([jnp.where(lane < DA_DIM, q, zero),
                              jnp.where(lane >= DA_DIM, q, zero)], axis=0)
        past = qi * tq
        s_diag = jnp.where(causal, _dot_nt(qs, k_ref[past:past + tq, :]), -jnp.inf)
        if qi == 0:
            return s_diag
        return jnp.concatenate([_dot_nt(qs, k_ref[0:past, :]), s_diag], axis=1)

    nq = seq // tq
    s_next = scores(0)
    for qi in range(nq):
        s = s_next
        if qi + 1 < nq:
            s_next = scores(qi + 1)
        p = jnp.exp2(s - jnp.max(s, axis=-1, keepdims=True))
        l = jnp.sum(p, axis=-1, keepdims=True)
        a = p[0:tq] - p[tq:2 * tq] * (lam * l[0:tq] / l[tq:2 * tq])
        o = _dot(a.astype(BF16), v_ref[0:(qi + 1) * tq, :]) / l[0:tq]
        o_ref[qi * tq:(qi + 1) * tq, :] = (
            _rms(o, gn_ref[...]) * (1.0 - LAMBDA_INIT)).astype(o_ref.dtype)


def _diffattn(q, k, v, lq1, lk1, lq2, lk2, gn, batch, seq, tq=256):
    vec = lambda a: pl.BlockSpec(a.shape, lambda b, h: (0, 0))
    blk = pl.BlockSpec((seq, LANES), lambda b, h: (b, h))
    return pl.pallas_call(
        functools.partial(_diffattn_kernel, tq=tq),
        grid=(batch, DA_HEADS),
        in_specs=[vec(lq1), vec(lk1), vec(lq2), vec(lk2), vec(gn), blk, blk, blk],
        out_specs=blk,
        out_shape=jax.ShapeDtypeStruct(q.shape, BF16),
        compiler_params=_params(("arbitrary", "arbitrary")),
        name="diffattn",
    )(lq1, lk1, lq2, lk2, gn, q, k, v)


def _gla_kernel(gq_ref, gk_ref, gv_ref, gr_ref, la_ref, gn_ref, o_ref, st_ref, *, rows):
    c = GLA_CHUNK
    nch = rows // c
    kw = GLA_HEADS * GLA_DK
    pw = 2 * GLA_DK
    vw = 2 * GLA_DV

    @pl.when(pl.program_id(1) == 0)
    def _():
        st_ref[...] = jnp.zeros_like(st_ref)

    r_i = lax.broadcasted_iota(jnp.int32, (rows, rows), 0)
    c_i = lax.broadcasted_iota(jnp.int32, (rows, rows), 1)
    causal = (r_i // c == c_i // c) & (c_i <= r_i)
    tril = causal.astype(BF16)
    sr = lax.broadcasted_iota(jnp.int32, (pw, vw), 0) // GLA_DK
    sc = lax.broadcasted_iota(jnp.int32, (pw, vw), 1) // GLA_DV
    blockdiag = sr == sc
    head_of_lane = lax.broadcasted_iota(jnp.int32, (rows, kw), 1) // GLA_DK

    grp = range(o_ref.shape[0])
    b, qe, kn, kd, b_last, v = [], [], [], [], [], []
    for bb in grp:
        hi, mid, lo = _split3(la_ref[bb])
        b.append(_dot(tril, hi) + _dot(tril, mid) + _dot(tril, lo))
    for bb in grp:
        b3 = b[bb].reshape(nch, c, kw)
        b_last.append(b3[:, c - 1:c, :])
        qe.append(gq_ref[bb].astype(F32) * jnp.exp(b[bb]) * (GLA_DK ** -0.5))
        gk = gk_ref[bb].astype(F32)
        kn.append((gk * jnp.exp(-b[bb])).astype(BF16))
        kd.append(gk.reshape(nch, c, kw) * jnp.exp(b_last[bb] - b3))
        v.append(gv_ref[bb])
    attn = [[None] * GLA_HEADS for _ in grp]
    for h in range(GLA_HEADS):
        for bb in grp:
            qh = jnp.where(head_of_lane == h, qe[bb], 0.0).astype(BF16)
            attn[bb][h] = jnp.where(causal, _dot_nt(qh, kn[bb]), 0.0).astype(BF16)
    intra = [[None] * GLA_HEADS for _ in grp]
    for h in range(GLA_HEADS):
        for bb in grp:
            intra[bb][h] = _dot(attn[bb][h], v[bb][:, h * GLA_DV:(h + 1) * GLA_DV])

    pairs = [(bb, p) for p in range(GLA_HEADS // 2) for bb in grp]
    qeb = [q_.astype(BF16) for q_ in qe]
    upd, decay = {}, {}
    for bb, p in pairs:
        ks = slice(p * pw, (p + 1) * pw)
        vs = slice(p * vw, (p + 1) * vw)
        for ci in range(nch):
            rs = slice(ci * c, (ci + 1) * c)
            u = _dot(kd[bb][ci][:, ks].T.astype(BF16), v[bb][rs, vs])
            upd[bb, p, ci] = jnp.where(blockdiag, u, 0.0)
            decay[bb, p, ci] = jnp.exp(b_last[bb][ci][:, ks]).T
    inter = {}
    for bb, p in pairs:
        ks = slice(p * pw, (p + 1) * pw)
        st = st_ref[bb, p]
        parts = []
        for ci in range(nch):
            rs = slice(ci * c, (ci + 1) * c)
            parts.append(_dot(qeb[bb][rs, ks], st.astype(BF16)))
            st = st * decay[bb, p, ci] + upd[bb, p, ci]
        st_ref[bb, p] = st
        inter[bb, p] = jnp.concatenate(parts, axis=0)

    for bb in grp:
        for h in range(GLA_HEADS):
            hs = slice(h * GLA_DV, (h + 1) * GLA_DV)
            o = inter[bb, h // 2][:, (h % 2) * GLA_DV:(h % 2 + 1) * GLA_DV] + intra[bb][h]
            gr = gr_ref[bb, :, hs].astype(F32)
            y = _rms(o, gn_ref[...]) * (gr * jax.nn.sigmoid(gr))
            o_ref[bb, :, hs] = y.astype(o_ref.dtype)


def _gla(gq, gk, gv, gr, la, gn, batch, seq, rows=256):
    nb = seq // rows
    grp = math.gcd(batch, GLA_GROUP)
    seqs = lambda a: a.reshape(batch, seq, a.shape[-1])
    blk = lambda w: pl.BlockSpec((grp, rows, w), lambda g, i: (g, i, 0))
    out = pl.pallas_call(
        functools.partial(_gla_kernel, rows=rows),
        grid=(batch // grp, nb),
        in_specs=[blk(256), blk(256), blk(512), blk(512), blk(256),
                  pl.BlockSpec(gn.shape, lambda g, i: (0, 0))],
        out_specs=blk(512),
        out_shape=jax.ShapeDtypeStruct((batch, seq, gv.shape[-1]), BF16),
        scratch_shapes=[pltpu.VMEM((grp, GLA_HEADS // 2, 2 * GLA_DK, 2 * GLA_DV), F32)],
        compiler_params=_params(("arbitrary", "arbitrary")),
        name="gla",
    )(seqs(gq), seqs(gk), seqs(gv), seqs(gr), seqs(la), gn)
    return out.reshape(gv.shape)


def _memkv_kernel(m_ref, g_ref, wk_ref, wv_ref, k_ref, v_ref):
    hm = _rms(m_ref[...], g_ref[...]).astype(BF16)
    k_ref[...] = _dot(hm, wk_ref[...]).astype(BF16)
    v_ref[...] = _dot(hm, wv_ref[...]).astype(BF16)


def _memkv(mem2d, g, wk, wv, tm=256):
    n = mem2d.shape[0]
    row = pl.BlockSpec((tm, D_MODEL), lambda i: (i, 0))
    full = lambda a: pl.BlockSpec(a.shape, lambda i: (0,) * a.ndim)
    return pl.pallas_call(
        _memkv_kernel,
        grid=(n // tm,),
        in_specs=[row, full(g), full(wk), full(wv)],
        out_specs=[row, row],
        out_shape=[jax.ShapeDtypeStruct((n, D_MODEL), BF16)] * 2,
        compiler_params=_params(("arbitrary",)),
        name="memkv",
    )(mem2d, g, wk, wv)


def _cross_kernel(oda_ref, ogla_ref, x_ref, kc_ref, vc_ref, wo_ref, gq_ref, wcq_ref, wco_ref,
                  g_ref, wrt_ref, br_ref,
                  x2_ref, t_ref, e_ref, gate_ref, rank_ref, cnt_ref, *, tile):
    subs = [slice(j * tile, (j + 1) * tile) for j in range(x_ref.shape[0] // tile)]
    half = oda_ref.shape[1]
    x1 = [x_ref[cs, :] + (_dot(oda_ref[cs, :], wo_ref[0:half, :])
                          + _dot(ogla_ref[cs, :], wo_ref[half:, :])) for cs in subs]
    hq = [_rms(a, gq_ref[...]).astype(BF16) for a in x1]
    qc = [(_dot(a, wcq_ref[...]) * (X_DIM ** -0.5)).astype(BF16) for a in hq]
    heads = [slice(h * X_DIM, (h + 1) * X_DIM) for h in range(X_HEADS)]
    s = [[_dot_nt(q[:, hs], kc_ref[:, hs]) for hs in heads] for q in qc]
    p = [[jnp.exp(a - jnp.max(a, axis=-1, keepdims=True)) for a in row] for row in s]
    o = [jnp.concatenate(
        [(_dot(a.astype(BF16), vc_ref[:, hs]) / jnp.sum(a, axis=-1, keepdims=True)).astype(BF16)
         for a, hs in zip(row, heads)], axis=1) for row in p]
    x2 = [a + _dot(b, wco_ref[...]) for a, b in zip(x1, o)]
    t = [_rms(a, g_ref[...]) for a in x2]
    for cs, a, b in zip(subs, x2, t):
        x2_ref[cs, :] = a
        t_ref[cs, :] = b.astype(t_ref.dtype)

    w_hi, w_mid, _ = _split3(wrt_ref[...])
    ur = lax.broadcasted_iota(jnp.int32, (tile, tile), 0)
    uc = lax.broadcasted_iota(jnp.int32, (tile, tile), 1)
    before = (ur < uc).astype(BF16)
    iota_e = lax.broadcasted_iota(jnp.int32, (N_EXPERTS, tile), 0)
    logit = []
    for a in t:
        t_hi, t_mid, _ = _split3(a)
        logit.append((_dot_nt(w_hi, t_hi) + _dot_nt(w_hi, t_mid) + _dot_nt(w_mid, t_hi))
                     + br_ref[...])
    vals, idxs, sels = [[] for _ in subs], [[] for _ in subs], [[] for _ in subs]
    for _ in range(TOP_K):
        for j in range(len(subs)):
            mx = jnp.max(logit[j], axis=0, keepdims=True)
            idx = jnp.min(jnp.where(logit[j] == mx, iota_e, N_EXPERTS), axis=0, keepdims=True)
            sel = iota_e == idx
            vals[j].append(mx)
            idxs[j].append(idx)
            sels[j].append(sel)
            logit[j] = jnp.where(sel, -jnp.inf, logit[j])
    for j, cs in enumerate(subs):
        ex = [jnp.exp(v - vals[j][0]) for v in vals[j]]
        den = ex[0] + ex[1] + ex[2] + ex[3]
        gate_ref[:, cs] = jnp.concatenate([e / den for e in ex], axis=0)
        e_ref[:, cs] = jnp.concatenate(idxs[j], axis=0)
        onehot = (sels[j][0] | sels[j][1] | sels[j][2] | sels[j][3])
        base = _dot(onehot.astype(BF16), before)
        rank_ref[:, cs] = jnp.concatenate(
            [jnp.sum(jnp.where(s_, base, 0.0), axis=0, keepdims=True) for s_ in sels[j]],
            axis=0).astype(jnp.int32)
        cnt_ref[j] = jnp.sum(onehot.astype(F32), axis=1, keepdims=True)


def _cross(oda, ogla, x2d, kc, vc, wo, gq, wcq, wco, g, wrt, br, batch, seq, mem_len, tile, tq):
    nq = seq // tq
    n = batch * seq
    row = lambda w: pl.BlockSpec((tq, w), lambda b, i: (b * nq + i, 0))
    col = lambda r: pl.BlockSpec((r, tq), lambda b, i: (0, b * nq + i))
    full = lambda a: pl.BlockSpec(a.shape, lambda b, i: (0,) * a.ndim)
    memb = pl.BlockSpec((mem_len, D_MODEL), lambda b, i: (b, 0))
    return pl.pallas_call(
        functools.partial(_cross_kernel, tile=tile),
        grid=(batch, nq),
        in_specs=[row(oda.shape[1]), row(ogla.shape[1]), row(D_MODEL), memb, memb, full(wo),
                  full(gq), full(wcq), full(wco), full(g), full(wrt), full(br)],
        out_specs=[row(D_MODEL), row(D_MODEL), col(TOP_K), col(TOP_K), col(TOP_K),
                   pl.BlockSpec((tq // tile, N_EXPERTS, 1), lambda b, i: (b * nq + i, 0, 0))],
        out_shape=[jax.ShapeDtypeStruct((n, D_MODEL), F32),
                   jax.ShapeDtypeStruct((n, D_MODEL), BF16),
                   jax.ShapeDtypeStruct((TOP_K, n), jnp.int32),
                   jax.ShapeDtypeStruct((TOP_K, n), F32),
                   jax.ShapeDtypeStruct((TOP_K, n), jnp.int32),
                   jax.ShapeDtypeStruct((n // tile, N_EXPERTS, 1), F32)],
        compiler_params=_params(("arbitrary", "arbitrary"), vmem=56 * 1024 * 1024),
        name="mix_cross_router",
    )(oda, ogla, x2d, kc, vc, wo, gq, wcq, wco, g, wrt, br)


def _run_copy(hbm_ref, tab_ref, loc_ref, sem, tile, e, s, to_hbm):
    base = (tile * N_EXPERTS + e) * 3
    n = pl.multiple_of(tab_ref[base + 2], ROW_GRAN)
    loc = loc_ref.at[s, pl.ds(pl.multiple_of(tab_ref[base + 1], ROW_GRAN), n)]
    far = hbm_ref.at[pl.ds(pl.multiple_of(tab_ref[base], ROW_GRAN), n)]
    return pltpu.make_async_copy(loc, far, sem.at[s]) if to_hbm else \
        pltpu.make_async_copy(far, loc, sem.at[s])


def _dispatch_kernel(dst_ref, nct_ref, tail_ref, slot_ref, t_ref, xp_ref, xloc_ref, zero_ref,
                     sem, zsem, usem, *, blk, mc):
    i = pl.program_id(0)
    s = i % 2
    lmax = xloc_ref.shape[1]
    n_blk = xp_ref.shape[0] // blk
    n_used = tail_ref[N_EXPERTS]

    def zero_block(j, semaphore):
        dst = xp_ref.at[pl.ds(pl.multiple_of(j * blk, blk), blk)]
        return pltpu.make_async_copy(zero_ref, dst, semaphore)

    @pl.when(i == 0)
    def _():
        zero_ref[...] = jnp.zeros_like(zero_ref)
        tails = [pl.multiple_of(tail_ref[e], blk) for e in range(N_EXPERTS)]
        for e in range(N_EXPERTS):
            pltpu.make_async_copy(zero_ref, xp_ref.at[pl.ds(tails[e], blk)], zsem).start()
        for e in range(N_EXPERTS):
            pltpu.make_async_copy(zero_ref, xp_ref.at[pl.ds(tails[e], blk)], zsem).wait()

        def start_unused(j, carry):
            zero_block(j, usem).start()
            return carry
        lax.fori_loop(n_used, n_blk, start_unused, 0)

    def compact(r0, r1):
        rows = lax.broadcasted_iota(jnp.int32, (r1 - r0, t_ref.shape[0]), 0) + r0
        hit = rows == slot_ref[0:1, :]
        for k in range(1, TOP_K):
            hit = hit | (rows == slot_ref[k:k + 1, :])
        xloc_ref[s, r0:r1] = _dot(jnp.where(hit, 1.0, 0.0).astype(BF16),
                                  t_ref[...]).astype(BF16)

    base = TOP_K * t_ref.shape[0]
    compact(0, base)
    for r0 in range(base, lmax, TAIL_ROWS):
        @pl.when(nct_ref[i] * ROW_GRAN > r0)
        def _():
            compact(r0, min(r0 + TAIL_ROWS, lmax))

    for e in range(N_EXPERTS):
        _run_copy(xp_ref, dst_ref, xloc_ref, sem, i, e, s, True).start()

    def drain(tile, slot):
        rows_out = nct_ref[tile] * ROW_GRAN
        pltpu.make_async_copy(xloc_ref.at[slot, pl.ds(0, rows_out)],
                              xp_ref.at[pl.ds(0, rows_out)], sem.at[slot]).wait()

    @pl.when(i > 0)
    def _():
        drain(i - 1, 1 - s)

    @pl.when(i == pl.num_programs(0) - 1)
    def _():
        drain(i, s)

        def wait_unused(j, carry):
            zero_block(j, usem).wait()
            return carry
        lax.fori_loop(n_used, n_blk, wait_unused, 0)


def _dispatch(dst_tab, nct, tail, slot, t, cap, blk, lmax, tm):
    n, d = t.shape
    mc = lmax // ROW_GRAN
    return pl.pallas_call(
        functools.partial(_dispatch_kernel, blk=blk, mc=mc),
        grid_spec=pltpu.PrefetchScalarGridSpec(
            num_scalar_prefetch=3,
            grid=(n // tm,),
            in_specs=[pl.BlockSpec((TOP_K, tm), lambda i, *_: (0, i)),
                      pl.BlockSpec((tm, d), lambda i, *_: (i, 0))],
            out_specs=pl.BlockSpec(memory_space=pl.ANY),
            scratch_shapes=[pltpu.VMEM((2, lmax, d), BF16), pltpu.VMEM((blk, d), BF16),
                            pltpu.SemaphoreType.DMA((2,)), pltpu.SemaphoreType.DMA(()),
                            pltpu.SemaphoreType.DMA(())]),
        out_shape=jax.ShapeDtypeStruct((cap, d), BF16),
        compiler_params=_params(("arbitrary",)),
        name="moe_dispatch",
    )(dst_tab, nct, tail, slot, t)


def _expert_kernel(blk_e_ref, nused_ref, x_ref, wu_ref, bu_ref, wd_ref, bd_ref, y_ref,
                   wub_ref, wdb_ref):
    i = pl.program_id(0)

    @pl.when((i == 0) | (blk_e_ref[i] != blk_e_ref[jnp.maximum(i - 1, 0)]))
    def _():
        wub_ref[...] = wu_ref[...].astype(BF16)
        wdb_ref[...] = wd_ref[...].astype(BF16)

    @pl.when(i < nused_ref[0])
    def _():
        u = _dot(x_ref[...], wub_ref[...]) + bu_ref[...]
        glu = jnp.minimum(u[:, :D_FF], SWIGLU_LIMIT)
        lin = jnp.clip(u[:, D_FF:], -SWIGLU_LIMIT, SWIGLU_LIMIT)
        act = glu * jax.nn.sigmoid(SWIGLU_ALPHA * glu) * (lin + 1.0)
        y_ref[...] = (_dot(act.astype(BF16), wdb_ref[...]) + bd_ref[...]).astype(y_ref.dtype)

    @pl.when(i >= nused_ref[0])
    def _():
        y_ref[...] = jnp.zeros_like(y_ref)


def _experts(blk_e, n_used, x_pad, wu, bu, wd, bd, blk):
    cap = x_pad.shape[0]
    n_blk = cap // blk
    rowmap = lambda i, be, nu: (jnp.minimum(i, nu[0] - 1), 0)
    emap = lambda i, be, nu: (be[i], 0, 0)
    return pl.pallas_call(
        _expert_kernel,
        grid_spec=pltpu.PrefetchScalarGridSpec(
            num_scalar_prefetch=2,
            grid=(n_blk,),
            in_specs=[pl.BlockSpec((blk, D_MODEL), rowmap),
                      pl.BlockSpec((None, D_MODEL, 2 * D_FF), emap),
                      pl.BlockSpec((None, 1, 2 * D_FF), emap),
                      pl.BlockSpec((None, D_FF, D_MODEL), emap),
                      pl.BlockSpec((None, 1, D_MODEL), emap)],
            out_specs=pl.BlockSpec((blk, D_MODEL), lambda i, be, nu: (i, 0)),
            scratch_shapes=[pltpu.VMEM((D_MODEL, 2 * D_FF), BF16),
                            pltpu.VMEM((D_FF, D_MODEL), BF16)]),
        out_shape=jax.ShapeDtypeStruct((cap, D_MODEL), BF16),
        compiler_params=_params(("arbitrary",), vmem=56 * 1024 * 1024),
        name="moe_experts",
    )(blk_e, n_used, x_pad, wu, bu, wd, bd)


def _combine_kernel(dst_ref, nct_ref, slot_ref, gate_ref, x2_ref, g_ref, y_ref, o_ref,
                    yloc_ref, sem, *, mc):
    i = pl.program_id(0)
    s = i % 2
    lmax = yloc_ref.shape[1]

    def issue(tile, slot):
        for e in range(N_EXPERTS):
            _run_copy(y_ref, dst_ref, yloc_ref, sem, tile, e, slot, False).start()

    @pl.when(i == 0)
    def _():
        yloc_ref[...] = jnp.zeros_like(yloc_ref)
        issue(0, 0)

    @pl.when(i + 1 < pl.num_programs(0))
    def _():
        issue(i + 1, 1 - s)

    rows_in = nct_ref[i] * ROW_GRAN
    pltpu.make_async_copy(y_ref.at[pl.ds(0, rows_in)], yloc_ref.at[s, pl.ds(0, rows_in)],
                          sem.at[s]).wait()

    cols = lax.broadcasted_iota(jnp.int32, (x2_ref.shape[0], lmax), 1)
    w = jnp.zeros(cols.shape, F32)
    for k in range(TOP_K):
        w = jnp.where(cols == slot_ref[:, k:k + 1], gate_ref[:, k:k + 1], w)
    acc = x2_ref[...] + _dot(w.astype(BF16), yloc_ref[s])
    o_ref[...] = _rms(acc, g_ref[...])


def _combine(dst_tab, nct, slot_t, gate_t, x2, g, y_pad, lmax, tm):
    n = x2.shape[0]
    mc = lmax // ROW_GRAN
    row = pl.BlockSpec((tm, D_MODEL), lambda i, *_: (i, 0))
    col = pl.BlockSpec((tm, TOP_K), lambda i, *_: (i, 0))
    return pl.pallas_call(
        functools.partial(_combine_kernel, mc=mc),
        grid_spec=pltpu.PrefetchScalarGridSpec(
            num_scalar_prefetch=2,
            grid=(n // tm,),
            in_specs=[col, col, row, pl.BlockSpec(g.shape, lambda i, *_: (0, 0)),
                      pl.BlockSpec(memory_space=pl.ANY)],
            out_specs=row,
            scratch_shapes=[pltpu.VMEM((2, lmax, D_MODEL), BF16),
                            pltpu.SemaphoreType.DMA((2,))]),
        out_shape=jax.ShapeDtypeStruct((n, D_MODEL), F32),
        compiler_params=_params(("arbitrary",)),
        name="moe_combine",
    )(dst_tab, nct, slot_t, gate_t, x2, g, y_pad)


def _rope_inv_freq():
    inv = ROPE_THETA ** (-np.arange(0, ROPE_DIM, 2, dtype=np.float32) / ROPE_DIM)
    lane = np.arange(LANES) % DA_DIM
    tab = np.where(lane < ROPE_DIM, inv.astype(np.float32)[lane % (ROPE_DIM // 2)], 0.0)
    return jnp.asarray(tab.astype(np.float32)[None, :])


def kernel(x, mem, positions, norm_mix_g, w_in, lambda_q1, lambda_k1, lambda_q2, lambda_k2, diff_norm_g, w_alpha2, b_alpha, gla_norm_g, w_out, norm_cross_g, norm_mem_g, w_cq, w_ck, w_cv, w_co, norm_ffn_g, w_router, b_router, w_up, b_up, w_down, b_down, norm_final_g):
    batch, seq, d = x.shape
    mem_len = mem.shape[1]
    n = batch * seq
    moe_blk = 512
    row = lambda a: a.reshape(1, -1)

    x2d = x.reshape(n, d)
    w = w_in[0]
    wqk, wv = w[:, :1024].astype(BF16), w[:, 1024:1536].astype(BF16)
    wg, wga = w[:, 1536:3072].astype(BF16), w[:, 3072:].astype(BF16)
    q, k, v, gq, gk, gv, gr, la = _inproj(
        x2d, positions.reshape(n, 1), _rope_inv_freq(), row(norm_mix_g[0]), wqk, wv, wg, wga,
        w_alpha2[0].astype(BF16), row(b_alpha[0]))

    o_da = _diffattn(q, k, v, row(lambda_q1[0]), row(lambda_k1[0]), row(lambda_q2[0]),
                     row(lambda_k2[0]), row(diff_norm_g[0]), batch, seq)
    o_gla = _gla(gq, gk, gv, gr, la, row(gla_norm_g[0]), batch, seq)

    kc, vc = _memkv(mem.reshape(batch * mem_len, d), row(norm_mem_g[0]),
                    w_ck[0].astype(BF16), w_cv[0].astype(BF16))
    tile = 512
    x2, t, top_e, gate, lrank, counts = _cross(
        o_da, o_gla, x2d, kc, vc, w_out[0].astype(BF16), row(norm_cross_g[0]),
        w_cq[0].astype(BF16), w_co[0].astype(BF16), row(norm_ffn_g[0]), w_router[0].T,
        b_router[0].reshape(-1, 1), batch, seq, mem_len, tile, tq=min(seq, 1024))

    nt = n // tile
    lmax = -(-(TOP_K * tile + N_EXPERTS * ROW_GRAN) // LANES) * LANES
    mc = lmax // ROW_GRAN
    cnt = counts[:, :, 0].astype(jnp.int32)
    plc = jnp.maximum((cnt + ROW_GRAN - 1) // ROW_GRAN * ROW_GRAN, ROW_GRAN)
    lend = jnp.cumsum(plc, axis=1)
    lstart = lend - plc
    tile_off = jnp.cumsum(plc, axis=0) - plc
    etot = jnp.sum(plc, axis=0)
    eblk = (etot + moe_blk - 1) // moe_blk * moe_blk
    gend = jnp.cumsum(eblk)
    dst0 = (gend - eblk)[None, :] + tile_off
    cap = nt * (TOP_K * tile + N_EXPERTS * ROW_GRAN) + N_EXPERTS * (moe_blk - 1)
    cap = -(-cap // moe_blk) * moe_blk
    n_blk = cap // moe_blk
    n_used = (gend[-1] // moe_blk).astype(jnp.int32)
    blk_ids = jnp.minimum(jnp.arange(n_blk, dtype=jnp.int32), n_used - 1)
    blk_e = jnp.minimum(jnp.sum(gend[None, :] <= (blk_ids * moe_blk)[:, None], axis=1),
                        N_EXPERTS - 1).astype(jnp.int32)
    tail = jnp.where(etot > 0, gend, gend[-1]).astype(jnp.int32) - moe_blk
    tail = jnp.concatenate([tail, n_used.reshape(1)])
    eids = jnp.arange(N_EXPERTS, dtype=jnp.int32)[:, None, None]
    lstart_tok = jnp.repeat(lstart.T, tile, axis=1)[:, None, :]
    slot = (lrank + jnp.sum(jnp.where(top_e[None] == eids, lstart_tok, 0), axis=0)
            ).astype(jnp.int32)
    dst_tab = jnp.stack([dst0, lstart, plc], axis=-1).astype(jnp.int32).reshape(-1)
    nct = (lend[:, -1] // ROW_GRAN).astype(jnp.int32)

    x_pad = _dispatch(dst_tab, nct, tail, slot, t, cap, moe_blk, lmax, tile)
    y_pad = _experts(blk_e, n_used.reshape(1), x_pad, w_up[0], b_up[0][:, None, :], w_down[0],
                     b_down[0][:, None, :], moe_blk)
    out = _combine(dst_tab, nct, slot.T, gate.T, x2, row(norm_final_g), y_pad, lmax, tile)
    return out.reshape(batch, seq, d)
```

```python
import functools
import math

import numpy as np
import jax
import jax.numpy as jnp
from jax import lax
from jax.experimental import pallas as pl
from jax.experimental.pallas import tpu as pltpu

EPS = 1e-6
D_MODEL = 1024
DA_HEADS = 4
DA_DIM = 64
GLA_HEADS = 4
GLA_DK = 64
GLA_DV = 128
GLA_RANK = 16
GLA_TAU = 16.0
GLA_CHUNK = 64
GLA_GROUP = 4
ROPE_THETA = 500000.0
ROPE_DIM = DA_DIM // 4
X_HEADS = 4
X_DIM = D_MODEL // X_HEADS
N_EXPERTS = 32
TOP_K = 4
D_FF = D_MODEL
SWIGLU_LIMIT = 7.0
SWIGLU_ALPHA = 1.702
LAMBDA_INIT = 0.8 - 0.6 * 1.0
Q_SCALE = DA_DIM ** -0.5 * float(np.log2(np.e))

LANES = 128
ROW_GRAN = 16
TAIL_ROWS = 256
BF16 = jnp.bfloat16
F32 = jnp.float32
VMEM_LIMIT = 48 * 1024 * 1024
VMEM_LIMIT_WIDE = 56 * 1024 * 1024

DA_QK = DA_HEADS * 2 * DA_DIM
DA_V = DA_HEADS * 2 * DA_DIM
G_QK = GLA_HEADS * GLA_DK
G_V = GLA_HEADS * GLA_DV

ROW_STEP = 1024
ROW_GROUP = 512
MOE_BLOCK = 512
SEQ_TILE = 256


def _params(sem, vmem=VMEM_LIMIT):
    return pltpu.CompilerParams(dimension_semantics=sem, vmem_limit_bytes=vmem)


def _rms(xf, g):
    return xf * lax.rsqrt(jnp.mean(xf * xf, axis=-1, keepdims=True) + EPS) * g


def _dot(a, b):
    return jnp.dot(a, b, preferred_element_type=F32)


def _dot_nt(a, b):
    return lax.dot_general(a, b, (((1,), (1,)), ((), ())), preferred_element_type=F32)


def _split3(a):
    hi = a.astype(BF16)
    r1 = a - hi.astype(F32)
    mid = r1.astype(BF16)
    lo = (r1 - mid.astype(F32)).astype(BF16)
    return hi, mid, lo


def _inproj_kernel(x_ref, pos_ref, invf_ref, g_ref, wqk_ref, wv_ref, wg_ref, wga_ref,
                   wa2_ref, ba_ref,
                   q_ref, k_ref, v_ref, gq_ref, gk_ref, gv_ref, gr_ref, la_ref, *, sub):
    groups = [slice(j * sub, (j + 1) * sub) for j in range(x_ref.shape[0] // sub)]
    half = ROPE_DIM // 2
    lane = lax.broadcasted_iota(jnp.int32, (sub, LANES), 1) % DA_DIM
    h = [_rms(x_ref[rs, :], g_ref[...]).astype(BF16) for rs in groups]
    tabs = []
    for rs in groups:
        ang = pos_ref[rs, :].astype(F32) * invf_ref[...]
        sinv = jnp.sin(ang)
        tabs.append((jnp.where(lane < ROPE_DIM, jnp.cos(ang), 1.0),
                     jnp.where(lane < half, -sinv, 0.0),
                     jnp.where((lane >= half) & (lane < ROPE_DIM), sinv, 0.0)))
    proj = [(_dot(a, wqk_ref[...]), _dot(a, wg_ref[...]), _dot(a, wv_ref[...]),
             _dot(a, wga_ref[...])) for a in h]
    for rs, (qk, gg, pv, ga), (cosv, s_lo, s_hi) in zip(groups, proj, tabs):
        n_grp = qk.shape[1] // LANES
        for j in range(n_grp):
            t = qk[:, j * LANES:(j + 1) * LANES]
            rot = (t * cosv + pltpu.roll(t, LANES - half, 1) * s_lo
                   + pltpu.roll(t, half, 1) * s_hi)
            if j < n_grp // 2:
                q_ref[rs, j * LANES:(j + 1) * LANES] = (rot * Q_SCALE).astype(BF16)
            else:
                jj = j - n_grp // 2
                k_ref[rs, jj * LANES:(jj + 1) * LANES] = rot.astype(BF16)
        v_ref[rs, :] = pv.astype(BF16)
        gq_ref[rs, :] = gg[:, 0:G_QK].astype(BF16)
        gk_ref[rs, :] = gg[:, G_QK:2 * G_QK].astype(BF16)
        gv_ref[rs, :] = gg[:, 2 * G_QK:2 * G_QK + G_V].astype(BF16)
        gr_ref[rs, :] = gg[:, 2 * G_QK + G_V:2 * G_QK + 2 * G_V].astype(BF16)
        z = _dot(ga.astype(BF16), wa2_ref[...]) + ba_ref[...]
        la_ref[rs, :] = (jnp.minimum(z, 0.0) - jnp.log1p(jnp.exp(-jnp.abs(z)))) * (1.0 / GLA_TAU)


def _inproj(x2d, pos2d, invf, g, wqk, wv, wg, wga, wa2, ba):
    n = x2d.shape[0]
    tm, sub = min(ROW_STEP, n), min(ROW_GROUP, n)
    row = lambda w: pl.BlockSpec((tm, w), lambda i: (i, 0))
    full = lambda a: pl.BlockSpec(a.shape, lambda i: (0,) * a.ndim)
    outs = [(DA_QK, BF16), (DA_QK, BF16), (DA_V, BF16), (G_QK, BF16), (G_QK, BF16), (G_V, BF16),
            (G_V, BF16), (G_QK, F32)]
    return pl.pallas_call(
        functools.partial(_inproj_kernel, sub=sub),
        grid=(n // tm,),
        in_specs=[row(D_MODEL), row(1), full(invf), full(g), full(wqk), full(wv), full(wg),
                  full(wga), full(wa2), full(ba)],
        out_specs=[row(w) for w, _ in outs],
        out_shape=[jax.ShapeDtypeStruct((n, w), dt) for w, dt in outs],
        compiler_params=_params(("arbitrary",)),
        name="inproj",
    )(x2d, pos2d, invf, g, wqk, wv, wg, wga, wa2, ba)


def _diffattn_kernel(lq1_ref, lk1_ref, lq2_ref, lk2_ref, gn_ref, q_ref, k_ref, v_ref, o_ref,
                     *, tq):
    seq = q_ref.shape[0]
    lam = (jnp.exp(jnp.sum(lq1_ref[...] * lk1_ref[...], axis=-1, keepdims=True))
           - jnp.exp(jnp.sum(lq2_ref[...] * lk2_ref[...], axis=-1, keepdims=True))
           + LAMBDA_INIT)
    lane = lax.broadcasted_iota(jnp.int32, (tq, LANES), 1)
    r = lax.broadcasted_iota(jnp.int32, (2 * tq, tq), 0) % tq
    c = lax.broadcasted_iota(jnp.int32, (2 * tq, tq), 1)
    causal = c <= r

    def scores(qi):
        q = q_ref[qi * tq:(qi + 1) * tq, :]
        zero = jnp.zeros_like(q)
        qs = jnp.concatenate([jnp.where(lane < DA_DIM, q, zero),
                              jnp.where(lane >= DA_DIM, q, zero)], axis=0)
        past = qi * tq
        s_diag = jnp.where(causal, _dot_nt(qs, k_ref[past:past + tq, :]), -jnp.inf)
        if qi == 0:
            return s_diag
        return jnp.concatenate([_dot_nt(qs, k_ref[0:past, :]), s_diag], axis=1)

    nq = seq // tq
    s_next = scores(0)
    for qi in range(nq):
        s = s_next
        if qi + 1 < nq:
            s_next = scores(qi + 1)
        p = jnp.exp2(s - jnp.max(s, axis=-1, keepdims=True))
        l = jnp.sum(p, axis=-1, keepdims=True)
        a = p[0:tq] - p[tq:2 * tq] * (lam * l[0:tq] / l[tq:2 * tq])
        o = _dot(a.astype(BF16), v_ref[0:(qi + 1) * tq, :]) / l[0:tq]
        o_ref[qi * tq:(qi + 1) * tq, :] = (
            _rms(o, gn_ref[...]) * (1.0 - LAMBDA_INIT)).astype(o_ref.dtype)


def _diffattn(q, k, v, lq1, lk1, lq2, lk2, gn, batch, seq, tq=SEQ_TILE):
    vec = lambda a: pl.BlockSpec(a.shape, lambda b, h: (0, 0))
    blk = pl.BlockSpec((seq, LANES), lambda b, h: (b, h))
    return pl.pallas_call(
        functools.partial(_diffattn_kernel, tq=tq),
        grid=(batch, DA_HEADS),
        in_specs=[vec(lq1), vec(lk1), vec(lq2), vec(lk2), vec(gn), blk, blk, blk],
        out_specs=blk,
        out_shape=jax.ShapeDtypeStruct(q.shape, BF16),
        compiler_params=_params(("arbitrary", "arbitrary")),
        name="diffattn",
    )(lq1, lk1, lq2, lk2, gn, q, k, v)


def _gla_kernel(gq_ref, gk_ref, gv_ref, gr_ref, la_ref, gn_ref, o_ref, st_ref, *, rows):
    c = GLA_CHUNK
    nch = rows // c
    kw = GLA_HEADS * GLA_DK
    pw = 2 * GLA_DK
    vw = 2 * GLA_DV

    @pl.when(pl.program_id(1) == 0)
    def _():
        st_ref[...] = jnp.zeros_like(st_ref)

    r_i = lax.broadcasted_iota(jnp.int32, (rows, rows), 0)
    c_i = lax.broadcasted_iota(jnp.int32, (rows, rows), 1)
    causal = (r_i // c == c_i // c) & (c_i <= r_i)
    tril = causal.astype(BF16)
    sr = lax.broadcasted_iota(jnp.int32, (pw, vw), 0) // GLA_DK
    sc = lax.broadcasted_iota(jnp.int32, (pw, vw), 1) // GLA_DV
    blockdiag = sr == sc
    head_of_lane = lax.broadcasted_iota(jnp.int32, (rows, kw), 1) // GLA_DK

    grp = range(o_ref.shape[0])
    b, qe, kn, kd, b_last, v = [], [], [], [], [], []
    for bb in grp:
        hi, mid, lo = _split3(la_ref[bb])
        b.append(_dot(tril, hi) + _dot(tril, mid) + _dot(tril, lo))
    for bb in grp:
        b3 = b[bb].reshape(nch, c, kw)
        b_last.append(b3[:, c - 1:c, :])
        qe.append(gq_ref[bb].astype(F32) * jnp.exp(b[bb]) * (GLA_DK ** -0.5))
        gk = gk_ref[bb].astype(F32)
        kn.append((gk * jnp.exp(-b[bb])).astype(BF16))
        kd.append(gk.reshape(nch, c, kw) * jnp.exp(b_last[bb] - b3))
        v.append(gv_ref[bb])
    attn = [[None] * GLA_HEADS for _ in grp]
    for h in range(GLA_HEADS):
        for bb in grp:
            qh = jnp.where(head_of_lane == h, qe[bb], 0.0).astype(BF16)
            attn[bb][h] = jnp.where(causal, _dot_nt(qh, kn[bb]), 0.0).astype(BF16)
    intra = [[None] * GLA_HEADS for _ in grp]
    for h in range(GLA_HEADS):
        for bb in grp:
            intra[bb][h] = _dot(attn[bb][h], v[bb][:, h * GLA_DV:(h + 1) * GLA_DV])

    pairs = [(bb, p) for p in range(GLA_HEADS // 2) for bb in grp]
    qeb = [q_.astype(BF16) for q_ in qe]
    upd, decay = {}, {}
    for bb, p in pairs:
        ks = slice(p * pw, (p + 1) * pw)
        vs = slice(p * vw, (p + 1) * vw)
        for ci in range(nch):
            rs = slice(ci * c, (ci + 1) * c)
            u = _dot(kd[bb][ci][:, ks].T.astype(BF16), v[bb][rs, vs])
            upd[bb, p, ci] = jnp.where(blockdiag, u, 0.0)
            decay[bb, p, ci] = jnp.exp(b_last[bb][ci][:, ks]).T
    inter = {}
    for bb, p in pairs:
        ks = slice(p * pw, (p + 1) * pw)
        st = st_ref[bb, p]
        parts = []
        for ci in range(nch):
            rs = slice(ci * c, (ci + 1) * c)
            parts.append(_dot(qeb[bb][rs, ks], st.astype(BF16)))
            st = st * decay[bb, p, ci] + upd[bb, p, ci]
        st_ref[bb, p] = st
        inter[bb, p] = jnp.concatenate(parts, axis=0)

    for bb in grp:
        for h in range(GLA_HEADS):
            hs = slice(h * GLA_DV, (h + 1) * GLA_DV)
            o = inter[bb, h // 2][:, (h % 2) * GLA_DV:(h % 2 + 1) * GLA_DV] + intra[bb][h]
            gr = gr_ref[bb, :, hs].astype(F32)
            y = _rms(o, gn_ref[...]) * (gr * jax.nn.sigmoid(gr))
            o_ref[bb, :, hs] = y.astype(o_ref.dtype)


def _gla(gq, gk, gv, gr, la, gn, batch, seq, rows=SEQ_TILE):
    nb = seq // rows
    grp = math.gcd(batch, GLA_GROUP)
    seqs = lambda a: a.reshape(batch, seq, a.shape[-1])
    blk = lambda w: pl.BlockSpec((grp, rows, w), lambda g, i: (g, i, 0))
    out = pl.pallas_call(
        functools.partial(_gla_kernel, rows=rows),
        grid=(batch // grp, nb),
        in_specs=[blk(G_QK), blk(G_QK), blk(G_V), blk(G_V), blk(G_QK),
                  pl.BlockSpec(gn.shape, lambda g, i: (0, 0))],
        out_specs=blk(G_V),
        out_shape=jax.ShapeDtypeStruct((batch, seq, gv.shape[-1]), BF16),
        scratch_shapes=[pltpu.VMEM((grp, GLA_HEADS // 2, 2 * GLA_DK, 2 * GLA_DV), F32)],
        compiler_params=_params(("arbitrary", "arbitrary")),
        name="gla",
    )(seqs(gq), seqs(gk), seqs(gv), seqs(gr), seqs(la), gn)
    return out.reshape(gv.shape)


def _memkv_kernel(m_ref, g_ref, wk_ref, wv_ref, k_ref, v_ref):
    hm = _rms(m_ref[...], g_ref[...]).astype(BF16)
    k_ref[...] = _dot(hm, wk_ref[...]).astype(BF16)
    v_ref[...] = _dot(hm, wv_ref[...]).astype(BF16)


def _memkv(mem2d, g, wk, wv, tm=SEQ_TILE):
    n = mem2d.shape[0]
    row = pl.BlockSpec((tm, D_MODEL), lambda i: (i, 0))
    full = lambda a: pl.BlockSpec(a.shape, lambda i: (0,) * a.ndim)
    return pl.pallas_call(
        _memkv_kernel,
        grid=(n // tm,),
        in_specs=[row, full(g), full(wk), full(wv)],
        out_specs=[row, row],
        out_shape=[jax.ShapeDtypeStruct((n, D_MODEL), BF16)] * 2,
        compiler_params=_params(("arbitrary",)),
        name="memkv",
    )(mem2d, g, wk, wv)


def _cross_kernel(oda_ref, ogla_ref, x_ref, kc_ref, vc_ref, wo_ref, gq_ref, wcq_ref, wco_ref,
                  g_ref, wrt_ref, br_ref,
                  x2_ref, t_ref, e_ref, gate_ref, rank_ref, cnt_ref, *, tile):
    subs = [slice(j * tile, (j + 1) * tile) for j in range(x_ref.shape[0] // tile)]
    half = oda_ref.shape[1]
    x1 = [x_ref[cs, :] + (_dot(oda_ref[cs, :], wo_ref[0:half, :])
                          + _dot(ogla_ref[cs, :], wo_ref[half:, :])) for cs in subs]
    hq = [_rms(a, gq_ref[...]).astype(BF16) for a in x1]
    qc = [(_dot(a, wcq_ref[...]) * (X_DIM ** -0.5)).astype(BF16) for a in hq]
    heads = [slice(h * X_DIM, (h + 1) * X_DIM) for h in range(X_HEADS)]
    s = [[_dot_nt(q[:, hs], kc_ref[:, hs]) for hs in heads] for q in qc]
    p = [[jnp.exp(a - jnp.max(a, axis=-1, keepdims=True)) for a in row] for row in s]
    o = [jnp.concatenate(
        [(_dot(a.astype(BF16), vc_ref[:, hs]) / jnp.sum(a, axis=-1, keepdims=True)).astype(BF16)
         for a, hs in zip(row, heads)], axis=1) for row in p]
    x2 = [a + _dot(b, wco_ref[...]) for a, b in zip(x1, o)]
    t = [_rms(a, g_ref[...]) for a in x2]
    for cs, a, b in zip(subs, x2, t):
        x2_ref[cs, :] = a
        t_ref[cs, :] = b.astype(t_ref.dtype)

    w_hi, w_mid, _ = _split3(wrt_ref[...])
    ur = lax.broadcasted_iota(jnp.int32, (tile, tile), 0)
    uc = lax.broadcasted_iota(jnp.int32, (tile, tile), 1)
    before = (ur < uc).astype(BF16)
    iota_e = lax.broadcasted_iota(jnp.int32, (N_EXPERTS, tile), 0)
    logit = []
    for a in t:
        t_hi, t_mid, _ = _split3(a)
        logit.append((_dot_nt(w_hi, t_hi) + _dot_nt(w_hi, t_mid) + _dot_nt(w_mid, t_hi))
                     + br_ref[...])
    vals, idxs, sels = [[] for _ in subs], [[] for _ in subs], [[] for _ in subs]
    for _ in range(TOP_K):
        for j in range(len(subs)):
            mx = jnp.max(logit[j], axis=0, keepdims=True)
            idx = jnp.min(jnp.where(logit[j] == mx, iota_e, N_EXPERTS), axis=0, keepdims=True)
            sel = iota_e == idx
            vals[j].append(mx)
            idxs[j].append(idx)
            sels[j].append(sel)
            logit[j] = jnp.where(sel, -jnp.inf, logit[j])
    for j, cs in enumerate(subs):
        ex = [jnp.exp(v - vals[j][0]) for v in vals[j]]
        den = ex[0] + ex[1] + ex[2] + ex[3]
        gate_ref[:, cs] = jnp.concatenate([e / den for e in ex], axis=0)
        e_ref[:, cs] = jnp.concatenate(idxs[j], axis=0)
        onehot = (sels[j][0] | sels[j][1] | sels[j][2] | sels[j][3])
        base = _dot(onehot.astype(BF16), before)
        rank_ref[:, cs] = jnp.concatenate(
            [jnp.sum(jnp.where(s_, base, 0.0), axis=0, keepdims=True) for s_ in sels[j]],
            axis=0).astype(jnp.int32)
        cnt_ref[j] = jnp.sum(onehot.astype(F32), axis=1, keepdims=True)


def _cross(oda, ogla, x2d, kc, vc, wo, gq, wcq, wco, g, wrt, br, batch, seq, mem_len, tile, tq):
    nq = seq // tq
    n = batch * seq
    row = lambda w: pl.BlockSpec((tq, w), lambda b, i: (b * nq + i, 0))
    col = lambda r: pl.BlockSpec((r, tq), lambda b, i: (0, b * nq + i))
    full = lambda a: pl.BlockSpec(a.shape, lambda b, i: (0,) * a.ndim)
    memb = pl.BlockSpec((mem_len, D_MODEL), lambda b, i: (b, 0))
    return pl.pallas_call(
        functools.partial(_cross_kernel, tile=tile),
        grid=(batch, nq),
        in_specs=[row(oda.shape[1]), row(ogla.shape[1]), row(D_MODEL), memb, memb, full(wo),
                  full(gq), full(wcq), full(wco), full(g), full(wrt), full(br)],
        out_specs=[row(D_MODEL), row(D_MODEL), col(TOP_K), col(TOP_K), col(TOP_K),
                   pl.BlockSpec((tq // tile, N_EXPERTS, 1), lambda b, i: (b * nq + i, 0, 0))],
        out_shape=[jax.ShapeDtypeStruct((n, D_MODEL), F32),
                   jax.ShapeDtypeStruct((n, D_MODEL), BF16),
                   jax.ShapeDtypeStruct((TOP_K, n), jnp.int32),
                   jax.ShapeDtypeStruct((TOP_K, n), F32),
                   jax.ShapeDtypeStruct((TOP_K, n), jnp.int32),
                   jax.ShapeDtypeStruct((n // tile, N_EXPERTS, 1), F32)],
        compiler_params=_params(("arbitrary", "arbitrary"), vmem=VMEM_LIMIT_WIDE),
        name="mix_cross_router",
    )(oda, ogla, x2d, kc, vc, wo, gq, wcq, wco, g, wrt, br)


def _run_copy(hbm_ref, tab_ref, loc_ref, sem, tile, e, s, to_hbm):
    base = (tile * N_EXPERTS + e) * 3
    n = pl.multiple_of(tab_ref[base + 2], ROW_GRAN)
    loc = loc_ref.at[s, pl.ds(pl.multiple_of(tab_ref[base + 1], ROW_GRAN), n)]
    far = hbm_ref.at[pl.ds(pl.multiple_of(tab_ref[base], ROW_GRAN), n)]
    return pltpu.make_async_copy(loc, far, sem.at[s]) if to_hbm else \
        pltpu.make_async_copy(far, loc, sem.at[s])


def _dispatch_kernel(dst_ref, nct_ref, tail_ref, slot_ref, t_ref, xp_ref, xloc_ref, zero_ref,
                     sem, zsem, usem, *, blk, mc):
    i = pl.program_id(0)
    s = i % 2
    lmax = xloc_ref.shape[1]
    n_blk = xp_ref.shape[0] // blk
    n_used = tail_ref[N_EXPERTS]

    def zero_block(j, semaphore):
        dst = xp_ref.at[pl.ds(pl.multiple_of(j * blk, blk), blk)]
        return pltpu.make_async_copy(zero_ref, dst, semaphore)

    @pl.when(i == 0)
    def _():
        zero_ref[...] = jnp.zeros_like(zero_ref)
        tails = [pl.multiple_of(tail_ref[e], blk) for e in range(N_EXPERTS)]
        for e in range(N_EXPERTS):
            pltpu.make_async_copy(zero_ref, xp_ref.at[pl.ds(tails[e], blk)], zsem).start()
        for e in range(N_EXPERTS):
            pltpu.make_async_copy(zero_ref, xp_ref.at[pl.ds(tails[e], blk)], zsem).wait()

        def start_unused(j, carry):
            zero_block(j, usem).start()
            return carry
        lax.fori_loop(n_used, n_blk, start_unused, 0)

    def compact(r0, r1):
        rows = lax.broadcasted_iota(jnp.int32, (r1 - r0, t_ref.shape[0]), 0) + r0
        hit = rows == slot_ref[0:1, :]
        for k in range(1, TOP_K):
            hit = hit | (rows == slot_ref[k:k + 1, :])
        xloc_ref[s, r0:r1] = _dot(jnp.where(hit, 1.0, 0.0).astype(BF16),
                                  t_ref[...]).astype(BF16)

    base = TOP_K * t_ref.shape[0]
    compact(0, base)
    for r0 in range(base, lmax, TAIL_ROWS):
        @pl.when(nct_ref[i] * ROW_GRAN > r0)
        def _():
            compact(r0, min(r0 + TAIL_ROWS, lmax))

    for e in range(N_EXPERTS):
        _run_copy(xp_ref, dst_ref, xloc_ref, sem, i, e, s, True).start()

    def drain(tile, slot):
        rows_out = nct_ref[tile] * ROW_GRAN
        pltpu.make_async_copy(xloc_ref.at[slot, pl.ds(0, rows_out)],
                              xp_ref.at[pl.ds(0, rows_out)], sem.at[slot]).wait()

    @pl.when(i > 0)
    def _():
        drain(i - 1, 1 - s)

    @pl.when(i == pl.num_programs(0) - 1)
    def _():
        drain(i, s)

        def wait_unused(j, carry):
            zero_block(j, usem).wait()
            return carry
        lax.fori_loop(n_used, n_blk, wait_unused, 0)


def _dispatch(dst_tab, nct, tail, slot, t, cap, blk, lmax, tm):
    n, d = t.shape
    mc = lmax // ROW_GRAN
    return pl.pallas_call(
        functools.partial(_dispatch_kernel, blk=blk, mc=mc),
        grid_spec=pltpu.PrefetchScalarGridSpec(
            num_scalar_prefetch=3,
            grid=(n // tm,),
            in_specs=[pl.BlockSpec((TOP_K, tm), lambda i, *_: (0, i)),
                      pl.BlockSpec((tm, d), lambda i, *_: (i, 0))],
            out_specs=pl.BlockSpec(memory_space=pl.ANY),
            scratch_shapes=[pltpu.VMEM((2, lmax, d), BF16), pltpu.VMEM((blk, d), BF16),
                            pltpu.SemaphoreType.DMA((2,)), pltpu.SemaphoreType.DMA(()),
                            pltpu.SemaphoreType.DMA(())]),
        out_shape=jax.ShapeDtypeStruct((cap, d), BF16),
        compiler_params=_params(("arbitrary",)),
        name="moe_dispatch",
    )(dst_tab, nct, tail, slot, t)


def _expert_kernel(blk_e_ref, nused_ref, x_ref, wu_ref, bu_ref, wd_ref, bd_ref, y_ref,
                   wub_ref, wdb_ref):
    i = pl.program_id(0)

    @pl.when((i == 0) | (blk_e_ref[i] != blk_e_ref[jnp.maximum(i - 1, 0)]))
    def _():
        wub_ref[...] = wu_ref[...].astype(BF16)
        wdb_ref[...] = wd_ref[...].astype(BF16)

    @pl.when(i < nused_ref[0])
    def _():
        u = _dot(x_ref[...], wub_ref[...]) + bu_ref[...]
        glu = jnp.minimum(u[:, :D_FF], SWIGLU_LIMIT)
        lin = jnp.clip(u[:, D_FF:], -SWIGLU_LIMIT, SWIGLU_LIMIT)
        act = glu * jax.nn.sigmoid(SWIGLU_ALPHA * glu) * (lin + 1.0)
        y_ref[...] = (_dot(act.astype(BF16), wdb_ref[...]) + bd_ref[...]).astype(y_ref.dtype)

    @pl.when(i >= nused_ref[0])
    def _():
        y_ref[...] = jnp.zeros_like(y_ref)


def _experts(blk_e, n_used, x_pad, wu, bu, wd, bd, blk):
    cap = x_pad.shape[0]
    n_blk = cap // blk
    rowmap = lambda i, be, nu: (jnp.minimum(i, nu[0] - 1), 0)
    emap = lambda i, be, nu: (be[i], 0, 0)
    return pl.pallas_call(
        _expert_kernel,
        grid_spec=pltpu.PrefetchScalarGridSpec(
            num_scalar_prefetch=2,
            grid=(n_blk,),
            in_specs=[pl.BlockSpec((blk, D_MODEL), rowmap),
                      pl.BlockSpec((None, D_MODEL, 2 * D_FF), emap),
                      pl.BlockSpec((None, 1, 2 * D_FF), emap),
                      pl.BlockSpec((None, D_FF, D_MODEL), emap),
                      pl.BlockSpec((None, 1, D_MODEL), emap)],
            out_specs=pl.BlockSpec((blk, D_MODEL), lambda i, be, nu: (i, 0)),
            scratch_shapes=[pltpu.VMEM((D_MODEL, 2 * D_FF), BF16),
                            pltpu.VMEM((D_FF, D_MODEL), BF16)]),
        out_shape=jax.ShapeDtypeStruct((cap, D_MODEL), BF16),
        compiler_params=_params(("arbitrary",), vmem=VMEM_LIMIT_WIDE),
        name="moe_experts",
    )(blk_e, n_used, x_pad, wu, bu, wd, bd)


def _combine_kernel(dst_ref, nct_ref, slot_ref, gate_ref, x2_ref, g_ref, y_ref, o_ref,
                    yloc_ref, sem, *, mc):
    i = pl.program_id(0)
    s = i % 2
    lmax = yloc_ref.shape[1]

    def issue(tile, slot):
        for e in range(N_EXPERTS):
            _run_copy(y_ref, dst_ref, yloc_ref, sem, tile, e, slot, False).start()

    @pl.when(i == 0)
    def _():
        yloc_ref[...] = jnp.zeros_like(yloc_ref)
        issue(0, 0)

    @pl.when(i + 1 < pl.num_programs(0))
    def _():
        issue(i + 1, 1 - s)

    rows_in = nct_ref[i] * ROW_GRAN
    pltpu.make_async_copy(y_ref.at[pl.ds(0, rows_in)], yloc_ref.at[s, pl.ds(0, rows_in)],
                          sem.at[s]).wait()

    cols = lax.broadcasted_iota(jnp.int32, (x2_ref.shape[0], lmax), 1)
    w = jnp.zeros(cols.shape, F32)
    for k in range(TOP_K):
        w = jnp.where(cols == slot_ref[:, k:k + 1], gate_ref[:, k:k + 1], w)
    acc = x2_ref[...] + _dot(w.astype(BF16), yloc_ref[s])
    o_ref[...] = _rms(acc, g_ref[...])


def _combine(dst_tab, nct, slot_t, gate_t, x2, g, y_pad, lmax, tm):
    n = x2.shape[0]
    mc = lmax // ROW_GRAN
    row = pl.BlockSpec((tm, D_MODEL), lambda i, *_: (i, 0))
    col = pl.BlockSpec((tm, TOP_K), lambda i, *_: (i, 0))
    return pl.pallas_call(
        functools.partial(_combine_kernel, mc=mc),
        grid_spec=pltpu.PrefetchScalarGridSpec(
            num_scalar_prefetch=2,
            grid=(n // tm,),
            in_specs=[col, col, row, pl.BlockSpec(g.shape, lambda i, *_: (0, 0)),
                      pl.BlockSpec(memory_space=pl.ANY)],
            out_specs=row,
            scratch_shapes=[pltpu.VMEM((2, lmax, D_MODEL), BF16),
                            pltpu.SemaphoreType.DMA((2,))]),
        out_shape=jax.ShapeDtypeStruct((n, D_MODEL), F32),
        compiler_params=_params(("arbitrary",)),
        name="moe_combine",
    )(dst_tab, nct, slot_t, gate_t, x2, g, y_pad)


def _rope_inv_freq():
    inv = ROPE_THETA ** (-np.arange(0, ROPE_DIM, 2, dtype=np.float32) / ROPE_DIM)
    lane = np.arange(LANES) % DA_DIM
    tab = np.where(lane < ROPE_DIM, inv.astype(np.float32)[lane % (ROPE_DIM // 2)], 0.0)
    return jnp.asarray(tab.astype(np.float32)[None, :])


def kernel(x, mem, positions, norm_mix_g, w_in, lambda_q1, lambda_k1, lambda_q2, lambda_k2, diff_norm_g, w_alpha2, b_alpha, gla_norm_g, w_out, norm_cross_g, norm_mem_g, w_cq, w_ck, w_cv, w_co, norm_ffn_g, w_router, b_router, w_up, b_up, w_down, b_down, norm_final_g):
    batch, seq, d = x.shape
    mem_len = mem.shape[1]
    n = batch * seq
    moe_blk = MOE_BLOCK
    row = lambda a: a.reshape(1, -1)

    x2d = x.reshape(n, d)
    w = w_in[0]
    c_v, c_g = 2 * DA_QK, 2 * DA_QK + DA_V
    c_a = c_g + 2 * G_QK + 2 * G_V
    wqk, wv = w[:, :c_v].astype(BF16), w[:, c_v:c_g].astype(BF16)
    wg, wga = w[:, c_g:c_a].astype(BF16), w[:, c_a:].astype(BF16)
    q, k, v, gq, gk, gv, gr, la = _inproj(
        x2d, positions.reshape(n, 1), _rope_inv_freq(), row(norm_mix_g[0]), wqk, wv, wg, wga,
        w_alpha2[0].astype(BF16), row(b_alpha[0]))

    o_da = _diffattn(q, k, v, row(lambda_q1[0]), row(lambda_k1[0]), row(lambda_q2[0]),
                     row(lambda_k2[0]), row(diff_norm_g[0]), batch, seq)
    o_gla = _gla(gq, gk, gv, gr, la, row(gla_norm_g[0]), batch, seq)

    kc, vc = _memkv(mem.reshape(batch * mem_len, d), row(norm_mem_g[0]),
                    w_ck[0].astype(BF16), w_cv[0].astype(BF16))
    tile = ROW_GROUP
    x2, t, top_e, gate, lrank, counts = _cross(
        o_da, o_gla, x2d, kc, vc, w_out[0].astype(BF16), row(norm_cross_g[0]),
        w_cq[0].astype(BF16), w_co[0].astype(BF16), row(norm_ffn_g[0]), w_router[0].T,
        b_router[0].reshape(-1, 1), batch, seq, mem_len, tile, tq=min(seq, ROW_STEP))

    nt = n // tile
    lmax = -(-(TOP_K * tile + N_EXPERTS * ROW_GRAN) // LANES) * LANES
    mc = lmax // ROW_GRAN
    cnt = counts[:, :, 0].astype(jnp.int32)
    plc = jnp.maximum((cnt + ROW_GRAN - 1) // ROW_GRAN * ROW_GRAN, ROW_GRAN)
    lend = jnp.cumsum(plc, axis=1)
    lstart = lend - plc
    tile_off = jnp.cumsum(plc, axis=0) - plc
    etot = jnp.sum(plc, axis=0)
    eblk = (etot + moe_blk - 1) // moe_blk * moe_blk
    gend = jnp.cumsum(eblk)
    dst0 = (gend - eblk)[None, :] + tile_off
    cap = nt * (TOP_K * tile + N_EXPERTS * ROW_GRAN) + N_EXPERTS * (moe_blk - 1)
    cap = -(-cap // moe_blk) * moe_blk
    n_blk = cap // moe_blk
    n_used = (gend[-1] // moe_blk).astype(jnp.int32)
    blk_ids = jnp.minimum(jnp.arange(n_blk, dtype=jnp.int32), n_used - 1)
    blk_e = jnp.minimum(jnp.sum(gend[None, :] <= (blk_ids * moe_blk)[:, None], axis=1),
                        N_EXPERTS - 1).astype(jnp.int32)
    tail = jnp.where(etot > 0, gend, gend[-1]).astype(jnp.int32) - moe_blk
    tail = jnp.concatenate([tail, n_used.reshape(1)])
    eids = jnp.arange(N_EXPERTS, dtype=jnp.int32)[:, None, None]
    lstart_tok = jnp.repeat(lstart.T, tile, axis=1)[:, None, :]
    slot = (lrank + jnp.sum(jnp.where(top_e[None] == eids, lstart_tok, 0), axis=0)
            ).astype(jnp.int32)
    dst_tab = jnp.stack([dst0, lstart, plc], axis=-1).astype(jnp.int32).reshape(-1)
    nct = (lend[:, -1] // ROW_GRAN).astype(jnp.int32)

    x_pad = _dispatch(dst_tab, nct, tail, slot, t, cap, moe_blk, lmax, tile)
    y_pad = _experts(blk_e, n_used.reshape(1), x_pad, w_up[0], b_up[0][:, None, :], w_down[0],
                     b_down[0][:, None, :], moe_blk)
    out = _combine(dst_tab, nct, slot.T, gate.T, x2, row(norm_final_g), y_pad, lmax, tile)
    return out.reshape(batch, seq, d)
```
